```python
import jax, jax.numpy as jnp
from jax import lax
import numpy as np


D_MODEL = 1024
BATCH = 16
SEQ = 2048
DEPTH = 4

GRID_W = 64
CTX_LEN = 256
HEAD_DIM = 64
HQ_A = D_MODEL // (2 * HEAD_DIM)
HKV_A = HQ_A // 4
HQ_B = D_MODEL // (2 * HEAD_DIM)
HKV_B = HQ_B // 4
Q_BLOCK = 128
WINDOW = 128
KEY_SPAN = Q_BLOCK + 2 * WINDOW
ROPE_THETA = 10000.0
ROPE_AXIS_DIM = HEAD_DIM // 2
ATT_WIDTHS = (HQ_A * HEAD_DIM, HKV_A * HEAD_DIM, HKV_A * HEAD_DIM,
              HQ_B * HEAD_DIM, HKV_B * HEAD_DIM, HKV_B * HEAD_DIM)
ATT_SPLITS = tuple(int(s) for s in np.cumsum(ATT_WIDTHS)[:-1])
ATT_IN_WIDTH = sum(ATT_WIDTHS)
ATT_OUT_WIDTH = (HQ_A + HQ_B) * HEAD_DIM
CONV_DIM = D_MODEL
CONV_WIDTH = 3
N_EXPERTS = 16
EXPERT_FF = D_MODEL
CAPACITY_FACTOR = 2
N_ADA = 6
N_ATT_LAYERS = (DEPTH + 1) // 2
N_CONV_LAYERS = DEPTH // 2
EPS = 1e-6
NEG_INF = -1e30

kernel_name = 'hybrid_dit_attn_window_shortconv_ecmoe'


def _rmsnorm(x, g):
    xf = x.astype(jnp.float32)
    xf = xf * lax.rsqrt(jnp.mean(xf * xf, axis=-1, keepdims=True) + EPS)
    return xf.astype(x.dtype) * g


def _modulate(h, shift, scale):
    return h * (1.0 + scale) + shift


def _axial_rope_tables(n_tokens):
    rows = n_tokens // GRID_W
    row = jnp.repeat(jnp.arange(rows, dtype=jnp.float32), GRID_W)
    col = jnp.tile(jnp.arange(GRID_W, dtype=jnp.float32), rows)
    inv_freq = ROPE_THETA ** (-jnp.arange(0, ROPE_AXIS_DIM, 2, dtype=jnp.float32) / ROPE_AXIS_DIM)
    ang_r = row[:, None] * inv_freq[None, :]
    ang_c = col[:, None] * inv_freq[None, :]
    return (jnp.cos(ang_r), jnp.sin(ang_r), jnp.cos(ang_c), jnp.sin(ang_c))


def _rotate(x, cos, sin):
    a, b = jnp.split(x, 2, axis=-1)
    cos = cos[None, :, None, :]
    sin = sin[None, :, None, :]
    return jnp.concatenate([a * cos - b * sin, a * sin + b * cos], axis=-1)


def _apply_axial_rope(x, rope):
    cos_r, sin_r, cos_c, sin_c = rope
    xr, xc = jnp.split(x.astype(jnp.float32), 2, axis=-1)
    return jnp.concatenate([_rotate(xr, cos_r, sin_r), _rotate(xc, cos_c, sin_c)], axis=-1).astype(x.dtype)


def _project_heads(h, w_in):
    p = h @ w_in
    qa, ka, va, qb, kb, vb = jnp.split(p, ATT_SPLITS, axis=-1)
    B, n = h.shape[0], h.shape[1]
    shp = lambda t: t.reshape(B, n, -1, HEAD_DIM)
    return shp(qa), shp(ka), shp(va), shp(qb), shp(kb), shp(vb)


def _sink_column(sink, hkv, g, B, nq):
    return jnp.broadcast_to(sink.astype(jnp.float32).reshape(1, hkv, g, 1, 1), (B, hkv, g, nq, 1))


def _global_attention_latent(q, k, v, k_ctx, v_ctx):
    B, n, hq, hd = q.shape
    hkv = k.shape[2]
    g = hq // hkv
    nb = n // Q_BLOCK
    qblk = (q * hd ** -0.5).reshape(B, nb, Q_BLOCK, hkv, g, hd).transpose(1, 0, 2, 3, 4, 5)

    def one_block(qb):
        s = jnp.concatenate([jnp.einsum('bqhgd,bkhd->bhgqk', qb, k),
                             jnp.einsum('bqhgd,bkhd->bhgqk', qb, k_ctx)], axis=-1)
        p = jax.nn.softmax(s.astype(jnp.float32), axis=-1).astype(v.dtype)
        return (jnp.einsum('bhgqk,bkhd->bqhgd', p[..., :n], v)
                + jnp.einsum('bhgqk,bkhd->bqhgd', p[..., n:], v_ctx))

    o = lax.map(one_block, qblk)
    return o.transpose(1, 0, 2, 3, 4, 5).reshape(B, n, hq * hd)


def _window_attention_latent(q, k, v, k_ctx, v_ctx, sink):
    B, n, hq, hd = q.shape
    hkv = k.shape[2]
    g = hq // hkv
    nb = n // Q_BLOCK
    qblk = (q * hd ** -0.5).reshape(B, nb, Q_BLOCK, hkv, g, hd).transpose(1, 0, 2, 3, 4, 5)
    k_pad = jnp.pad(k, ((0, 0), (WINDOW, WINDOW), (0, 0), (0, 0)))
    v_pad = jnp.pad(v, ((0, 0), (WINDOW, WINDOW), (0, 0), (0, 0)))
    a_idx = jnp.arange(Q_BLOCK)[:, None]
    b_idx = jnp.arange(KEY_SPAN)[None, :]
    band = (b_idx >= a_idx) & (b_idx <= a_idx + 2 * WINDOW)
    sink_col = _sink_column(sink, hkv, g, B, Q_BLOCK)
    L = k_ctx.shape[1]

    def one_block(args):
        blk, qb = args
        start = blk * Q_BLOCK
        kw = lax.dynamic_slice_in_dim(k_pad, start, KEY_SPAN, axis=1)
        vw = lax.dynamic_slice_in_dim(v_pad, start, KEY_SPAN, axis=1)
        j = start - WINDOW + jnp.arange(KEY_SPAN)
        mask = band & ((j >= 0) & (j < n))[None, :]
        s_w = jnp.einsum('bqhgd,bkhd->bhgqk', qb, kw).astype(jnp.float32)
        s_w = jnp.where(mask, s_w, NEG_INF)
        s_c = jnp.einsum('bqhgd,bkhd->bhgqk', qb, k_ctx).astype(jnp.float32)
        s = jnp.concatenate([s_w, s_c, sink_col], axis=-1)
        p = jax.nn.softmax(s, axis=-1).astype(v.dtype)
        return (jnp.einsum('bhgqk,bkhd->bqhgd', p[..., :KEY_SPAN], vw)
                + jnp.einsum('bhgqk,bkhd->bqhgd', p[..., KEY_SPAN:KEY_SPAN + L], v_ctx))

    o = lax.map(one_block, (jnp.arange(nb), qblk))
    return o.transpose(1, 0, 2, 3, 4, 5).reshape(B, n, hq * hd)


def _context_attention(q, k, v, sink=None):
    B, L, hq, hd = q.shape
    hkv = k.shape[2]
    g = hq // hkv
    qg = (q * hd ** -0.5).reshape(B, L, hkv, g, hd)
    s = jnp.einsum('bqhgd,bkhd->bhgqk', qg, k).astype(jnp.float32)
    if sink is not None:
        s = jnp.concatenate([s, _sink_column(sink, hkv, g, B, L)], axis=-1)
    p = jax.nn.softmax(s, axis=-1).astype(v.dtype)[..., :L]
    return jnp.einsum('bhgqk,bkhd->bqhgd', p, v).reshape(B, L, hq * hd)


def _hybrid_attention(h, hc, w_in, w_out, qg_a, kg_a, qg_b, kg_b, sink, rope, need_ctx):
    qa, ka, va, qb, kb, vb = _project_heads(h, w_in)
    qac, kac, vac, qbc, kbc, vbc = _project_heads(hc, w_in)
    qa = _apply_axial_rope(_rmsnorm(qa, qg_a), rope)
    ka = _apply_axial_rope(_rmsnorm(ka, kg_a), rope)
    qb = _apply_axial_rope(_rmsnorm(qb, qg_b), rope)
    kb = _apply_axial_rope(_rmsnorm(kb, kg_b), rope)
    kac = _rmsnorm(kac, kg_a)
    kbc = _rmsnorm(kbc, kg_b)
    o_a = _global_attention_latent(qa, ka, va, kac, vac)
    o_b = _window_attention_latent(qb, kb, vb, kbc, vbc, sink)
    y = jnp.concatenate([o_a, o_b], axis=-1) @ w_out
    yc = None
    if need_ctx:
        oc_a = _context_attention(_rmsnorm(qac, qg_a), kac, vac)
        oc_b = _context_attention(_rmsnorm(qbc, qg_b), kbc, vbc, sink)
        yc = jnp.concatenate([oc_a, oc_b], axis=-1) @ w_out
    return y, yc


def _short_conv(h, w_in, conv_k, conv_b, w_out):
    bg, cg, v = jnp.split(h @ w_in, 3, axis=-1)
    u = cg * v
    y = lax.conv_general_dilated(u, conv_k[:, None, :].astype(u.dtype), window_strides=(1,),
                                 padding=((CONV_WIDTH // 2, CONV_WIDTH // 2),),
                                 dimension_numbers=('NWC', 'WIO', 'NWC'),
                                 feature_group_count=u.shape[-1]) + conv_b
    return (bg * y) @ w_out


def _expert_choice_moe(h, router_w, w_gate, w_up, w_down):
    B, n, _ = h.shape
    cap = CAPACITY_FACTOR * n // N_EXPERTS
    aff = jax.nn.softmax(jnp.einsum('bnd,de->bne', h, router_w).astype(jnp.float32), axis=-1)
    gate, idx = lax.top_k(aff.transpose(0, 2, 1), cap)
    bidx = jnp.arange(B)[:, None, None]
    xe = h[bidx, idx]
    hid = jax.nn.silu(jnp.einsum('becd,edf->becf', xe, w_gate)) * jnp.einsum('becd,edf->becf', xe, w_up)
    ye = jnp.einsum('becf,efd->becd', hid, w_down) * gate[..., None].astype(h.dtype)
    return jnp.zeros_like(h).at[bidx, idx].add(ye)


def setup_inputs(seed: int = 0) -> dict:
    key = jax.random.key(seed)
    ks = jax.random.split(key, 24)
    f32 = jnp.float32
    nrm = lambda k, shape, s: jax.random.normal(k, shape, f32) * s
    D, F = D_MODEL, EXPERT_FF
    return {
        'x': nrm(ks[0], (BATCH, SEQ, D), 1.0),
        'c': nrm(ks[1], (BATCH, D), 1.0),
        'ctx': nrm(ks[2], (BATCH, CTX_LEN, D), 1.0),
        'c_ctx': nrm(ks[3], (D,), 1.0),
        'ada_w': nrm(ks[4], (DEPTH, D, N_ADA * D), 0.5 * D ** -0.5),
        'ada_b': nrm(ks[5], (DEPTH, N_ADA * D), 0.02),
        'norm1_g': 1.0 + nrm(ks[6], (DEPTH, D), 0.1),
        'norm2_g': 1.0 + nrm(ks[7], (DEPTH, D), 0.1),
        'attn_w_in': nrm(ks[8], (N_ATT_LAYERS, D, ATT_IN_WIDTH), D ** -0.5),
        'attn_w_out': nrm(ks[9], (N_ATT_LAYERS, ATT_OUT_WIDTH, D), ATT_OUT_WIDTH ** -0.5),
        'qnorm_a': 1.0 + nrm(ks[10], (N_ATT_LAYERS, HEAD_DIM), 0.1),
        'knorm_a': 1.0 + nrm(ks[11], (N_ATT_LAYERS, HEAD_DIM), 0.1),
        'qnorm_b': 1.0 + nrm(ks[12], (N_ATT_LAYERS, HEAD_DIM), 0.1),
        'knorm_b': 1.0 + nrm(ks[13], (N_ATT_LAYERS, HEAD_DIM), 0.1),
        'sink_b': nrm(ks[14], (N_ATT_LAYERS, HQ_B), 0.5),
        'conv_w_in': nrm(ks[15], (N_CONV_LAYERS, D, 3 * CONV_DIM), D ** -0.5),
        'conv_k': nrm(ks[16], (N_CONV_LAYERS, CONV_WIDTH, CONV_DIM), CONV_WIDTH ** -0.5),
        'conv_b': nrm(ks[17], (N_CONV_LAYERS, CONV_DIM), 0.02),
        'conv_w_out': nrm(ks[18], (N_CONV_LAYERS, CONV_DIM, D), CONV_DIM ** -0.5),
        'router_w': nrm(ks[19], (DEPTH, D, N_EXPERTS), D ** -0.5),
        'moe_w_gate': nrm(ks[20], (DEPTH, N_EXPERTS, D, F), D ** -0.5),
        'moe_w_up': nrm(ks[21], (DEPTH, N_EXPERTS, D, F), D ** -0.5),
        'moe_w_down': nrm(ks[22], (DEPTH, N_EXPERTS, F, D), F ** -0.5),
    }


def reference(x, c, ctx, c_ctx, ada_w, ada_b, norm1_g, norm2_g, attn_w_in, attn_w_out,
              qnorm_a, knorm_a, qnorm_b, knorm_b, sink_b, conv_w_in, conv_k, conv_b, conv_w_out,
              router_w, moe_w_gate, moe_w_up, moe_w_down):
    rope = _axial_rope_tables(x.shape[1])
    silu_c = jax.nn.silu(c)
    silu_cc = jax.nn.silu(c_ctx)
    for i in range(DEPTH):
        need_ctx = i < DEPTH - 1
        j = i // 2
        mod = silu_c @ ada_w[i] + ada_b[i]
        mod_c = silu_cc @ ada_w[i] + ada_b[i]
        sh1, sc1, g1, sh2, sc2, g2 = [m[:, None, :] for m in jnp.split(mod, N_ADA, axis=-1)]
        sh1c, sc1c, g1c, sh2c, sc2c, g2c = jnp.split(mod_c, N_ADA, axis=-1)
        h = _modulate(_rmsnorm(x, norm1_g[i]), sh1, sc1)
        hc = _modulate(_rmsnorm(ctx, norm1_g[i]), sh1c, sc1c)
        if i % 2 == 0:
            y, yc = _hybrid_attention(h, hc, attn_w_in[j], attn_w_out[j], qnorm_a[j], knorm_a[j],
                                      qnorm_b[j], knorm_b[j], sink_b[j], rope, need_ctx)
        else:
            y = _short_conv(h, conv_w_in[j], conv_k[j], conv_b[j], conv_w_out[j])
            yc = _short_conv(hc, conv_w_in[j], conv_k[j], conv_b[j], conv_w_out[j]) if need_ctx else None
        x = x + g1 * y
        x = x + g2 * _expert_choice_moe(_modulate(_rmsnorm(x, norm2_g[i]), sh2, sc2),
                                        router_w[i], moe_w_gate[i], moe_w_up[i], moe_w_down[i])
        if need_ctx:
            ctx = ctx + g1c * yc
            ctx = ctx + g2c * _expert_choice_moe(_modulate(_rmsnorm(ctx, norm2_g[i]), sh2c, sc2c),
                                                 router_w[i], moe_w_gate[i], moe_w_up[i], moe_w_down[i])
    return x
```

```python
import functools

import jax
import jax.numpy as jnp
import numpy as np
from jax import lax
from jax.experimental import pallas as pl
from jax.experimental.pallas import tpu as pltpu

F32 = jnp.float32
BF16 = jnp.bfloat16
I32 = jnp.int32

D_MODEL = 1024
BATCH = 16
SEQ = 2048
CTX_LEN = 256
TOK = SEQ + CTX_LEN
DEPTH = 4
GRID_W = 64
HEAD_DIM = 64
N_Q_HEADS = 8
N_KV_HEADS = 2
Q_WIDTH = N_Q_HEADS * HEAD_DIM
KV_WIDTH = N_KV_HEADS * HEAD_DIM
ATT_IN_WIDTH = 2 * (Q_WIDTH + 2 * KV_WIDTH)
WINDOW = 128
ROPE_THETA = 10000.0
N_EXPERTS = 16
CAP_LAT = 2 * SEQ // N_EXPERTS
CAP_CTX = 2 * CTX_LEN // N_EXPERTS
N_ADA = 6
EPS = 1e-6
NEG_INF = -1e30

LANES = 128
ROW_TILE = 256
N_ROW_TILES = TOK // ROW_TILE
N_LAT_ROW_TILES = SEQ // ROW_TILE
Q_TILE = 128
N_Q_TILES = TOK // Q_TILE
N_LAT_Q_TILES = SEQ // Q_TILE
WIN_SPAN = Q_TILE + 2 * WINDOW
HALO = 16
PREFIX_CHUNK = 256
VMEM_LIMIT = 56 * 1024 * 1024

HIGHEST = lax.Precision.HIGHEST


def _dot(a, b, precision=None):
    return jnp.dot(a, b, preferred_element_type=F32, precision=precision)


def _dot_nt(a, b, precision=None):
    return lax.dot_general(a, b, (((1,), (1,)), ((), ())), preferred_element_type=F32,
                           precision=precision)


def _dot_tn(a, b):
    return lax.dot_general(a, b, (((0,), (0,)), ((), ())), preferred_element_type=F32)


def _rmsnorm_rows(x, g):
    return x * lax.rsqrt(jnp.mean(x * x, axis=-1, keepdims=True) + EPS) * g


def _silu(x):
    return x * (1.0 / (1.0 + jnp.exp(-x)))


def _adaln_kernel(c_ref, w_ref, b_ref, o_ref):
    o_ref[0] = _dot(_silu(c_ref[...]), w_ref[0], precision=HIGHEST) + b_ref[0]


def _adaln(c_all, ada_w, ada_b):
    rows = c_all.shape[0]
    return pl.pallas_call(
        _adaln_kernel,
        grid=(DEPTH, N_ADA),
        in_specs=[
            pl.BlockSpec((rows, D_MODEL), lambda i, j: (0, 0)),
            pl.BlockSpec((1, D_MODEL, D_MODEL), lambda i, j: (i, 0, j)),
            pl.BlockSpec((1, 1, D_MODEL), lambda i, j: (i, 0, j)),
        ],
        out_specs=pl.BlockSpec((1, rows, D_MODEL), lambda i, j: (i, 0, j)),
        out_shape=jax.ShapeDtypeStruct((DEPTH, rows, N_ADA * D_MODEL), F32),
        name="adaln",
    )(c_all, ada_w, ada_b.reshape(DEPTH, 1, N_ADA * D_MODEL))


def _mod_row(b, t):
    return jnp.where(t >= N_LAT_ROW_TILES, BATCH, b)


def _attn_in_kernel(x_ref, mod_ref, g_ref, w_ref, qkg_ref, cos_ref, sin_ref, q_ref, kv_ref):
    x = x_ref[0]
    h = _rmsnorm_rows(x, g_ref[...])
    h = h * (1.0 + mod_ref[1:2, :]) + mod_ref[0:1, :]
    p = _dot(h.astype(BF16), w_ref[...])

    lane = lax.broadcasted_iota(I32, (ROW_TILE, LANES), 1)
    r = lax.broadcasted_iota(I32, (LANES, LANES), 0)
    c = lax.broadcasted_iota(I32, (LANES, LANES), 1)
    seg_mean = jnp.where((r >> 6) == (c >> 6), 1.0 / HEAD_DIM, 0.0).astype(BF16)
    cos = cos_ref[...]
    sin = sin_ref[...]
    first_half = (lane & (HEAD_DIM // 2 - 1)) < (HEAD_DIM // 4)
    low = lane < HEAD_DIM

    def qk_norm_rope(xc, g):
        sq = xc * xc
        sq_hi = sq.astype(BF16)
        sq_lo = (sq - sq_hi.astype(F32)).astype(BF16)
        ms = _dot(sq_hi, seg_mean) + _dot(sq_lo, seg_mean)
        xn = xc * lax.rsqrt(ms + EPS) * g
        partner = jnp.where(first_half, pltpu.roll(xn, LANES - HEAD_DIM // 4, axis=1),
                            pltpu.roll(xn, HEAD_DIM // 4, axis=1))
        return xn * cos + partner * sin

    def put_kv(kind, base, vals):
        rolled = pltpu.roll(vals, HEAD_DIM, axis=1)
        kv_ref[0, kind, 0, base + 0] = jnp.where(low, vals, 0.0).astype(BF16)
        kv_ref[0, kind, 0, base + 1] = jnp.where(low, 0.0, rolled).astype(BF16)
        kv_ref[0, kind, 1, base + 0] = jnp.where(low, rolled, 0.0).astype(BF16)
        kv_ref[0, kind, 1, base + 1] = jnp.where(low, 0.0, vals).astype(BF16)

    for kind in range(2):
        col0 = kind * (Q_WIDTH + 2 * KV_WIDTH)
        gq = qkg_ref[2 * kind:2 * kind + 1, :]
        gk = qkg_ref[2 * kind + 1:2 * kind + 2, :]
        for ci in range(Q_WIDTH // LANES):
            qc = qk_norm_rope(p[:, col0 + ci * LANES:col0 + (ci + 1) * LANES], gq)
            q_ref[0, :, kind * Q_WIDTH + ci * LANES:kind * Q_WIDTH + (ci + 1) * LANES] = (
                qc * HEAD_DIM ** -0.5).astype(BF16)
        kc = qk_norm_rope(p[:, col0 + Q_WIDTH:col0 + Q_WIDTH + KV_WIDTH], gk)
        put_kv(kind, 0, kc)
        put_kv(kind, 2, p[:, col0 + Q_WIDTH + KV_WIDTH:col0 + Q_WIDTH + 2 * KV_WIDTH])


def _attn_in(xs, mod, layer, norm_g, w_in, qk_g, cos_t, sin_t):
    return pl.pallas_call(
        _attn_in_kernel,
        grid=(BATCH, N_ROW_TILES),
        in_specs=[
            pl.BlockSpec((1, ROW_TILE, D_MODEL), lambda b, t: (b, t, 0)),
            pl.BlockSpec((None, None, N_ADA, D_MODEL), lambda b, t: (layer, _mod_row(b, t), 0, 0)),
            pl.BlockSpec((1, D_MODEL), lambda b, t: (0, 0)),
            pl.BlockSpec((D_MODEL, ATT_IN_WIDTH), lambda b, t: (0, 0)),
            pl.BlockSpec((4, LANES), lambda b, t: (0, 0)),
            pl.BlockSpec((ROW_TILE, LANES), lambda b, t: (t, 0)),
            pl.BlockSpec((ROW_TILE, LANES), lambda b, t: (t, 0)),
        ],
        out_specs=[
            pl.BlockSpec((1, ROW_TILE, 2 * Q_WIDTH), lambda b, t: (b, t, 0)),
            pl.BlockSpec((1, 2, N_KV_HEADS, 4, ROW_TILE, LANES), lambda b, t: (b, 0, 0, 0, t, 0)),
        ],
        out_shape=[
            jax.ShapeDtypeStruct((BATCH, TOK, 2 * Q_WIDTH), BF16),
            jax.ShapeDtypeStruct((BATCH, 2, N_KV_HEADS, 4, TOK, LANES), BF16),
        ],
        compiler_params=pltpu.CompilerParams(vmem_limit_bytes=VMEM_LIMIT),
        name="attn_in",
    )(xs, mod, norm_g, w_in, qk_g, cos_t, sin_t)


def _softmax_pv(parts, sink=None):
    m = None
    for s, _ in parts:
        mx = jnp.max(s, axis=-1, keepdims=True)
        m = mx if m is None else jnp.maximum(m, mx)
    if sink is not None:
        m = jnp.maximum(m, sink)
    l = jnp.exp(sink - m) if sink is not None else None
    acc = None
    for s, v in parts:
        pr = jnp.exp(s - m)
        ls = jnp.sum(pr, axis=-1, keepdims=True)
        l = ls if l is None else l + ls
        pv = _dot(pr.astype(BF16), v)
        acc = pv if acc is None else acc + pv
    return acc * (1.0 / l)


def _attn_kernel(sink_ref, q_ref, kv_ref, o_ref):
    t = pl.program_id(1)
    half = lax.broadcasted_iota(I32, (2 * Q_TILE, 1), 0) < Q_TILE

    def q_rows(kind, g):
        c0 = kind * Q_WIDTH + g * 2 * LANES
        return jnp.concatenate([q_ref[0, :, c0:c0 + LANES], q_ref[0, :, c0 + LANES:c0 + 2 * LANES]],
                               axis=0)

    def put(kind, g, o):
        c0 = kind * Q_WIDTH + g * 2 * LANES
        o_ref[0, :, c0:c0 + LANES] = o[:Q_TILE].astype(BF16)
        o_ref[0, :, c0 + LANES:c0 + 2 * LANES] = o[Q_TILE:].astype(BF16)

    def sink_col(g, parity):
        return jnp.where(half, sink_ref[4 * g + parity], sink_ref[4 * g + 2 + parity])

    @pl.when(t < N_LAT_Q_TILES)
    def _latent():
        for g in range(N_KV_HEADS):
            lhs = q_rows(0, g)
            o = None
            for parity in range(2):
                s = _dot_nt(lhs, kv_ref[0, 0, g, parity])
                op = _softmax_pv([(s, kv_ref[0, 0, g, 2 + parity])])
                o = op if o is None else o + op
            put(0, g, o)

        ks = pl.multiple_of(jnp.clip(t * Q_TILE - WINDOW, 0, SEQ - WIN_SPAN), Q_TILE)
        qi = lax.broadcasted_iota(I32, (2 * Q_TILE, WIN_SPAN), 0) & (Q_TILE - 1)
        kj = lax.broadcasted_iota(I32, (2 * Q_TILE, WIN_SPAN), 1)
        band = jnp.abs(qi - kj + (t * Q_TILE - ks)) <= WINDOW
        for g in range(N_KV_HEADS):
            lhs = q_rows(1, g)
            o = None
            for parity in range(2):
                s_w = _dot_nt(lhs, kv_ref[0, 1, g, parity, pl.ds(ks, WIN_SPAN), :])
                s_w = jnp.where(band, s_w, NEG_INF)
                s_c = _dot_nt(lhs, kv_ref[0, 1, g, parity, SEQ:TOK, :])
                op = _softmax_pv(
                    [(s_w, kv_ref[0, 1, g, 2 + parity, pl.ds(ks, WIN_SPAN), :]),
                     (s_c, kv_ref[0, 1, g, 2 + parity, SEQ:TOK, :])],
                    sink=sink_col(g, parity))
                o = op if o is None else o + op
            put(1, g, o)

    @pl.when(t >= N_LAT_Q_TILES)
    def _context():
        for kind in range(2):
            for g in range(N_KV_HEADS):
                lhs = q_rows(kind, g)
                o = None
                for parity in range(2):
                    s_c = _dot_nt(lhs, kv_ref[0, kind, g, parity, SEQ:TOK, :])
                    op = _softmax_pv([(s_c, kv_ref[0, kind, g, 2 + parity, SEQ:TOK, :])],
                                     sink=sink_col(g, parity) if kind == 1 else None)
                    o = op if o is None else o + op
                put(kind, g, o)


def _attention(q, kv, sink, n_q_tiles):
    return pl.pallas_call(
        _attn_kernel,
        grid=(BATCH, n_q_tiles),
        in_specs=[
            pl.BlockSpec(memory_space=pltpu.SMEM),
            pl.BlockSpec((1, Q_TILE, 2 * Q_WIDTH), lambda b, t: (b, t, 0)),
            pl.BlockSpec((1, 2, N_KV_HEADS, 4, TOK, LANES), lambda b, t: (b, 0, 0, 0, 0, 0)),
        ],
        out_specs=pl.BlockSpec((1, Q_TILE, 2 * Q_WIDTH), lambda b, t: (b, t, 0)),
        out_shape=jax.ShapeDtypeStruct((BATCH, TOK, 2 * Q_WIDTH), BF16),
        compiler_params=pltpu.CompilerParams(vmem_limit_bytes=VMEM_LIMIT),
        name="attention",
    )(sink, q, kv)


def _conv_in_kernel(x_ref, mod_ref, g_ref, w_ref, u_ref, bg_ref):
    x = x_ref[0]
    h = _rmsnorm_rows(x, g_ref[...])
    h = (h * (1.0 + mod_ref[1:2, :]) + mod_ref[0:1, :]).astype(BF16)
    bg_ref[0] = _dot(h, w_ref[:, 0:D_MODEL]).astype(BF16)
    cg = _dot(h, w_ref[:, D_MODEL:2 * D_MODEL])
    v = _dot(h, w_ref[:, 2 * D_MODEL:3 * D_MODEL])
    u_ref[0] = (cg * v).astype(BF16)


def _conv_in(xs, mod, layer, norm_g, w_in):
    return pl.pallas_call(
        _conv_in_kernel,
        grid=(BATCH, N_ROW_TILES),
        in_specs=[
            pl.BlockSpec((1, ROW_TILE, D_MODEL), lambda b, t: (b, t, 0)),
            pl.BlockSpec((None, None, N_ADA, D_MODEL), lambda b, t: (layer, _mod_row(b, t), 0, 0)),
            pl.BlockSpec((1, D_MODEL), lambda b, t: (0, 0)),
            pl.BlockSpec((D_MODEL, 3 * D_MODEL), lambda b, t: (0, 0)),
        ],
        out_specs=[
            pl.BlockSpec((1, ROW_TILE, D_MODEL), lambda b, t: (b, t, 0)),
            pl.BlockSpec((1, ROW_TILE, D_MODEL), lambda b, t: (b, t, 0)),
        ],
        out_shape=[
            jax.ShapeDtypeStruct((BATCH, TOK, D_MODEL), BF16),
            jax.ShapeDtypeStruct((BATCH, TOK, D_MODEL), BF16),
        ],
        compiler_params=pltpu.CompilerParams(vmem_limit_bytes=VMEM_LIMIT),
        name="conv_in",
    )(xs, mod, norm_g, w_in)


def _tail(lhs, w_ref, x_ref, mod_ref, g_ref, rw_ref, x1_ref, h2_ref, aff_ref):
    x1 = x_ref[0] + mod_ref[2:3, :] * _dot(lhs, w_ref[...])
    x1_ref[0] = x1
    h2 = _rmsnorm_rows(x1, g_ref[...])
    h2 = h2 * (1.0 + mod_ref[4:5, :]) + mod_ref[3:4, :]
    h2_ref[0] = h2.astype(BF16)
    logits = _dot_nt(rw_ref[...], h2, precision=HIGHEST)
    e = jnp.exp(logits - jnp.max(logits, axis=0, keepdims=True))
    aff_ref[0] = e / jnp.sum(e, axis=0, keepdims=True)


def _attn_out_kernel(o_ref, w_ref, x_ref, mod_ref, g_ref, rw_ref, x1_ref, h2_ref, aff_ref):
    _tail(o_ref[0], w_ref, x_ref, mod_ref, g_ref, rw_ref, x1_ref, h2_ref, aff_ref)


def _conv_out_kernel(u_ref, up_ref, un_ref, bg_ref, ck_ref, cb_ref, w_ref, x_ref, mod_ref, g_ref,
                     rw_ref, x1_ref, h2_ref, aff_ref):
    t = pl.program_id(1)
    u = u_ref[0].astype(F32)
    has_prev = jnp.logical_and(t != 0, t != N_LAT_ROW_TILES)
    has_next = t < N_LAT_ROW_TILES - 1
    prev_row = jnp.where(has_prev, up_ref[0, HALO - 1:HALO, :].astype(F32), 0.0)
    next_row = jnp.where(has_next, un_ref[0, 0:1, :].astype(F32), 0.0)
    row = lax.broadcasted_iota(I32, (ROW_TILE, D_MODEL), 0)
    u_prev = jnp.where(row == 0, prev_row, pltpu.roll(u, 1, axis=0))
    u_next = jnp.where(row == ROW_TILE - 1, next_row, pltpu.roll(u, ROW_TILE - 1, axis=0))
    y = ck_ref[0:1, :] * u_prev + ck_ref[1:2, :] * u + ck_ref[2:3, :] * u_next + cb_ref[...]
    lhs = (bg_ref[0].astype(F32) * y).astype(BF16)
    _tail(lhs, w_ref, x_ref, mod_ref, g_ref, rw_ref, x1_ref, h2_ref, aff_ref)


def _mixer_out(xs, mod, layer, w_out, norm_g, rw_t, n_row_tiles, attn_o=None, conv=None):
    tile = pl.BlockSpec((1, ROW_TILE, D_MODEL), lambda b, t: (b, t, 0))
    common_specs = [
        pl.BlockSpec((D_MODEL, D_MODEL), lambda b, t: (0, 0)),
        tile,
        pl.BlockSpec((None, None, N_ADA, D_MODEL), lambda b, t: (layer, _mod_row(b, t), 0, 0)),
        pl.BlockSpec((1, D_MODEL), lambda b, t: (0, 0)),
        pl.BlockSpec((N_EXPERTS, D_MODEL), lambda b, t: (0, 0)),
    ]
    if conv is None:
        body, specs, args = _attn_out_kernel, [tile], [attn_o]
    else:
        u, bg, ck, cb = conv
        halo_per_tile = ROW_TILE // HALO
        last_halo = TOK // HALO - 1
        body = _conv_out_kernel
        specs = [
            tile,
            pl.BlockSpec((1, HALO, D_MODEL), lambda b, t: (b, jnp.maximum(t * halo_per_tile - 1, 0), 0)),
            pl.BlockSpec((1, HALO, D_MODEL),
                         lambda b, t: (b, jnp.minimum((t + 1) * halo_per_tile, last_halo), 0)),
            tile,
            pl.BlockSpec((3, D_MODEL), lambda b, t: (0, 0)),
            pl.BlockSpec((1, D_MODEL), lambda b, t: (0, 0)),
        ]
        args = [u, u, u, bg, ck, cb]
    return pl.pallas_call(
        body,
        grid=(BATCH, n_row_tiles),
        in_specs=specs + common_specs,
        out_specs=[
            tile,
            tile,
            pl.BlockSpec((1, N_EXPERTS, ROW_TILE), lambda b, t: (b, 0, t)),
        ],
        out_shape=[
            jax.ShapeDtypeStruct((BATCH, TOK, D_MODEL), F32),
            jax.ShapeDtypeStruct((BATCH, TOK, D_MODEL), BF16),
            jax.ShapeDtypeStruct((BATCH, N_EXPERTS, TOK), F32),
        ],
        compiler_params=pltpu.CompilerParams(vmem_limit_bytes=VMEM_LIMIT),
        name="mixer_out",
    )(*args, w_out, xs, mod, norm_g, rw_t)


def _select_slots(v, k):
    rows, n = v.shape
    u = pltpu.bitcast(v, I32)

    def body(i, thr):
        cand = thr | jnp.left_shift(jnp.int32(1), 30 - i)
        cnt = jnp.sum(jnp.where(u >= cand, 1.0, 0.0), axis=1, keepdims=True)
        return jnp.where(cnt >= k, cand, thr)

    thr = lax.fori_loop(0, 31, body, jnp.zeros((rows, 1), I32))
    gt = u > thr
    eq = u == thr
    need = k - jnp.sum(jnp.where(gt, 1.0, 0.0), axis=1, keepdims=True)
    r = lax.broadcasted_iota(I32, (PREFIX_CHUNK, PREFIX_CHUNK), 0)
    c = lax.broadcasted_iota(I32, (PREFIX_CHUNK, PREFIX_CHUNK), 1)
    before = jnp.where(r < c, 1.0, 0.0).astype(BF16)

    def prefix_count(flags):
        out, carry = [], jnp.zeros((rows, 1), F32)
        for j in range(n // PREFIX_CHUNK):
            chunk = flags[:, j * PREFIX_CHUNK:(j + 1) * PREFIX_CHUNK]
            out.append(_dot(chunk.astype(BF16), before) + carry)
            carry = carry + jnp.sum(chunk, axis=1, keepdims=True)
        return jnp.concatenate(out, axis=1)

    eq_rank = prefix_count(jnp.where(eq, 1.0, 0.0))
    sel = jnp.where(gt, 1.0, jnp.where(eq, jnp.where(eq_rank < need, 1.0, 0.0), 0.0))
    slot = prefix_count(sel)
    return jnp.where(sel > 0.5, slot, -1.0).astype(I32)


def _topk_kernel(a_ref, pos_ref):
    pos_ref[:, 0:SEQ] = _select_slots(a_ref[:, 0:SEQ], CAP_LAT)
    pos_ref[:, SEQ:TOK] = _select_slots(a_ref[:, SEQ:TOK], CAP_CTX)


def _topk(aff_t):
    rows = BATCH * N_EXPERTS
    return pl.pallas_call(
        _topk_kernel,
        out_shape=jax.ShapeDtypeStruct((rows, TOK), I32),
        compiler_params=pltpu.CompilerParams(vmem_limit_bytes=VMEM_LIMIT),
        name="topk",
    )(aff_t.reshape(rows, TOK))


def _moe_ffn_kernel(pos_ref, h_ref, wg_ref, wu_ref, wd_ref, yl_ref, yc_ref):
    pos = pos_ref[0, 0]
    slot_l = lax.broadcasted_iota(I32, (CAP_LAT, SEQ), 0)
    pick_l = jnp.where(slot_l == pos[:, 0:SEQ], 1.0, 0.0).astype(BF16)
    xe_l = _dot(pick_l, h_ref[0, 0:SEQ, :]).astype(BF16)
    slot_c = lax.broadcasted_iota(I32, (CAP_CTX, CTX_LEN), 0)
    pick_c = jnp.where(slot_c == pos[:, SEQ:TOK], 1.0, 0.0).astype(BF16)
    xe_c = _dot(pick_c, h_ref[0, SEQ:TOK, :]).astype(BF16)
    xe = jnp.concatenate([xe_l, xe_c], axis=0)
    hid = (_silu(_dot(xe, wg_ref[0])) * _dot(xe, wu_ref[0])).astype(BF16)
    ye = _dot(hid, wd_ref[0]).astype(BF16)
    yl_ref[0, 0] = ye[0:CAP_LAT]
    yc_ref[0, 0] = ye[CAP_LAT:CAP_LAT + CAP_CTX]


def _moe_ffn(pos, h2, wg, wu, wd):
    wspec = pl.BlockSpec((1, D_MODEL, D_MODEL), lambda b, e: (e, 0, 0))
    return pl.pallas_call(
        _moe_ffn_kernel,
        grid=(BATCH, N_EXPERTS),
        in_specs=[
            pl.BlockSpec((1, 1, 1, TOK), lambda b, e: (b, e, 0, 0)),
            pl.BlockSpec((1, TOK, D_MODEL), lambda b, e: (b, 0, 0)),
            wspec, wspec, wspec,
        ],
        out_specs=[
            pl.BlockSpec((1, 1, CAP_LAT, D_MODEL), lambda b, e: (b, e, 0, 0)),
            pl.BlockSpec((1, 1, CAP_CTX, D_MODEL), lambda b, e: (b, e, 0, 0)),
        ],
        out_shape=[
            jax.ShapeDtypeStruct((BATCH, N_EXPERTS, CAP_LAT, D_MODEL), BF16),
            jax.ShapeDtypeStruct((BATCH, N_EXPERTS, CAP_CTX, D_MODEL), BF16),
        ],
        compiler_params=pltpu.CompilerParams(vmem_limit_bytes=VMEM_LIMIT),
        name="moe_ffn",
    )(pos.reshape(BATCH, N_EXPERTS, 1, TOK), h2, wg, wu, wd)


def _moe_combine_kernel(pos_ref, aff_ref, yl_ref, yc_ref, x_ref, mod_ref, o_ref, gl_ref, gc_ref):
    t = pl.program_id(1)

    def gates(cap, g_ref):
        slot = lax.broadcasted_iota(I32, (cap, ROW_TILE), 0)
        for e in range(N_EXPERTS):
            g_ref[e * cap:(e + 1) * cap, :] = jnp.where(
                slot == pos_ref[0, e:e + 1, :], aff_ref[0, e:e + 1, :], 0.0).astype(BF16)

    @pl.when(t < N_LAT_ROW_TILES)
    def _latent():
        gates(CAP_LAT, gl_ref)
        o_ref[0] = x_ref[0] + mod_ref[5:6, :] * _dot_tn(gl_ref[...], yl_ref[0])

    @pl.when(t >= N_LAT_ROW_TILES)
    def _context():
        gates(CAP_CTX, gc_ref)
        o_ref[0] = x_ref[0] + mod_ref[5:6, :] * _dot_tn(gc_ref[...], yc_ref[0])


def _moe_combine(pos, aff_t, ye_l, ye_c, x1, mod, layer, n_row_tiles):
    tile = pl.BlockSpec((1, ROW_TILE, D_MODEL), lambda b, t: (b, t, 0))
    sel = pl.BlockSpec((1, N_EXPERTS, ROW_TILE), lambda b, t: (b, 0, t))
    return pl.pallas_call(
        _moe_combine_kernel,
        grid=(BATCH, n_row_tiles),
        in_specs=[
            sel, sel,
            pl.BlockSpec((1, N_EXPERTS * CAP_LAT, D_MODEL), lambda b, t: (b, 0, 0)),
            pl.BlockSpec((1, N_EXPERTS * CAP_CTX, D_MODEL), lambda b, t: (b, 0, 0)),
            tile,
            pl.BlockSpec((None, None, N_ADA, D_MODEL), lambda b, t: (layer, _mod_row(b, t), 0, 0)),
        ],
        out_specs=tile,
        out_shape=jax.ShapeDtypeStruct((BATCH, n_row_tiles * ROW_TILE, D_MODEL), F32),
        scratch_shapes=[
            pltpu.VMEM((N_EXPERTS * CAP_LAT, ROW_TILE), BF16),
            pltpu.VMEM((N_EXPERTS * CAP_CTX, ROW_TILE), BF16),
        ],
        compiler_params=pltpu.CompilerParams(vmem_limit_bytes=VMEM_LIMIT),
        name="moe_combine",
    )(pos.reshape(BATCH, N_EXPERTS, TOK), aff_t,
      ye_l.reshape(BATCH, N_EXPERTS * CAP_LAT, D_MODEL),
      ye_c.reshape(BATCH, N_EXPERTS * CAP_CTX, D_MODEL), x1, mod)


def _rope_tables():
    rows = SEQ // GRID_W
    row = jnp.repeat(jnp.arange(rows, dtype=F32), GRID_W)
    col = jnp.tile(jnp.arange(GRID_W, dtype=F32), rows)
    axis_dim = HEAD_DIM // 2
    inv_freq = ROPE_THETA ** (-jnp.arange(0, axis_dim, 2, dtype=F32) / axis_dim)
    ang_r = row[:, None] * inv_freq[None, :]
    ang_c = col[:, None] * inv_freq[None, :]
    cos_h = jnp.concatenate([jnp.cos(ang_r), jnp.cos(ang_r), jnp.cos(ang_c), jnp.cos(ang_c)], axis=-1)
    sin_h = jnp.concatenate([-jnp.sin(ang_r), jnp.sin(ang_r), -jnp.sin(ang_c), jnp.sin(ang_c)], axis=-1)
    cos_t = jnp.concatenate([jnp.tile(cos_h, (1, 2)), jnp.ones((CTX_LEN, LANES), F32)], axis=0)
    sin_t = jnp.concatenate([jnp.tile(sin_h, (1, 2)), jnp.zeros((CTX_LEN, LANES), F32)], axis=0)
    return cos_t, sin_t


def kernel(x, c, ctx, c_ctx, ada_w, ada_b, norm1_g, norm2_g, attn_w_in, attn_w_out, qnorm_a, knorm_a,
           qnorm_b, knorm_b, sink_b, conv_w_in, conv_k, conv_b, conv_w_out, router_w, moe_w_gate,
           moe_w_up, moe_w_down):
    assert x.shape == (BATCH, SEQ, D_MODEL) and ctx.shape == (BATCH, CTX_LEN, D_MODEL)
    cos_t, sin_t = _rope_tables()
    mod_rows = 24
    c_all = jnp.concatenate([c, c_ctx[None, :], jnp.zeros((mod_rows - BATCH - 1, D_MODEL), F32)], axis=0)
    mod = _adaln(c_all, ada_w, ada_b).reshape(DEPTH, mod_rows, N_ADA, D_MODEL)
    xs = jnp.concatenate([x, ctx], axis=1)

    for i in range(DEPTH):
        j = i // 2
        last = i == DEPTH - 1
        n_row_tiles = N_ROW_TILES
        g1 = norm1_g[i].reshape(1, D_MODEL)
        g2 = norm2_g[i].reshape(1, D_MODEL)
        rw_t = router_w[i].T
        if i % 2 == 0:
            tile2 = lambda g: jnp.tile(g.reshape(1, HEAD_DIM), (1, 2))
            qk_g = jnp.concatenate([tile2(qnorm_a[j]), tile2(knorm_a[j]), tile2(qnorm_b[j]),
                                    tile2(knorm_b[j])], axis=0)
            q, kv = _attn_in(xs, mod, i, g1, attn_w_in[j].astype(BF16), qk_g, cos_t, sin_t)
            o = _attention(q, kv, sink_b[j], N_Q_TILES)
            x1, h2, aff_t = _mixer_out(xs, mod, i, attn_w_out[j].astype(BF16), g2, rw_t, n_row_tiles,
                                       attn_o=o)
        else:
            u, bg = _conv_in(xs, mod, i, g1, conv_w_in[j].astype(BF16))
            x1, h2, aff_t = _mixer_out(xs, mod, i, conv_w_out[j].astype(BF16), g2, rw_t, n_row_tiles,
                                       conv=(u, bg, conv_k[j], conv_b[j].reshape(1, D_MODEL)))
        pos = _topk(aff_t)
        ye_l, ye_c = _moe_ffn(pos, h2, moe_w_gate[i].astype(BF16), moe_w_up[i].astype(BF16),
                              moe_w_down[i].astype(BF16))
        xs = _moe_combine(pos, aff_t, ye_l, ye_c, x1, mod, i, N_LAT_ROW_TILES if last else N_ROW_TILES)
    return xs
```

```python
import functools

import jax
import jax.numpy as jnp
import numpy as np
from jax import lax
from jax.experimental import pallas as pl
from jax.experimental.pallas import tpu as pltpu

F32 = jnp.float32
BF16 = jnp.bfloat16
I32 = jnp.int32

D_MODEL = 1024
BATCH = 16
SEQ = 2048
CTX_LEN = 256
TOK = SEQ + CTX_LEN
DEPTH = 4
GRID_W = 64
HEAD_DIM = 64
N_Q_HEADS = 8
N_KV_HEADS = 2
Q_WIDTH = N_Q_HEADS * HEAD_DIM
KV_WIDTH = N_KV_HEADS * HEAD_DIM
ATT_IN_WIDTH = 2 * (Q_WIDTH + 2 * KV_WIDTH)
WINDOW = 128
ROPE_THETA = 10000.0
N_EXPERTS = 16
CAP_LAT = 2 * SEQ // N_EXPERTS
CAP_CTX = 2 * CTX_LEN // N_EXPERTS
N_ADA = 6
EPS = 1e-6
NEG_INF = -1e30

LANES = 128
ROW_TILE = 256
N_ROW_TILES = TOK // ROW_TILE
N_LAT_ROW_TILES = SEQ // ROW_TILE
Q_TILE = 128
N_Q_TILES = TOK // Q_TILE
N_LAT_Q_TILES = SEQ // Q_TILE
WIN_SPAN = Q_TILE + 2 * WINDOW
HALO = 16
PREFIX_CHUNK = 256
SLOT_ALIGN = 16
WIN_SLOTS = 64
STARTS_STRIDE = 16
VMEM_LIMIT = 56 * 1024 * 1024

HIGHEST = lax.Precision.HIGHEST


def _dot(a, b, precision=None):
    return jnp.dot(a, b, preferred_element_type=F32, precision=precision)


def _dot_nt(a, b, precision=None):
    return lax.dot_general(a, b, (((1,), (1,)), ((), ())), preferred_element_type=F32,
                           precision=precision)


def _dot_tn(a, b):
    return lax.dot_general(a, b, (((0,), (0,)), ((), ())), preferred_element_type=F32)


def _rmsnorm_rows(x, g):
    return x * lax.rsqrt(jnp.mean(x * x, axis=-1, keepdims=True) + EPS) * g


def _silu(x):
    return x * (1.0 / (1.0 + jnp.exp(-x)))


def _adaln_kernel(c_ref, w_ref, b_ref, o_ref):
    o_ref[0] = _dot(_silu(c_ref[...]), w_ref[0], precision=HIGHEST) + b_ref[0]


def _adaln(c_all, ada_w, ada_b):
    rows = c_all.shape[0]
    return pl.pallas_call(
        _adaln_kernel,
        grid=(DEPTH, N_ADA),
        in_specs=[
            pl.BlockSpec((rows, D_MODEL), lambda i, j: (0, 0)),
            pl.BlockSpec((1, D_MODEL, D_MODEL), lambda i, j: (i, 0, j)),
            pl.BlockSpec((1, 1, D_MODEL), lambda i, j: (i, 0, j)),
        ],
        out_specs=pl.BlockSpec((1, rows, D_MODEL), lambda i, j: (i, 0, j)),
        out_shape=jax.ShapeDtypeStruct((DEPTH, rows, N_ADA * D_MODEL), F32),
        name="adaln",
    )(c_all, ada_w, ada_b.reshape(DEPTH, 1, N_ADA * D_MODEL))


def _mod_row(b, t):
    return jnp.where(t >= N_LAT_ROW_TILES, BATCH, b)


def _attn_in_kernel(x_ref, mod_ref, g_ref, w_ref, qkg_ref, cos_ref, sin_ref, q_ref, kv_ref):
    x = x_ref[0]
    h = _rmsnorm_rows(x, g_ref[...])
    h = h * (1.0 + mod_ref[1:2, :]) + mod_ref[0:1, :]
    p = _dot(h.astype(BF16), w_ref[...])

    lane = lax.broadcasted_iota(I32, (ROW_TILE, LANES), 1)
    r = lax.broadcasted_iota(I32, (LANES, LANES), 0)
    c = lax.broadcasted_iota(I32, (LANES, LANES), 1)
    seg_mean = jnp.where((r >> 6) == (c >> 6), 1.0 / HEAD_DIM, 0.0).astype(BF16)
    cos = cos_ref[...]
    sin = sin_ref[...]
    first_half = (lane & (HEAD_DIM // 2 - 1)) < (HEAD_DIM // 4)
    low = lane < HEAD_DIM

    def qk_norm_rope(xc, g):
        sq = xc * xc
        sq_hi = sq.astype(BF16)
        sq_lo = (sq - sq_hi.astype(F32)).astype(BF16)
        ms = _dot(sq_hi, seg_mean) + _dot(sq_lo, seg_mean)
        xn = xc * lax.rsqrt(ms + EPS) * g
        partner = jnp.where(first_half, pltpu.roll(xn, LANES - HEAD_DIM // 4, axis=1),
                            pltpu.roll(xn, HEAD_DIM // 4, axis=1))
        return xn * cos + partner * sin

    def put_kv(kind, base, vals):
        rolled = pltpu.roll(vals, HEAD_DIM, axis=1)
        kv_ref[0, kind, 0, base + 0] = jnp.where(low, vals, 0.0).astype(BF16)
        kv_ref[0, kind, 0, base + 1] = jnp.where(low, 0.0, rolled).astype(BF16)
        kv_ref[0, kind, 1, base + 0] = jnp.where(low, rolled, 0.0).astype(BF16)
        kv_ref[0, kind, 1, base + 1] = jnp.where(low, 0.0, vals).astype(BF16)

    for kind in range(2):
        col0 = kind * (Q_WIDTH + 2 * KV_WIDTH)
        gq = qkg_ref[2 * kind:2 * kind + 1, :]
        gk = qkg_ref[2 * kind + 1:2 * kind + 2, :]
        for ci in range(Q_WIDTH // LANES):
            qc = qk_norm_rope(p[:, col0 + ci * LANES:col0 + (ci + 1) * LANES], gq)
            q_ref[0, :, kind * Q_WIDTH + ci * LANES:kind * Q_WIDTH + (ci + 1) * LANES] = (
                qc * HEAD_DIM ** -0.5).astype(BF16)
        kc = qk_norm_rope(p[:, col0 + Q_WIDTH:col0 + Q_WIDTH + KV_WIDTH], gk)
        put_kv(kind, 0, kc)
        put_kv(kind, 2, p[:, col0 + Q_WIDTH + KV_WIDTH:col0 + Q_WIDTH + 2 * KV_WIDTH])


def _attn_in(xs, mod, layer, norm_g, w_in, qk_g, cos_t, sin_t):
    return pl.pallas_call(
        _attn_in_kernel,
        grid=(BATCH, N_ROW_TILES),
        in_specs=[
            pl.BlockSpec((1, ROW_TILE, D_MODEL), lambda b, t: (b, t, 0)),
            pl.BlockSpec((None, None, N_ADA, D_MODEL), lambda b, t: (layer, _mod_row(b, t), 0, 0)),
            pl.BlockSpec((1, D_MODEL), lambda b, t: (0, 0)),
            pl.BlockSpec((D_MODEL, ATT_IN_WIDTH), lambda b, t: (0, 0)),
            pl.BlockSpec((4, LANES), lambda b, t: (0, 0)),
            pl.BlockSpec((ROW_TILE, LANES), lambda b, t: (t, 0)),
            pl.BlockSpec((ROW_TILE, LANES), lambda b, t: (t, 0)),
        ],
        out_specs=[
            pl.BlockSpec((1, ROW_TILE, 2 * Q_WIDTH), lambda b, t: (b, t, 0)),
            pl.BlockSpec((1, 2, N_KV_HEADS, 4, ROW_TILE, LANES), lambda b, t: (b, 0, 0, 0, t, 0)),
        ],
        out_shape=[
            jax.ShapeDtypeStruct((BATCH, TOK, 2 * Q_WIDTH), BF16),
            jax.ShapeDtypeStruct((BATCH, 2, N_KV_HEADS, 4, TOK, LANES), BF16),
        ],
        compiler_params=pltpu.CompilerParams(vmem_limit_bytes=VMEM_LIMIT),
        name="attn_in",
    )(xs, mod, norm_g, w_in, qk_g, cos_t, sin_t)


def _softmax_pv(parts, sink=None):
    m = None
    for s, _ in parts:
        mx = jnp.max(s, axis=-1, keepdims=True)
        m = mx if m is None else jnp.maximum(m, mx)
    if sink is not None:
        m = jnp.maximum(m, sink)
    l = jnp.exp(sink - m) if sink is not None else None
    acc = None
    for s, v in parts:
        pr = jnp.exp(s - m)
        ls = jnp.sum(pr, axis=-1, keepdims=True)
        l = ls if l is None else l + ls
        pv = _dot(pr.astype(BF16), v)
        acc = pv if acc is None else acc + pv
    return acc * (1.0 / l)


def _attn_kernel(sink_ref, q_ref, kv_ref, o_ref):
    t = pl.program_id(1)
    half = lax.broadcasted_iota(I32, (2 * Q_TILE, 1), 0) < Q_TILE

    def q_rows(kind, g):
        c0 = kind * Q_WIDTH + g * 2 * LANES
        return jnp.concatenate([q_ref[0, :, c0:c0 + LANES], q_ref[0, :, c0 + LANES:c0 + 2 * LANES]],
                               axis=0)

    def put(kind, g, o):
        c0 = kind * Q_WIDTH + g * 2 * LANES
        o_ref[0, :, c0:c0 + LANES] = o[:Q_TILE].astype(BF16)
        o_ref[0, :, c0 + LANES:c0 + 2 * LANES] = o[Q_TILE:].astype(BF16)

    def sink_col(g, parity):
        return jnp.where(half, sink_ref[4 * g + parity], sink_ref[4 * g + 2 + parity])

    @pl.when(t < N_LAT_Q_TILES)
    def _latent():
        for g in range(N_KV_HEADS):
            lhs = q_rows(0, g)
            o = None
            for parity in range(2):
                s = _dot_nt(lhs, kv_ref[0, 0, g, parity])
                op = _softmax_pv([(s, kv_ref[0, 0, g, 2 + parity])])
                o = op if o is None else o + op
            put(0, g, o)

        ks = pl.multiple_of(jnp.clip(t * Q_TILE - WINDOW, 0, SEQ - WIN_SPAN), Q_TILE)
        qi = lax.broadcasted_iota(I32, (2 * Q_TILE, WIN_SPAN), 0) & (Q_TILE - 1)
        kj = lax.broadcasted_iota(I32, (2 * Q_TILE, WIN_SPAN), 1)
        band = jnp.abs(qi - kj + (t * Q_TILE - ks)) <= WINDOW
        for g in range(N_KV_HEADS):
            lhs = q_rows(1, g)
            o = None
            for parity in range(2):
                s_w = _dot_nt(lhs, kv_ref[0, 1, g, parity, pl.ds(ks, WIN_SPAN), :])
                s_w = jnp.where(band, s_w, NEG_INF)
                s_c = _dot_nt(lhs, kv_ref[0, 1, g, parity, SEQ:TOK, :])
                op = _softmax_pv(
                    [(s_w, kv_ref[0, 1, g, 2 + parity, pl.ds(ks, WIN_SPAN), :]),
                     (s_c, kv_ref[0, 1, g, 2 + parity, SEQ:TOK, :])],
                    sink=sink_col(g, parity))
                o = op if o is None else o + op
            put(1, g, o)

    @pl.when(t >= N_LAT_Q_TILES)
    def _context():
        for kind in range(2):
            for g in range(N_KV_HEADS):
                lhs = q_rows(kind, g)
                o = None
                for parity in range(2):
                    s_c = _dot_nt(lhs, kv_ref[0, kind, g, parity, SEQ:TOK, :])
                    op = _softmax_pv([(s_c, kv_ref[0, kind, g, 2 + parity, SEQ:TOK, :])],
                                     sink=sink_col(g, parity) if kind == 1 else None)
                    o = op if o is None else o + op
                put(kind, g, o)


def _attention(q, kv, sink, n_q_tiles):
    return pl.pallas_call(
        _attn_kernel,
        grid=(BATCH, n_q_tiles),
        in_specs=[
            pl.BlockSpec(memory_space=pltpu.SMEM),
            pl.BlockSpec((1, Q_TILE, 2 * Q_WIDTH), lambda b, t: (b, t, 0)),
            pl.BlockSpec((1, 2, N_KV_HEADS, 4, TOK, LANES), lambda b, t: (b, 0, 0, 0, 0, 0)),
        ],
        out_specs=pl.BlockSpec((1, Q_TILE, 2 * Q_WIDTH), lambda b, t: (b, t, 0)),
        out_shape=jax.ShapeDtypeStruct((BATCH, TOK, 2 * Q_WIDTH), BF16),
        compiler_params=pltpu.CompilerParams(vmem_limit_bytes=VMEM_LIMIT),
        name="attention",
    )(sink, q, kv)


def _conv_in_kernel(x_ref, mod_ref, g_ref, w_ref, u_ref, bg_ref):
    x = x_ref[0]
    h = _rmsnorm_rows(x, g_ref[...])
    h = (h * (1.0 + mod_ref[1:2, :]) + mod_ref[0:1, :]).astype(BF16)
    bg_ref[0] = _dot(h, w_ref[:, 0:D_MODEL]).astype(BF16)
    cg = _dot(h, w_ref[:, D_MODEL:2 * D_MODEL])
    v = _dot(h, w_ref[:, 2 * D_MODEL:3 * D_MODEL])
    u_ref[0] = (cg * v).astype(BF16)


def _conv_in(xs, mod, layer, norm_g, w_in):
    return pl.pallas_call(
        _conv_in_kernel,
        grid=(BATCH, N_ROW_TILES),
        in_specs=[
            pl.BlockSpec((1, ROW_TILE, D_MODEL), lambda b, t: (b, t, 0)),
            pl.BlockSpec((None, None, N_ADA, D_MODEL), lambda b, t: (layer, _mod_row(b, t), 0, 0)),
            pl.BlockSpec((1, D_MODEL), lambda b, t: (0, 0)),
            pl.BlockSpec((D_MODEL, 3 * D_MODEL), lambda b, t: (0, 0)),
        ],
        out_specs=[
            pl.BlockSpec((1, ROW_TILE, D_MODEL), lambda b, t: (b, t, 0)),
            pl.BlockSpec((1, ROW_TILE, D_MODEL), lambda b, t: (b, t, 0)),
        ],
        out_shape=[
            jax.ShapeDtypeStruct((BATCH, TOK, D_MODEL), BF16),
            jax.ShapeDtypeStruct((BATCH, TOK, D_MODEL), BF16),
        ],
        compiler_params=pltpu.CompilerParams(vmem_limit_bytes=VMEM_LIMIT),
        name="conv_in",
    )(xs, mod, norm_g, w_in)


def _tail(lhs, w_ref, x_ref, mod_ref, g_ref, rw_ref, x1_ref, h2_ref, aff_ref):
    x1 = x_ref[0] + mod_ref[2:3, :] * _dot(lhs, w_ref[...])
    x1_ref[0] = x1
    h2 = _rmsnorm_rows(x1, g_ref[...])
    h2 = h2 * (1.0 + mod_ref[4:5, :]) + mod_ref[3:4, :]
    h2_ref[0] = h2.astype(BF16)
    logits = _dot_nt(rw_ref[...], h2, precision=HIGHEST)
    e = jnp.exp(logits - jnp.max(logits, axis=0, keepdims=True))
    aff_ref[0] = e / jnp.sum(e, axis=0, keepdims=True)


def _attn_out_kernel(o_ref, w_ref, x_ref, mod_ref, g_ref, rw_ref, x1_ref, h2_ref, aff_ref):
    _tail(o_ref[0], w_ref, x_ref, mod_ref, g_ref, rw_ref, x1_ref, h2_ref, aff_ref)


def _conv_out_kernel(u_ref, up_ref, un_ref, bg_ref, ck_ref, cb_ref, w_ref, x_ref, mod_ref, g_ref,
                     rw_ref, x1_ref, h2_ref, aff_ref):
    t = pl.program_id(1)
    u = u_ref[0].astype(F32)
    has_prev = jnp.logical_and(t != 0, t != N_LAT_ROW_TILES)
    has_next = t < N_LAT_ROW_TILES - 1
    prev_row = jnp.where(has_prev, up_ref[0, HALO - 1:HALO, :].astype(F32), 0.0)
    next_row = jnp.where(has_next, un_ref[0, 0:1, :].astype(F32), 0.0)
    row = lax.broadcasted_iota(I32, (ROW_TILE, D_MODEL), 0)
    u_prev = jnp.where(row == 0, prev_row, pltpu.roll(u, 1, axis=0))
    u_next = jnp.where(row == ROW_TILE - 1, next_row, pltpu.roll(u, ROW_TILE - 1, axis=0))
    y = ck_ref[0:1, :] * u_prev + ck_ref[1:2, :] * u + ck_ref[2:3, :] * u_next + cb_ref[...]
    lhs = (bg_ref[0].astype(F32) * y).astype(BF16)
    _tail(lhs, w_ref, x_ref, mod_ref, g_ref, rw_ref, x1_ref, h2_ref, aff_ref)


def _mixer_out(xs, mod, layer, w_out, norm_g, rw_t, n_row_tiles, attn_o=None, conv=None):
    tile = pl.BlockSpec((1, ROW_TILE, D_MODEL), lambda b, t: (b, t, 0))
    common_specs = [
        pl.BlockSpec((D_MODEL, D_MODEL), lambda b, t: (0, 0)),
        tile,
        pl.BlockSpec((None, None, N_ADA, D_MODEL), lambda b, t: (layer, _mod_row(b, t), 0, 0)),
        pl.BlockSpec((1, D_MODEL), lambda b, t: (0, 0)),
        pl.BlockSpec((N_EXPERTS, D_MODEL), lambda b, t: (0, 0)),
    ]
    if conv is None:
        body, specs, args = _attn_out_kernel, [tile], [attn_o]
    else:
        u, bg, ck, cb = conv
        halo_per_tile = ROW_TILE // HALO
        last_halo = TOK // HALO - 1
        body = _conv_out_kernel
        specs = [
            tile,
            pl.BlockSpec((1, HALO, D_MODEL), lambda b, t: (b, jnp.maximum(t * halo_per_tile - 1, 0), 0)),
            pl.BlockSpec((1, HALO, D_MODEL),
                         lambda b, t: (b, jnp.minimum((t + 1) * halo_per_tile, last_halo), 0)),
            tile,
            pl.BlockSpec((3, D_MODEL), lambda b, t: (0, 0)),
            pl.BlockSpec((1, D_MODEL), lambda b, t: (0, 0)),
        ]
        args = [u, u, u, bg, ck, cb]
    return pl.pallas_call(
        body,
        grid=(BATCH, n_row_tiles),
        in_specs=specs + common_specs,
        out_specs=[
            tile,
            tile,
            pl.BlockSpec((1, N_EXPERTS, ROW_TILE), lambda b, t: (b, 0, t)),
        ],
        out_shape=[
            jax.ShapeDtypeStruct((BATCH, TOK, D_MODEL), F32),
            jax.ShapeDtypeStruct((BATCH, TOK, D_MODEL), BF16),
            jax.ShapeDtypeStruct((BATCH, N_EXPERTS, TOK), F32),
        ],
        compiler_params=pltpu.CompilerParams(vmem_limit_bytes=VMEM_LIMIT),
        name="mixer_out",
    )(*args, w_out, xs, mod, norm_g, rw_t)


def _select_slots(v, k):
    rows, n = v.shape
    u = pltpu.bitcast(v, I32)

    def body(i, thr):
        cand = thr | jnp.left_shift(jnp.int32(1), 30 - i)
        cnt = jnp.sum(jnp.where(u >= cand, 1.0, 0.0), axis=1, keepdims=True)
        return jnp.where(cnt >= k, cand, thr)

    thr = lax.fori_loop(0, 31, body, jnp.zeros((rows, 1), I32))
    gt = u > thr
    eq = u == thr
    need = k - jnp.sum(jnp.where(gt, 1.0, 0.0), axis=1, keepdims=True)
    r = lax.broadcasted_iota(I32, (PREFIX_CHUNK, PREFIX_CHUNK), 0)
    c = lax.broadcasted_iota(I32, (PREFIX_CHUNK, PREFIX_CHUNK), 1)
    before = jnp.where(r < c, 1.0, 0.0).astype(BF16)

    def prefix_count(flags):
        out, carry = [], jnp.zeros((rows, 1), F32)
        for j in range(n // PREFIX_CHUNK):
            chunk = flags[:, j * PREFIX_CHUNK:(j + 1) * PREFIX_CHUNK]
            out.append(_dot(chunk.astype(BF16), before) + carry)
            carry = carry + jnp.sum(chunk, axis=1, keepdims=True)
        return jnp.concatenate(out, axis=1)

    eq_rank = prefix_count(jnp.where(eq, 1.0, 0.0))
    sel = jnp.where(gt, 1.0, jnp.where(eq, jnp.where(eq_rank < need, 1.0, 0.0), 0.0))
    slot = prefix_count(sel)
    return jnp.where(sel > 0.5, slot, -1.0).astype(I32), sel


def _topk_kernel(a_ref, pos_ref, starts_ref):
    pos_l, sel_l = _select_slots(a_ref[:, 0:SEQ], CAP_LAT)
    pos_c, _ = _select_slots(a_ref[:, SEQ:TOK], CAP_CTX)
    pos_ref[:, 0:SEQ] = pos_l
    pos_ref[:, SEQ:TOK] = pos_c
    tok = lax.broadcasted_iota(I32, (SEQ, LANES), 0)
    tile = lax.broadcasted_iota(I32, (SEQ, LANES), 1)
    earlier = jnp.where(tok < tile * ROW_TILE, 1.0, 0.0).astype(BF16)
    starts_ref[...] = _dot(sel_l.astype(BF16), earlier).astype(I32)


def _topk(aff_t):
    rows = BATCH * N_EXPERTS
    return pl.pallas_call(
        _topk_kernel,
        out_shape=[jax.ShapeDtypeStruct((rows, TOK), I32), jax.ShapeDtypeStruct((rows, LANES), I32)],
        compiler_params=pltpu.CompilerParams(vmem_limit_bytes=VMEM_LIMIT),
        name="topk",
    )(aff_t.reshape(rows, TOK))


def _slot_window(starts_ref, b, e, t):
    base = (b * N_EXPERTS + e) * STARTS_STRIDE + t
    s = starts_ref[base]
    n = starts_ref[base + 1] - s
    a = jnp.minimum(s - (s & (SLOT_ALIGN - 1)), CAP_LAT - WIN_SLOTS)
    return pl.multiple_of(a, SLOT_ALIGN), (s - a + n) <= WIN_SLOTS


def _all_windows(starts_ref, b, t):
    wins = [_slot_window(starts_ref, b, e, t) for e in range(N_EXPERTS)]
    fits = functools.reduce(jnp.logical_and, [ok for _, ok in wins])
    return [a for a, _ in wins], fits


def _moe_gather_kernel(starts_ref, pos_ref, h_ref, xl_ref, xc_ref):
    b = pl.program_id(0)
    t = pl.program_id(1)
    h = h_ref[0]

    @pl.when(t == 0)
    def _zero():
        xl_ref[...] = jnp.zeros(xl_ref.shape, BF16)

    @pl.when(t < N_LAT_ROW_TILES)
    def _latent():
        starts, fits = _all_windows(starts_ref, b, t)

        @pl.when(fits)
        def _windowed():
            slot = lax.broadcasted_iota(I32, (WIN_SLOTS, ROW_TILE), 0)
            picks = [jnp.where(slot == pos_ref[0, e:e + 1, :] - starts[e], 1.0, 0.0).astype(BF16)
                     for e in range(N_EXPERTS)]
            y = _dot(jnp.concatenate(picks, axis=0), h).astype(BF16)
            for e in range(N_EXPERTS):
                xl_ref[0, e, pl.ds(starts[e], WIN_SLOTS), :] += y[e * WIN_SLOTS:(e + 1) * WIN_SLOTS]

        @pl.when(jnp.logical_not(fits))
        def _full():
            slot = lax.broadcasted_iota(I32, (CAP_LAT, ROW_TILE), 0)
            for e in range(N_EXPERTS):
                pick = jnp.where(slot == pos_ref[0, e:e + 1, :], 1.0, 0.0).astype(BF16)
                xl_ref[0, e] += _dot(pick, h).astype(BF16)

    @pl.when(t >= N_LAT_ROW_TILES)
    def _context():
        slot = lax.broadcasted_iota(I32, (CAP_CTX, ROW_TILE), 0)
        picks = [jnp.where(slot == pos_ref[0, e:e + 1, :], 1.0, 0.0).astype(BF16)
                 for e in range(N_EXPERTS)]
        y = _dot(jnp.concatenate(picks, axis=0), h).astype(BF16)
        for e in range(N_EXPERTS):
            xc_ref[0, e] = y[e * CAP_CTX:(e + 1) * CAP_CTX]


def _moe_gather(starts, pos, h2):
    return pl.pallas_call(
        _moe_gather_kernel,
        grid_spec=pltpu.PrefetchScalarGridSpec(
            num_scalar_prefetch=1,
            grid=(BATCH, N_ROW_TILES),
            in_specs=[
                pl.BlockSpec((1, N_EXPERTS, ROW_TILE), lambda b, t, s: (b, 0, t)),
                pl.BlockSpec((1, ROW_TILE, D_MODEL), lambda b, t, s: (b, t, 0)),
            ],
            out_specs=[
                pl.BlockSpec((1, N_EXPERTS, CAP_LAT, D_MODEL), lambda b, t, s: (b, 0, 0, 0)),
                pl.BlockSpec((1, N_EXPERTS, CAP_CTX, D_MODEL), lambda b, t, s: (b, 0, 0, 0)),
            ],
        ),
        out_shape=[
            jax.ShapeDtypeStruct((BATCH, N_EXPERTS, CAP_LAT, D_MODEL), BF16),
            jax.ShapeDtypeStruct((BATCH, N_EXPERTS, CAP_CTX, D_MODEL), BF16),
        ],
        compiler_params=pltpu.CompilerParams(vmem_limit_bytes=VMEM_LIMIT),
        name="moe_gather",
    )(starts, pos, h2)


def _moe_ffn_kernel(xl_ref, xc_ref, wg_ref, wu_ref, wd_ref, yl_ref, yc_ref, wg_bf, wu_bf, wd_bf):
    @pl.when(pl.program_id(1) == 0)
    def _cast_expert_weights():
        wg_bf[...] = wg_ref[0].astype(BF16)
        wu_bf[...] = wu_ref[0].astype(BF16)
        wd_bf[...] = wd_ref[0].astype(BF16)

    xe = jnp.concatenate([xl_ref[0, 0], xc_ref[0, 0]], axis=0)
    hid = (_silu(_dot(xe, wg_bf[...])) * _dot(xe, wu_bf[...])).astype(BF16)
    ye = _dot(hid, wd_bf[...]).astype(BF16)
    yl_ref[0, 0] = ye[0:CAP_LAT]
    yc_ref[0, 0] = ye[CAP_LAT:CAP_LAT + CAP_CTX]


def _moe_ffn(xe_l, xe_c, layer, wg, wu, wd):
    wspec = pl.BlockSpec((None, 1, D_MODEL, D_MODEL), lambda e, b: (layer, e, 0, 0))
    lat = pl.BlockSpec((1, 1, CAP_LAT, D_MODEL), lambda e, b: (b, e, 0, 0))
    ctx = pl.BlockSpec((1, 1, CAP_CTX, D_MODEL), lambda e, b: (b, e, 0, 0))
    return pl.pallas_call(
        _moe_ffn_kernel,
        grid=(N_EXPERTS, BATCH),
        in_specs=[lat, ctx, wspec, wspec, wspec],
        out_specs=[lat, ctx],
        out_shape=[
            jax.ShapeDtypeStruct((BATCH, N_EXPERTS, CAP_LAT, D_MODEL), BF16),
            jax.ShapeDtypeStruct((BATCH, N_EXPERTS, CAP_CTX, D_MODEL), BF16),
        ],
        scratch_shapes=[pltpu.VMEM((D_MODEL, D_MODEL), BF16)] * 3,
        compiler_params=pltpu.CompilerParams(vmem_limit_bytes=VMEM_LIMIT),
        name="moe_ffn",
    )(xe_l, xe_c, wg, wu, wd)


def _moe_combine_kernel(starts_ref, pos_ref, aff_ref, yl_ref, yc_ref, x_ref, mod_ref, o_ref,
                        gw_ref, yw_ref, gl_ref, gc_ref):
    b = pl.program_id(0)
    t = pl.program_id(1)

    def gates(g_ref, cap, e, first_slot):
        slot = lax.broadcasted_iota(I32, (cap, ROW_TILE), 0)
        g_ref[e * cap:(e + 1) * cap, :] = jnp.where(
            slot == pos_ref[0, e:e + 1, :] - first_slot, aff_ref[0, e:e + 1, :], 0.0).astype(BF16)

    def finish(moe):
        o_ref[0] = x_ref[0] + mod_ref[5:6, :] * moe

    @pl.when(t < N_LAT_ROW_TILES)
    def _latent():
        starts, fits = _all_windows(starts_ref, b, t)

        @pl.when(fits)
        def _windowed():
            for e in range(N_EXPERTS):
                gates(gw_ref, WIN_SLOTS, e, starts[e])
                row0 = pl.multiple_of(e * CAP_LAT + starts[e], SLOT_ALIGN)
                yw_ref[e * WIN_SLOTS:(e + 1) * WIN_SLOTS, :] = yl_ref[0, pl.ds(row0, WIN_SLOTS), :]
            finish(_dot_tn(gw_ref[...], yw_ref[...]))

        @pl.when(jnp.logical_not(fits))
        def _full():
            for e in range(N_EXPERTS):
                gates(gl_ref, CAP_LAT, e, 0)
            finish(_dot_tn(gl_ref[...], yl_ref[0]))

    @pl.when(t >= N_LAT_ROW_TILES)
    def _context():
        for e in range(N_EXPERTS):
            gates(gc_ref, CAP_CTX, e, 0)
        finish(_dot_tn(gc_ref[...], yc_ref[0]))


def _moe_combine(starts, pos, aff_t, ye_l, ye_c, x1, mod, layer, n_row_tiles):
    tile = pl.BlockSpec((1, ROW_TILE, D_MODEL), lambda b, t, s: (b, t, 0))
    sel = pl.BlockSpec((1, N_EXPERTS, ROW_TILE), lambda b, t, s: (b, 0, t))
    return pl.pallas_call(
        _moe_combine_kernel,
        grid_spec=pltpu.PrefetchScalarGridSpec(
            num_scalar_prefetch=1,
            grid=(BATCH, n_row_tiles),
            in_specs=[
                sel, sel,
                pl.BlockSpec((1, N_EXPERTS * CAP_LAT, D_MODEL), lambda b, t, s: (b, 0, 0)),
                pl.BlockSpec((1, N_EXPERTS * CAP_CTX, D_MODEL), lambda b, t, s: (b, 0, 0)),
                tile,
                pl.BlockSpec((None, None, N_ADA, D_MODEL),
                             lambda b, t, s: (layer, _mod_row(b, t), 0, 0)),
            ],
            out_specs=tile,
            scratch_shapes=[
                pltpu.VMEM((N_EXPERTS * WIN_SLOTS, ROW_TILE), BF16),
                pltpu.VMEM((N_EXPERTS * WIN_SLOTS, D_MODEL), BF16),
                pltpu.VMEM((N_EXPERTS * CAP_LAT, ROW_TILE), BF16),
                pltpu.VMEM((N_EXPERTS * CAP_CTX, ROW_TILE), BF16),
            ],
        ),
        out_shape=jax.ShapeDtypeStruct((BATCH, n_row_tiles * ROW_TILE, D_MODEL), F32),
        compiler_params=pltpu.CompilerParams(vmem_limit_bytes=VMEM_LIMIT),
        name="moe_combine",
    )(starts, pos, aff_t, ye_l.reshape(BATCH, N_EXPERTS * CAP_LAT, D_MODEL),
      ye_c.reshape(BATCH, N_EXPERTS * CAP_CTX, D_MODEL), x1, mod)


def _rope_tables():
    rows = SEQ // GRID_W
    row = jnp.repeat(jnp.arange(rows, dtype=F32), GRID_W)
    col = jnp.tile(jnp.arange(GRID_W, dtype=F32), rows)
    axis_dim = HEAD_DIM // 2
    inv_freq = ROPE_THETA ** (-jnp.arange(0, axis_dim, 2, dtype=F32) / axis_dim)
    ang_r = row[:, None] * inv_freq[None, :]
    ang_c = col[:, None] * inv_freq[None, :]
    cos_h = jnp.concatenate([jnp.cos(ang_r), jnp.cos(ang_r), jnp.cos(ang_c), jnp.cos(ang_c)], axis=-1)
    sin_h = jnp.concatenate([-jnp.sin(ang_r), jnp.sin(ang_r), -jnp.sin(ang_c), jnp.sin(ang_c)], axis=-1)
    cos_t = jnp.concatenate([jnp.tile(cos_h, (1, 2)), jnp.ones((CTX_LEN, LANES), F32)], axis=0)
    sin_t = jnp.concatenate([jnp.tile(sin_h, (1, 2)), jnp.zeros((CTX_LEN, LANES), F32)], axis=0)
    return cos_t, sin_t


def kernel(x, c, ctx, c_ctx, ada_w, ada_b, norm1_g, norm2_g, attn_w_in, attn_w_out, qnorm_a, knorm_a,
           qnorm_b, knorm_b, sink_b, conv_w_in, conv_k, conv_b, conv_w_out, router_w, moe_w_gate,
           moe_w_up, moe_w_down):
    assert x.shape == (BATCH, SEQ, D_MODEL) and ctx.shape == (BATCH, CTX_LEN, D_MODEL)
    cos_t, sin_t = _rope_tables()
    mod_rows = 24
    c_all = jnp.concatenate([c, c_ctx[None, :], jnp.zeros((mod_rows - BATCH - 1, D_MODEL), F32)], axis=0)
    mod = _adaln(c_all, ada_w, ada_b).reshape(DEPTH, mod_rows, N_ADA, D_MODEL)
    xs = jnp.concatenate([x, ctx], axis=1)

    for i in range(DEPTH):
        j = i // 2
        last = i == DEPTH - 1
        n_row_tiles = N_ROW_TILES
        g1 = norm1_g[i].reshape(1, D_MODEL)
        g2 = norm2_g[i].reshape(1, D_MODEL)
        rw_t = router_w[i].T
        if i % 2 == 0:
            tile2 = lambda g: jnp.tile(g.reshape(1, HEAD_DIM), (1, 2))
            qk_g = jnp.concatenate([tile2(qnorm_a[j]), tile2(knorm_a[j]), tile2(qnorm_b[j]),
                                    tile2(knorm_b[j])], axis=0)
            q, kv = _attn_in(xs, mod, i, g1, attn_w_in[j].astype(BF16), qk_g, cos_t, sin_t)
            o = _attention(q, kv, sink_b[j], N_Q_TILES)
            x1, h2, aff_t = _mixer_out(xs, mod, i, attn_w_out[j].astype(BF16), g2, rw_t, n_row_tiles,
                                       attn_o=o)
        else:
            u, bg = _conv_in(xs, mod, i, g1, conv_w_in[j].astype(BF16))
            x1, h2, aff_t = _mixer_out(xs, mod, i, conv_w_out[j].astype(BF16), g2, rw_t, n_row_tiles,
                                       conv=(u, bg, conv_k[j], conv_b[j].reshape(1, D_MODEL)))
        pos, starts = _topk(aff_t)
        pos = pos.reshape(BATCH, N_EXPERTS, TOK)
        starts = starts[:, :STARTS_STRIDE].reshape(-1)
        xe_l, xe_c = _moe_gather(starts, pos, h2)
        ye_l, ye_c = _moe_ffn(xe_l, xe_c, i, moe_w_gate, moe_w_up, moe_w_down)
        xs = _moe_combine(starts, pos, aff_t, ye_l, ye_c, x1, mod, i,
                          N_LAT_ROW_TILES if last else N_ROW_TILES)
    return xs
```

```python
import functools

import jax
import jax.numpy as jnp
import numpy as np
from jax import lax
from jax.experimental import pallas as pl
from jax.experimental.pallas import tpu as pltpu

F32 = jnp.float32
BF16 = jnp.bfloat16
I32 = jnp.int32

D_MODEL = 1024
BATCH = 16
SEQ = 2048
CTX_LEN = 256
TOK = SEQ + CTX_LEN
DEPTH = 4
GRID_W = 64
HEAD_DIM = 64
N_Q_HEADS = 8
N_KV_HEADS = 2
Q_WIDTH = N_Q_HEADS * HEAD_DIM
KV_WIDTH = N_KV_HEADS * HEAD_DIM
ATT_IN_WIDTH = 2 * (Q_WIDTH + 2 * KV_WIDTH)
WINDOW = 128
ROPE_THETA = 10000.0
N_EXPERTS = 16
CAP_LAT = 2 * SEQ // N_EXPERTS
CAP_CTX = 2 * CTX_LEN // N_EXPERTS
N_ADA = 6
EPS = 1e-6
NEG_INF = -1e30

LANES = 128
ROW_TILE = 256
TILES_PER_STEP = 3
STEP_ROWS = TILES_PER_STEP * ROW_TILE
STEPS_PER_SAMPLE = (SEQ + CTX_LEN) // STEP_ROWS
N_ROW_TILES = TOK // ROW_TILE
N_LAT_ROW_TILES = SEQ // ROW_TILE
Q_TILE = 128
N_Q_TILES = TOK // Q_TILE
N_LAT_Q_TILES = SEQ // Q_TILE
WIN_SPAN = Q_TILE + 2 * WINDOW
HALO = 16
PREFIX_CHUNK = 256
SLOT_ALIGN = 16
WIN_SLOTS = 64
STARTS_STRIDE = 16
VMEM_LIMIT = 56 * 1024 * 1024

LOG2E = 1.4426950408889634
PAR_SHIFT = 8
PAR_BOUNDED = 10
PAR_LEN = 16
MAX_SHIFT_SPAN = 100.0

HIGHEST = lax.Precision.HIGHEST


def _dot(a, b, precision=None):
    return jnp.dot(a, b, preferred_element_type=F32, precision=precision)


def _dot_nt(a, b, precision=None):
    return lax.dot_general(a, b, (((1,), (1,)), ((), ())), preferred_element_type=F32,
                           precision=precision)


def _dot_tn(a, b):
    return lax.dot_general(a, b, (((0,), (0,)), ((), ())), preferred_element_type=F32)


def _rmsnorm_rows(x, g):
    return x * lax.rsqrt(jnp.mean(x * x, axis=-1, keepdims=True) + EPS) * g


def _silu(x):
    return x * (1.0 / (1.0 + jnp.exp(-x)))


def _adaln_kernel(c_ref, w_ref, b_ref, o_ref):
    o_ref[0] = _dot(_silu(c_ref[...]), w_ref[0], precision=HIGHEST) + b_ref[0]


def _adaln(c_all, ada_w, ada_b):
    rows = c_all.shape[0]
    return pl.pallas_call(
        _adaln_kernel,
        grid=(DEPTH, N_ADA),
        in_specs=[
            pl.BlockSpec((rows, D_MODEL), lambda i, j: (0, 0)),
            pl.BlockSpec((1, D_MODEL, D_MODEL), lambda i, j: (i, 0, j)),
            pl.BlockSpec((1, 1, D_MODEL), lambda i, j: (i, 0, j)),
        ],
        out_specs=pl.BlockSpec((1, rows, D_MODEL), lambda i, j: (i, 0, j)),
        out_shape=jax.ShapeDtypeStruct((DEPTH, rows, N_ADA * D_MODEL), F32),
        name="adaln",
    )(c_all, ada_w, ada_b.reshape(DEPTH, 1, N_ADA * D_MODEL))


def _mod_row(b, t):
    return jnp.where(t >= N_LAT_ROW_TILES, BATCH, b)


def _tile_mods(mod_ref, modc_ref):
    last_step = pl.program_id(1) == STEPS_PER_SAMPLE - 1
    mods = [mod_ref[...]] * (TILES_PER_STEP - 1)
    return mods + [jnp.where(last_step, modc_ref[...], mod_ref[...])]


def _tile_rows(k):
    return slice(k * ROW_TILE, (k + 1) * ROW_TILE)


def _modulated_norm(x_ref, g_ref, mods, shift_row, scale_row):
    hs = []
    for k, m in enumerate(mods):
        h = _rmsnorm_rows(x_ref[0, _tile_rows(k), :], g_ref[...])
        hs.append((h * (1.0 + m[scale_row:scale_row + 1, :]) + m[shift_row:shift_row + 1, :]).astype(BF16))
    return jnp.concatenate(hs, axis=0)


def _attn_in_kernel(x_ref, mod_ref, modc_ref, g_ref, w_ref, qkg_ref, cos_ref, sin_ref, q_ref, kv_ref):
    h = _modulated_norm(x_ref, g_ref, _tile_mods(mod_ref, modc_ref), 0, 1)
    p = _dot(h, w_ref[...])

    lane = lax.broadcasted_iota(I32, (STEP_ROWS, LANES), 1)
    r = lax.broadcasted_iota(I32, (LANES, LANES), 0)
    c = lax.broadcasted_iota(I32, (LANES, LANES), 1)
    seg_mean = jnp.where((r >> 6) == (c >> 6), 1.0 / HEAD_DIM, 0.0).astype(BF16)
    cos = cos_ref[...]
    sin = sin_ref[...]
    first_half = (lane & (HEAD_DIM // 2 - 1)) < (HEAD_DIM // 4)
    low = lane < HEAD_DIM

    def qk_norm_rope(xc, g):
        sq = xc * xc
        sq_hi = sq.astype(BF16)
        sq_lo = (sq - sq_hi.astype(F32)).astype(BF16)
        ms = _dot(sq_hi, seg_mean) + _dot(sq_lo, seg_mean)
        xn = xc * lax.rsqrt(ms + EPS) * g
        partner = jnp.where(first_half, pltpu.roll(xn, LANES - HEAD_DIM // 4, axis=1),
                            pltpu.roll(xn, HEAD_DIM // 4, axis=1))
        return xn * cos + partner * sin

    def put_kv(kind, base, vals, pad):
        rolled = pltpu.roll(vals, HEAD_DIM, axis=1)
        kv_ref[0, kind, 0, base + 0] = jnp.where(low, vals, pad).astype(BF16)
        kv_ref[0, kind, 0, base + 1] = jnp.where(low, pad, rolled).astype(BF16)
        kv_ref[0, kind, 1, base + 0] = jnp.where(low, rolled, pad).astype(BF16)
        kv_ref[0, kind, 1, base + 1] = jnp.where(low, pad, vals).astype(BF16)

    for kind in range(2):
        col0 = kind * (Q_WIDTH + 2 * KV_WIDTH)
        gq = qkg_ref[2 * kind:2 * kind + 1, :]
        gk = qkg_ref[2 * kind + 1:2 * kind + 2, :]
        for ci in range(Q_WIDTH // LANES):
            qc = qk_norm_rope(p[:, col0 + ci * LANES:col0 + (ci + 1) * LANES], gq)
            q_ref[0, :, kind * Q_WIDTH + ci * LANES:kind * Q_WIDTH + (ci + 1) * LANES] = (
                qc * (HEAD_DIM ** -0.5 * LOG2E)).astype(BF16)
        kc = qk_norm_rope(p[:, col0 + Q_WIDTH:col0 + Q_WIDTH + KV_WIDTH], gk)
        put_kv(kind, 0, kc, 0.0)
        put_kv(kind, 2, p[:, col0 + Q_WIDTH + KV_WIDTH:col0 + Q_WIDTH + 2 * KV_WIDTH], 1.0)


def _mod_specs(layer):
    return [pl.BlockSpec((None, None, N_ADA, D_MODEL), lambda b, t: (layer, b, 0, 0)),
            pl.BlockSpec((None, None, N_ADA, D_MODEL), lambda b, t: (layer, BATCH, 0, 0))]


def _attn_in(xs, mod, layer, norm_g, w_in, qk_g, cos_t, sin_t):
    return pl.pallas_call(
        _attn_in_kernel,
        grid=(BATCH, STEPS_PER_SAMPLE),
        in_specs=[
            pl.BlockSpec((1, STEP_ROWS, D_MODEL), lambda b, t: (b, t, 0)),
            *_mod_specs(layer),
            pl.BlockSpec((1, D_MODEL), lambda b, t: (0, 0)),
            pl.BlockSpec((D_MODEL, ATT_IN_WIDTH), lambda b, t: (0, 0)),
            pl.BlockSpec((4, LANES), lambda b, t: (0, 0)),
            pl.BlockSpec((STEP_ROWS, LANES), lambda b, t: (t, 0)),
            pl.BlockSpec((STEP_ROWS, LANES), lambda b, t: (t, 0)),
        ],
        out_specs=[
            pl.BlockSpec((1, STEP_ROWS, 2 * Q_WIDTH), lambda b, t: (b, t, 0)),
            pl.BlockSpec((1, 2, N_KV_HEADS, 4, STEP_ROWS, LANES), lambda b, t: (b, 0, 0, 0, t, 0)),
        ],
        out_shape=[
            jax.ShapeDtypeStruct((BATCH, TOK, 2 * Q_WIDTH), BF16),
            jax.ShapeDtypeStruct((BATCH, 2, N_KV_HEADS, 4, TOK, LANES), BF16),
        ],
        compiler_params=pltpu.CompilerParams(vmem_limit_bytes=VMEM_LIMIT),
        name="attn_in",
    )(xs, mod, mod, norm_g, w_in, qk_g, cos_t, sin_t)


def _softmax_pv(parts, shift, sink=None):
    if shift is None:
        for s, _ in parts:
            mx = jnp.max(s, axis=-1, keepdims=True)
            shift = mx if shift is None else jnp.maximum(shift, mx)
        if sink is not None:
            shift = jnp.maximum(shift, sink)
    acc = None
    for s, v in parts:
        pv = _dot(jnp.exp2(s - shift).astype(BF16), v)
        acc = pv if acc is None else acc + pv
    return acc, (jnp.exp2(sink - shift) if sink is not None else None)


def _attn_body(par_ref, q_ref, kv_ref, o_ref, bounded):
    t = pl.program_id(1)
    half = lax.broadcasted_iota(I32, (2 * Q_TILE, 1), 0) < Q_TILE
    low = lax.broadcasted_iota(I32, (2 * Q_TILE, LANES), 1) < HEAD_DIM

    def q_rows(kind, g):
        c0 = kind * Q_WIDTH + g * 2 * LANES
        return jnp.concatenate([q_ref[0, :, c0:c0 + LANES], q_ref[0, :, c0 + LANES:c0 + 2 * LANES]],
                               axis=0)

    def heads(kind, g, parts_of, with_sink):
        outs = []
        lhs = q_rows(kind, g)
        for parity in range(2):
            sink = None
            if with_sink:
                sink = jnp.where(half, par_ref[4 * g + parity], par_ref[4 * g + 2 + parity])
            acc, sink_term = _softmax_pv(parts_of(lhs, parity), par_ref[PAR_SHIFT + kind] if bounded
                                         else None, sink)
            denom = acc[:, (1 - parity) * HEAD_DIM:(1 - parity) * HEAD_DIM + 1]
            if sink_term is not None:
                denom = denom + sink_term
            outs.append(acc * (1.0 / denom))
        o = jnp.where(low, outs[0], outs[1])
        c0 = kind * Q_WIDTH + g * 2 * LANES
        o_ref[0, :, c0:c0 + LANES] = o[:Q_TILE].astype(BF16)
        o_ref[0, :, c0 + LANES:c0 + 2 * LANES] = o[Q_TILE:].astype(BF16)

    @pl.when(t < N_LAT_Q_TILES)
    def _latent():
        for g in range(N_KV_HEADS):
            heads(0, g, lambda lhs, parity: [(_dot_nt(lhs, kv_ref[0, 0, g, parity]),
                                              kv_ref[0, 0, g, 2 + parity])], False)

        ks = pl.multiple_of(jnp.clip(t * Q_TILE - WINDOW, 0, SEQ - WIN_SPAN), Q_TILE)
        qi = lax.broadcasted_iota(I32, (2 * Q_TILE, WIN_SPAN), 0) & (Q_TILE - 1)
        kj = lax.broadcasted_iota(I32, (2 * Q_TILE, WIN_SPAN), 1)
        band = jnp.abs(qi - kj + (t * Q_TILE - ks)) <= WINDOW

        def window_parts(g):
            def parts_of(lhs, parity):
                s_w = _dot_nt(lhs, kv_ref[0, 1, g, parity, pl.ds(ks, WIN_SPAN), :])
                s_c = _dot_nt(lhs, kv_ref[0, 1, g, parity, SEQ:TOK, :])
                return [(jnp.where(band, s_w, NEG_INF), kv_ref[0, 1, g, 2 + parity, pl.ds(ks, WIN_SPAN), :]),
                        (s_c, kv_ref[0, 1, g, 2 + parity, SEQ:TOK, :])]
            return parts_of

        for g in range(N_KV_HEADS):
            heads(1, g, window_parts(g), True)

    @pl.when(t >= N_LAT_Q_TILES)
    def _context():
        for kind in range(2):
            for g in range(N_KV_HEADS):
                heads(kind, g,
                      lambda lhs, parity: [(_dot_nt(lhs, kv_ref[0, kind, g, parity, SEQ:TOK, :]),
                                            kv_ref[0, kind, g, 2 + parity, SEQ:TOK, :])],
                      kind == 1)


def _attn_kernel(par_ref, q_ref, kv_ref, o_ref):
    bounded = par_ref[PAR_BOUNDED] > 0.5

    @pl.when(bounded)
    def _bounded():
        _attn_body(par_ref, q_ref, kv_ref, o_ref, True)

    @pl.when(jnp.logical_not(bounded))
    def _exact_max():
        _attn_body(par_ref, q_ref, kv_ref, o_ref, False)


def _attention_params(qg_a, kg_a, qg_b, kg_b, sink):
    def bound(qg, kg):
        return 1.01 * LOG2E * HEAD_DIM ** 0.5 * jnp.max(jnp.abs(qg)) * jnp.max(jnp.abs(kg))
    sink2 = sink * LOG2E
    shift_a = bound(qg_a, kg_a)
    shift_b = jnp.maximum(bound(qg_b, kg_b), jnp.max(sink2))
    ok = jnp.logical_and(2.0 * shift_a < MAX_SHIFT_SPAN, shift_b + bound(qg_b, kg_b) < MAX_SHIFT_SPAN)
    return jnp.concatenate([sink2, jnp.stack([shift_a, shift_b, ok.astype(F32)]),
                            jnp.zeros((PAR_LEN - PAR_BOUNDED - 1,), F32)])


def _attention(q, kv, sink, n_q_tiles):
    return pl.pallas_call(
        _attn_kernel,
        grid=(BATCH, n_q_tiles),
        in_specs=[
            pl.BlockSpec(memory_space=pltpu.SMEM),
            pl.BlockSpec((1, Q_TILE, 2 * Q_WIDTH), lambda b, t: (b, t, 0)),
            pl.BlockSpec((1, 2, N_KV_HEADS, 4, TOK, LANES), lambda b, t: (b, 0, 0, 0, 0, 0)),
        ],
        out_specs=pl.BlockSpec((1, Q_TILE, 2 * Q_WIDTH), lambda b, t: (b, t, 0)),
        out_shape=jax.ShapeDtypeStruct((BATCH, TOK, 2 * Q_WIDTH), BF16),
        compiler_params=pltpu.CompilerParams(vmem_limit_bytes=VMEM_LIMIT),
        name="attention",
    )(sink, q, kv)


def _conv_in_kernel(x_ref, mod_ref, modc_ref, g_ref, w_ref, u_ref, bg_ref):
    h = _modulated_norm(x_ref, g_ref, _tile_mods(mod_ref, modc_ref), 0, 1)
    bg_ref[0] = _dot(h, w_ref[:, 0:D_MODEL]).astype(BF16)
    cg = _dot(h, w_ref[:, D_MODEL:2 * D_MODEL])
    v = _dot(h, w_ref[:, 2 * D_MODEL:3 * D_MODEL])
    u_ref[0] = (cg * v).astype(BF16)


def _conv_in(xs, mod, layer, norm_g, w_in):
    step = pl.BlockSpec((1, STEP_ROWS, D_MODEL), lambda b, t: (b, t, 0))
    return pl.pallas_call(
        _conv_in_kernel,
        grid=(BATCH, STEPS_PER_SAMPLE),
        in_specs=[
            step,
            *_mod_specs(layer),
            pl.BlockSpec((1, D_MODEL), lambda b, t: (0, 0)),
            pl.BlockSpec((D_MODEL, 3 * D_MODEL), lambda b, t: (0, 0)),
        ],
        out_specs=[step, step],
        out_shape=[
            jax.ShapeDtypeStruct((BATCH, TOK, D_MODEL), BF16),
            jax.ShapeDtypeStruct((BATCH, TOK, D_MODEL), BF16),
        ],
        compiler_params=pltpu.CompilerParams(vmem_limit_bytes=VMEM_LIMIT),
        name="conv_in",
    )(xs, mod, mod, norm_g, w_in)


def _tail(lhs, w_ref, x_ref, mod_ref, modc_ref, g_ref, rw_ref, x1_ref, h2_ref, aff_ref):
    y = _dot(lhs, w_ref[...])
    h2_hi, h2_lo = [], []
    for k, m in enumerate(_tile_mods(mod_ref, modc_ref)):
        x1 = x_ref[0, _tile_rows(k), :] + m[2:3, :] * y[_tile_rows(k)]
        x1_ref[0, _tile_rows(k), :] = x1
        h2 = _rmsnorm_rows(x1, g_ref[...])
        h2 = h2 * (1.0 + m[4:5, :]) + m[3:4, :]
        hi = h2.astype(BF16)
        h2_ref[0, _tile_rows(k), :] = hi
        h2_hi.append(hi)
        h2_lo.append((h2 - hi.astype(F32)).astype(BF16))
    rw = rw_ref[...]
    part = _dot_nt(rw, jnp.concatenate(h2_hi, axis=0))
    logits = (part[0:N_EXPERTS] + part[N_EXPERTS:2 * N_EXPERTS]
              + _dot_nt(rw[0:N_EXPERTS], jnp.concatenate(h2_lo, axis=0)))
    e = jnp.exp(logits - jnp.max(logits, axis=0, keepdims=True))
    aff_ref[0] = e / jnp.sum(e, axis=0, keepdims=True)


def _attn_out_kernel(o_ref, w_ref, x_ref, mod_ref, modc_ref, g_ref, rw_ref, x1_ref, h2_ref, aff_ref):
    _tail(o_ref[0], w_ref, x_ref, mod_ref, modc_ref, g_ref, rw_ref, x1_ref, h2_ref, aff_ref)


def _conv_out_kernel(u_ref, up_ref, un_ref, bg_ref, ck_ref, cb_ref, w_ref, x_ref, mod_ref, modc_ref,
                     g_ref, rw_ref, x1_ref, h2_ref, aff_ref):
    u = u_ref[0].astype(F32)
    row = lax.broadcasted_iota(I32, (STEP_ROWS, 1), 0)
    pos = row + pl.program_id(1) * STEP_ROWS
    seq_first = jnp.logical_or(pos == 0, pos == SEQ)
    seq_last = jnp.logical_or(pos == SEQ - 1, pos == TOK - 1)
    u_prev = jnp.where(row == 0, up_ref[0, HALO - 1:HALO, :].astype(F32), pltpu.roll(u, 1, axis=0))
    u_next = jnp.where(row == STEP_ROWS - 1, un_ref[0, 0:1, :].astype(F32),
                       pltpu.roll(u, STEP_ROWS - 1, axis=0))
    u_prev = jnp.where(seq_first, 0.0, u_prev)
    u_next = jnp.where(seq_last, 0.0, u_next)
    y = ck_ref[0:1, :] * u_prev + ck_ref[1:2, :] * u + ck_ref[2:3, :] * u_next + cb_ref[...]
    lhs = (bg_ref[0].astype(F32) * y).astype(BF16)
    _tail(lhs, w_ref, x_ref, mod_ref, modc_ref, g_ref, rw_ref, x1_ref, h2_ref, aff_ref)


def _mixer_out(xs, mod, layer, w_out, norm_g, rw_split, attn_o=None, conv=None):
    step = pl.BlockSpec((1, STEP_ROWS, D_MODEL), lambda b, t: (b, t, 0))
    common_specs = [
        pl.BlockSpec((D_MODEL, D_MODEL), lambda b, t: (0, 0)),
        step,
        *_mod_specs(layer),
        pl.BlockSpec((1, D_MODEL), lambda b, t: (0, 0)),
        pl.BlockSpec((2 * N_EXPERTS, D_MODEL), lambda b, t: (0, 0)),
    ]
    if conv is None:
        body, specs, args = _attn_out_kernel, [step], [attn_o]
    else:
        u, bg, ck, cb = conv
        halo_per_step = STEP_ROWS // HALO
        last_halo = TOK // HALO - 1
        body = _conv_out_kernel
        specs = [
            step,
            pl.BlockSpec((1, HALO, D_MODEL), lambda b, t: (b, jnp.maximum(t * halo_per_step - 1, 0), 0)),
            pl.BlockSpec((1, HALO, D_MODEL),
                         lambda b, t: (b, jnp.minimum((t + 1) * halo_per_step, last_halo), 0)),
            step,
            pl.BlockSpec((3, D_MODEL), lambda b, t: (0, 0)),
            pl.BlockSpec((1, D_MODEL), lambda b, t: (0, 0)),
        ]
        args = [u, u, u, bg, ck, cb]
    return pl.pallas_call(
        body,
        grid=(BATCH, STEPS_PER_SAMPLE),
        in_specs=specs + common_specs,
        out_specs=[
            step,
            step,
            pl.BlockSpec((1, N_EXPERTS, STEP_ROWS), lambda b, t: (b, 0, t)),
        ],
        out_shape=[
            jax.ShapeDtypeStruct((BATCH, TOK, D_MODEL), F32),
            jax.ShapeDtypeStruct((BATCH, TOK, D_MODEL), BF16),
            jax.ShapeDtypeStruct((BATCH, N_EXPERTS, TOK), F32),
        ],
        compiler_params=pltpu.CompilerParams(vmem_limit_bytes=VMEM_LIMIT),
        name="mixer_out",
    )(*args, w_out, xs, mod, mod, norm_g, rw_split)


def _select_slots(v, k):
    rows, n = v.shape
    u = pltpu.bitcast(v, I32)

    def body(i, thr):
        cand = thr | jnp.left_shift(jnp.int32(1), 30 - i)
        cnt = jnp.sum(jnp.where(u >= cand, 1.0, 0.0), axis=1, keepdims=True)
        return jnp.where(cnt >= k, cand, thr)

    thr = lax.fori_loop(0, 31, body, jnp.zeros((rows, 1), I32))
    gt = u > thr
    eq = u == thr
    need = k - jnp.sum(jnp.where(gt, 1.0, 0.0), axis=1, keepdims=True)
    r = lax.broadcasted_iota(I32, (PREFIX_CHUNK, PREFIX_CHUNK), 0)
    c = lax.broadcasted_iota(I32, (PREFIX_CHUNK, PREFIX_CHUNK), 1)
    before = jnp.where(r < c, 1.0, 0.0).astype(BF16)

    def prefix_count(flags):
        out, carry = [], jnp.zeros((rows, 1), F32)
        for j in range(n // PREFIX_CHUNK):
            chunk = flags[:, j * PREFIX_CHUNK:(j + 1) * PREFIX_CHUNK]
            out.append(_dot(chunk.astype(BF16), before) + carry)
            carry = carry + jnp.sum(chunk, axis=1, keepdims=True)
        return jnp.concatenate(out, axis=1)

    eq_rank = prefix_count(jnp.where(eq, 1.0, 0.0))
    sel = jnp.where(gt, 1.0, jnp.where(eq, jnp.where(eq_rank < need, 1.0, 0.0), 0.0))
    slot = prefix_count(sel)
    return jnp.where(sel > 0.5, slot, -1.0).astype(I32), sel


def _topk_kernel(a_ref, pos_ref, starts_ref):
    pos_l, sel_l = _select_slots(a_ref[:, 0:SEQ], CAP_LAT)
    pos_c, _ = _select_slots(a_ref[:, SEQ:TOK], CAP_CTX)
    pos_ref[:, 0:SEQ] = pos_l
    pos_ref[:, SEQ:TOK] = pos_c
    tok = lax.broadcasted_iota(I32, (SEQ, LANES), 0)
    tile = lax.broadcasted_iota(I32, (SEQ, LANES), 1)
    earlier = jnp.where(tok < tile * ROW_TILE, 1.0, 0.0).astype(BF16)
    starts_ref[...] = _dot(sel_l.astype(BF16), earlier).astype(I32)


def _topk(aff_t):
    rows = BATCH * N_EXPERTS
    return pl.pallas_call(
        _topk_kernel,
        out_shape=[jax.ShapeDtypeStruct((rows, TOK), I32), jax.ShapeDtypeStruct((rows, LANES), I32)],
        compiler_params=pltpu.CompilerParams(vmem_limit_bytes=VMEM_LIMIT),
        name="topk",
    )(aff_t.reshape(rows, TOK))


def _slot_window(starts_ref, b, e, t):
    base = (b * N_EXPERTS + e) * STARTS_STRIDE + t
    s = starts_ref[base]
    n = starts_ref[base + 1] - s
    a = jnp.minimum(s - (s & (SLOT_ALIGN - 1)), CAP_LAT - WIN_SLOTS)
    return pl.multiple_of(a, SLOT_ALIGN), (s - a + n) <= WIN_SLOTS


def _all_windows(starts_ref, b, t):
    wins = [_slot_window(starts_ref, b, e, t) for e in range(N_EXPERTS)]
    fits = functools.reduce(jnp.logical_and, [ok for _, ok in wins])
    return [a for a, _ in wins], fits


def _moe_gather_kernel(starts_ref, pos_ref, h_ref, xl_ref, xc_ref):
    b = pl.program_id(0)
    t = pl.program_id(1)
    h = h_ref[0]

    @pl.when(t == 0)
    def _zero():
        xl_ref[...] = jnp.zeros(xl_ref.shape, BF16)

    @pl.when(t < N_LAT_ROW_TILES)
    def _latent():
        starts, fits = _all_windows(starts_ref, b, t)

        @pl.when(fits)
        def _windowed():
            slot = lax.broadcasted_iota(I32, (WIN_SLOTS, ROW_TILE), 0)
            picks = [jnp.where(slot == pos_ref[0, e:e + 1, :] - starts[e], 1.0, 0.0).astype(BF16)
                     for e in range(N_EXPERTS)]
            y = _dot(jnp.concatenate(picks, axis=0), h).astype(BF16)
            for e in range(N_EXPERTS):
                xl_ref[0, e, pl.ds(starts[e], WIN_SLOTS), :] += y[e * WIN_SLOTS:(e + 1) * WIN_SLOTS]

        @pl.when(jnp.logical_not(fits))
        def _full():
            slot = lax.broadcasted_iota(I32, (CAP_LAT, ROW_TILE), 0)
            for e in range(N_EXPERTS):
                pick = jnp.where(slot == pos_ref[0, e:e + 1, :], 1.0, 0.0).astype(BF16)
                xl_ref[0, e] += _dot(pick, h).astype(BF16)

    @pl.when(t >= N_LAT_ROW_TILES)
    def _context():
        slot = lax.broadcasted_iota(I32, (CAP_CTX, ROW_TILE), 0)
        picks = [jnp.where(slot == pos_ref[0, e:e + 1, :], 1.0, 0.0).astype(BF16)
                 for e in range(N_EXPERTS)]
        y = _dot(jnp.concatenate(picks, axis=0), h).astype(BF16)
        for e in range(N_EXPERTS):
            xc_ref[0, e] = y[e * CAP_CTX:(e + 1) * CAP_CTX]


def _moe_gather(starts, pos, h2):
    return pl.pallas_call(
        _moe_gather_kernel,
        grid_spec=pltpu.PrefetchScalarGridSpec(
            num_scalar_prefetch=1,
            grid=(BATCH, N_ROW_TILES),
            in_specs=[
                pl.BlockSpec((1, N_EXPERTS, ROW_TILE), lambda b, t, s: (b, 0, t)),
                pl.BlockSpec((1, ROW_TILE, D_MODEL), lambda b, t, s: (b, t, 0)),
            ],
            out_specs=[
                pl.BlockSpec((1, N_EXPERTS, CAP_LAT, D_MODEL), lambda b, t, s: (b, 0, 0, 0)),
                pl.BlockSpec((1, N_EXPERTS, CAP_CTX, D_MODEL), lambda b, t, s: (b, 0, 0, 0)),
            ],
        ),
        out_shape=[
            jax.ShapeDtypeStruct((BATCH, N_EXPERTS, CAP_LAT, D_MODEL), BF16),
            jax.ShapeDtypeStruct((BATCH, N_EXPERTS, CAP_CTX, D_MODEL), BF16),
        ],
        compiler_params=pltpu.CompilerParams(vmem_limit_bytes=VMEM_LIMIT),
        name="moe_gather",
    )(starts, pos, h2)


def _moe_ffn_kernel(xl_ref, xc_ref, wg_ref, wu_ref, wd_ref, yl_ref, yc_ref, wg_bf, wu_bf, wd_bf):
    @pl.when(pl.program_id(1) == 0)
    def _cast_expert_weights():
        wg_bf[...] = wg_ref[0].astype(BF16)
        wu_bf[...] = wu_ref[0].astype(BF16)
        wd_bf[...] = wd_ref[0].astype(BF16)

    xe = jnp.concatenate([xl_ref[0, 0], xc_ref[0, 0]], axis=0)
    hid = (_silu(_dot(xe, wg_bf[...])) * _dot(xe, wu_bf[...])).astype(BF16)
    ye = _dot(hid, wd_bf[...]).astype(BF16)
    yl_ref[0, 0] = ye[0:CAP_LAT]
    yc_ref[0, 0] = ye[CAP_LAT:CAP_LAT + CAP_CTX]


def _moe_ffn(xe_l, xe_c, layer, wg, wu, wd):
    wspec = pl.BlockSpec((None, 1, D_MODEL, D_MODEL), lambda e, b: (layer, e, 0, 0))
    lat = pl.BlockSpec((1, 1, CAP_LAT, D_MODEL), lambda e, b: (b, e, 0, 0))
    ctx = pl.BlockSpec((1, 1, CAP_CTX, D_MODEL), lambda e, b: (b, e, 0, 0))
    return pl.pallas_call(
        _moe_ffn_kernel,
        grid=(N_EXPERTS, BATCH),
        in_specs=[lat, ctx, wspec, wspec, wspec],
        out_specs=[lat, ctx],
        out_shape=[
            jax.ShapeDtypeStruct((BATCH, N_EXPERTS, CAP_LAT, D_MODEL), BF16),
            jax.ShapeDtypeStruct((BATCH, N_EXPERTS, CAP_CTX, D_MODEL), BF16),
        ],
        scratch_shapes=[pltpu.VMEM((D_MODEL, D_MODEL), BF16)] * 3,
        compiler_params=pltpu.CompilerParams(vmem_limit_bytes=VMEM_LIMIT),
        name="moe_ffn",
    )(xe_l, xe_c, wg, wu, wd)


def _moe_combine_kernel(starts_ref, pos_ref, aff_ref, yl_ref, yc_ref, x_ref, mod_ref, o_ref,
                        gw_ref, yw_ref, gl_ref, gc_ref):
    b = pl.program_id(0)
    t = pl.program_id(1)

    def gates(g_ref, cap, e, first_slot):
        slot = lax.broadcasted_iota(I32, (cap, ROW_TILE), 0)
        g_ref[e * cap:(e + 1) * cap, :] = jnp.where(
            slot == pos_ref[0, e:e + 1, :] - first_slot, aff_ref[0, e:e + 1, :], 0.0).astype(BF16)

    def finish(moe):
        o_ref[0] = x_ref[0] + mod_ref[5:6, :] * moe

    @pl.when(t < N_LAT_ROW_TILES)
    def _latent():
        starts, fits = _all_windows(starts_ref, b, t)

        @pl.when(fits)
        def _windowed():
            for e in range(N_EXPERTS):
                gates(gw_ref, WIN_SLOTS, e, starts[e])
                row0 = pl.multiple_of(e * CAP_LAT + starts[e], SLOT_ALIGN)
                yw_ref[e * WIN_SLOTS:(e + 1) * WIN_SLOTS, :] = yl_ref[0, pl.ds(row0, WIN_SLOTS), :]
            finish(_dot_tn(gw_ref[...], yw_ref[...]))

        @pl.when(jnp.logical_not(fits))
        def _full():
            for e in range(N_EXPERTS):
                gates(gl_ref, CAP_LAT, e, 0)
            finish(_dot_tn(gl_ref[...], yl_ref[0]))

    @pl.when(t >= N_LAT_ROW_TILES)
    def _context():
        for e in range(N_EXPERTS):
            gates(gc_ref, CAP_CTX, e, 0)
        finish(_dot_tn(gc_ref[...], yc_ref[0]))


def _moe_combine(starts, pos, aff_t, ye_l, ye_c, x1, mod, layer, n_row_tiles):
    tile = pl.BlockSpec((1, ROW_TILE, D_MODEL), lambda b, t, s: (b, t, 0))
    sel = pl.BlockSpec((1, N_EXPERTS, ROW_TILE), lambda b, t, s: (b, 0, t))
    return pl.pallas_call(
        _moe_combine_kernel,
        grid_spec=pltpu.PrefetchScalarGridSpec(
            num_scalar_prefetch=1,
            grid=(BATCH, n_row_tiles),
            in_specs=[
                sel, sel,
                pl.BlockSpec((1, N_EXPERTS * CAP_LAT, D_MODEL), lambda b, t, s: (b, 0, 0)),
                pl.BlockSpec((1, N_EXPERTS * CAP_CTX, D_MODEL), lambda b, t, s: (b, 0, 0)),
                tile,
                pl.BlockSpec((None, None, N_ADA, D_MODEL),
                             lambda b, t, s: (layer, _mod_row(b, t), 0, 0)),
            ],
            out_specs=tile,
            scratch_shapes=[
                pltpu.VMEM((N_EXPERTS * WIN_SLOTS, ROW_TILE), BF16),
                pltpu.VMEM((N_EXPERTS * WIN_SLOTS, D_MODEL), BF16),
                pltpu.VMEM((N_EXPERTS * CAP_LAT, ROW_TILE), BF16),
                pltpu.VMEM((N_EXPERTS * CAP_CTX, ROW_TILE), BF16),
            ],
        ),
        out_shape=jax.ShapeDtypeStruct((BATCH, n_row_tiles * ROW_TILE, D_MODEL), F32),
        compiler_params=pltpu.CompilerParams(vmem_limit_bytes=VMEM_LIMIT),
        name="moe_combine",
    )(starts, pos, aff_t, ye_l.reshape(BATCH, N_EXPERTS * CAP_LAT, D_MODEL),
      ye_c.reshape(BATCH, N_EXPERTS * CAP_CTX, D_MODEL), x1, mod)


def _rope_tables():
    rows = SEQ // GRID_W
    row = jnp.repeat(jnp.arange(rows, dtype=F32), GRID_W)
    col = jnp.tile(jnp.arange(GRID_W, dtype=F32), rows)
    axis_dim = HEAD_DIM // 2
    inv_freq = ROPE_THETA ** (-jnp.arange(0, axis_dim, 2, dtype=F32) / axis_dim)
    ang_r = row[:, None] * inv_freq[None, :]
    ang_c = col[:, None] * inv_freq[None, :]
    cos_h = jnp.concatenate([jnp.cos(ang_r), jnp.cos(ang_r), jnp.cos(ang_c), jnp.cos(ang_c)], axis=-1)
    sin_h = jnp.concatenate([-jnp.sin(ang_r), jnp.sin(ang_r), -jnp.sin(ang_c), jnp.sin(ang_c)], axis=-1)
    cos_t = jnp.concatenate([jnp.tile(cos_h, (1, 2)), jnp.ones((CTX_LEN, LANES), F32)], axis=0)
    sin_t = jnp.concatenate([jnp.tile(sin_h, (1, 2)), jnp.zeros((CTX_LEN, LANES), F32)], axis=0)
    return cos_t, sin_t


def kernel(x, c, ctx, c_ctx, ada_w, ada_b, norm1_g, norm2_g, attn_w_in, attn_w_out, qnorm_a, knorm_a,
           qnorm_b, knorm_b, sink_b, conv_w_in, conv_k, conv_b, conv_w_out, router_w, moe_w_gate,
           moe_w_up, moe_w_down):
    assert x.shape == (BATCH, SEQ, D_MODEL) and ctx.shape == (BATCH, CTX_LEN, D_MODEL)
    cos_t, sin_t = _rope_tables()
    mod_rows = 24
    c_all = jnp.concatenate([c, c_ctx[None, :], jnp.zeros((mod_rows - BATCH - 1, D_MODEL), F32)], axis=0)
    mod = _adaln(c_all, ada_w, ada_b).reshape(DEPTH, mod_rows, N_ADA, D_MODEL)
    xs = jnp.concatenate([x, ctx], axis=1)

    for i in range(DEPTH):
        j = i // 2
        last = i == DEPTH - 1
        g1 = norm1_g[i].reshape(1, D_MODEL)
        g2 = norm2_g[i].reshape(1, D_MODEL)
        rw_t = router_w[i].T
        rw_hi = rw_t.astype(BF16)
        rw_split = jnp.concatenate([rw_hi, (rw_t - rw_hi.astype(F32)).astype(BF16)], axis=0)
        if i % 2 == 0:
            tile2 = lambda g: jnp.tile(g.reshape(1, HEAD_DIM), (1, 2))
            qk_g = jnp.concatenate([tile2(qnorm_a[j]), tile2(knorm_a[j]), tile2(qnorm_b[j]),
                                    tile2(knorm_b[j])], axis=0)
            q, kv = _attn_in(xs, mod, i, g1, attn_w_in[j].astype(BF16), qk_g, cos_t, sin_t)
            o = _attention(q, kv, _attention_params(qnorm_a[j], knorm_a[j], qnorm_b[j], knorm_b[j],
                                                    sink_b[j]), N_Q_TILES)
            x1, h2, aff_t = _mixer_out(xs, mod, i, attn_w_out[j].astype(BF16), g2, rw_split, attn_o=o)
        else:
            u, bg = _conv_in(xs, mod, i, g1, conv_w_in[j].astype(BF16))
            x1, h2, aff_t = _mixer_out(xs, mod, i, conv_w_out[j].astype(BF16), g2, rw_split,
                                       conv=(u, bg, conv_k[j], conv_b[j].reshape(1, D_MODEL)))
        pos, starts = _topk(aff_t)
        pos = pos.reshape(BATCH, N_EXPERTS, TOK)
        starts = starts[:, :STARTS_STRIDE].reshape(-1)
        xe_l, xe_c = _moe_gather(starts, pos, h2)
        ye_l, ye_c = _moe_ffn(xe_l, xe_c, i, moe_w_gate, moe_w_up, moe_w_down)
        xs = _moe_combine(starts, pos, aff_t, ye_l, ye_c, x1, mod, i,
                          N_LAT_ROW_TILES if last else N_ROW_TILES)
    return xs
```

```python
import functools

import jax
import jax.numpy as jnp
import numpy as np
from jax import lax
from jax.experimental import pallas as pl
from jax.experimental.pallas import tpu as pltpu

F32 = jnp.float32
BF16 = jnp.bfloat16
I32 = jnp.int32

D_MODEL = 1024
BATCH = 16
SEQ = 2048
CTX_LEN = 256
TOK = SEQ + CTX_LEN
DEPTH = 4
GRID_W = 64
HEAD_DIM = 64
N_Q_HEADS = 8
N_KV_HEADS = 2
Q_WIDTH = N_Q_HEADS * HEAD_DIM
KV_WIDTH = N_KV_HEADS * HEAD_DIM
ATT_IN_WIDTH = 2 * (Q_WIDTH + 2 * KV_WIDTH)
WINDOW = 128
ROPE_THETA = 10000.0
N_EXPERTS = 16
CAP_LAT = 2 * SEQ // N_EXPERTS
CAP_CTX = 2 * CTX_LEN // N_EXPERTS
N_ADA = 6
EPS = 1e-6
NEG_INF = -1e30

LANES = 128
ROW_TILE = 256
TILES_PER_STEP = 3
STEP_ROWS = TILES_PER_STEP * ROW_TILE
STEPS_PER_SAMPLE = (SEQ + CTX_LEN) // STEP_ROWS
N_ROW_TILES = TOK // ROW_TILE
N_LAT_ROW_TILES = SEQ // ROW_TILE
Q_TILE = 128
N_Q_TILES = TOK // Q_TILE
N_LAT_Q_TILES = SEQ // Q_TILE
WIN_SPAN = Q_TILE + 2 * WINDOW
HALO = 16
PREFIX_CHUNK = 256
SLOT_ALIGN = 16
WIN_SLOTS = 64
STARTS_STRIDE = 16
FFN_SAMPLES = 4
VMEM_LIMIT = 56 * 1024 * 1024

LOG2E = 1.4426950408889634
PAR_SHIFT = 8
PAR_BOUNDED = 10
PAR_LEN = 16
MAX_SHIFT_SPAN = 100.0

HIGHEST = lax.Precision.HIGHEST


def _dot(a, b, precision=None):
    return jnp.dot(a, b, preferred_element_type=F32, precision=precision)


def _dot_nt(a, b, precision=None):
    return lax.dot_general(a, b, (((1,), (1,)), ((), ())), preferred_element_type=F32,
                           precision=precision)


def _dot_tn(a, b):
    return lax.dot_general(a, b, (((0,), (0,)), ((), ())), preferred_element_type=F32)


def _rmsnorm_rows(x, g):
    return x * lax.rsqrt(jnp.mean(x * x, axis=-1, keepdims=True) + EPS) * g


def _silu(x):
    return x * (1.0 / (1.0 + jnp.exp(-x)))


def _adaln_kernel(c_ref, w_ref, b_ref, o_ref):
    o_ref[0] = _dot(_silu(c_ref[...]), w_ref[0], precision=HIGHEST) + b_ref[0]


def _adaln(c_all, ada_w, ada_b):
    rows = c_all.shape[0]
    return pl.pallas_call(
        _adaln_kernel,
        grid=(DEPTH, N_ADA),
        in_specs=[
            pl.BlockSpec((rows, D_MODEL), lambda i, j: (0, 0)),
            pl.BlockSpec((1, D_MODEL, D_MODEL), lambda i, j: (i, 0, j)),
            pl.BlockSpec((1, 1, D_MODEL), lambda i, j: (i, 0, j)),
        ],
        out_specs=pl.BlockSpec((1, rows, D_MODEL), lambda i, j: (i, 0, j)),
        out_shape=jax.ShapeDtypeStruct((DEPTH, rows, N_ADA * D_MODEL), F32),
        name="adaln",
    )(c_all, ada_w, ada_b.reshape(DEPTH, 1, N_ADA * D_MODEL))


def _tile_mods(mod_ref, modc_ref):
    last_step = pl.program_id(1) == STEPS_PER_SAMPLE - 1
    mods = [mod_ref[...]] * (TILES_PER_STEP - 1)
    return mods + [jnp.where(last_step, modc_ref[...], mod_ref[...])]


def _tile_rows(k):
    return slice(k * ROW_TILE, (k + 1) * ROW_TILE)


def _modulated_norm(x_ref, g_ref, mods, shift_row, scale_row):
    hs = []
    for k, m in enumerate(mods):
        h = _rmsnorm_rows(x_ref[0, _tile_rows(k), :], g_ref[...])
        hs.append((h * (1.0 + m[scale_row:scale_row + 1, :]) + m[shift_row:shift_row + 1, :]).astype(BF16))
    return jnp.concatenate(hs, axis=0)


def _attn_in_kernel(x_ref, mod_ref, modc_ref, g_ref, w_ref, qkg_ref, cos_ref, sin_ref, q_ref, kv_ref):
    h = _modulated_norm(x_ref, g_ref, _tile_mods(mod_ref, modc_ref), 0, 1)
    p = _dot(h, w_ref[...])

    lane = lax.broadcasted_iota(I32, (STEP_ROWS, LANES), 1)
    r = lax.broadcasted_iota(I32, (LANES, LANES), 0)
    c = lax.broadcasted_iota(I32, (LANES, LANES), 1)
    seg_mean = jnp.where((r >> 6) == (c >> 6), 1.0 / HEAD_DIM, 0.0).astype(BF16)
    cos = cos_ref[...]
    sin = sin_ref[...]
    first_half = (lane & (HEAD_DIM // 2 - 1)) < (HEAD_DIM // 4)
    low = lane < HEAD_DIM

    def qk_norm_rope(xc, g):
        sq = xc * xc
        sq_hi = sq.astype(BF16)
        sq_lo = (sq - sq_hi.astype(F32)).astype(BF16)
        ms = _dot(sq_hi, seg_mean) + _dot(sq_lo, seg_mean)
        xn = xc * lax.rsqrt(ms + EPS) * g
        partner = jnp.where(first_half, pltpu.roll(xn, LANES - HEAD_DIM // 4, axis=1),
                            pltpu.roll(xn, HEAD_DIM // 4, axis=1))
        return xn * cos + partner * sin

    def put_kv(kind, base, vals, pad):
        rolled = pltpu.roll(vals, HEAD_DIM, axis=1)
        kv_ref[0, kind, 0, base + 0] = jnp.where(low, vals, pad).astype(BF16)
        kv_ref[0, kind, 0, base + 1] = jnp.where(low, pad, rolled).astype(BF16)
        kv_ref[0, kind, 1, base + 0] = jnp.where(low, rolled, pad).astype(BF16)
        kv_ref[0, kind, 1, base + 1] = jnp.where(low, pad, vals).astype(BF16)

    for kind in range(2):
        col0 = kind * (Q_WIDTH + 2 * KV_WIDTH)
        gq = qkg_ref[2 * kind:2 * kind + 1, :]
        gk = qkg_ref[2 * kind + 1:2 * kind + 2, :]
        for ci in range(Q_WIDTH // LANES):
            qc = qk_norm_rope(p[:, col0 + ci * LANES:col0 + (ci + 1) * LANES], gq)
            q_ref[0, :, kind * Q_WIDTH + ci * LANES:kind * Q_WIDTH + (ci + 1) * LANES] = (
                qc * (HEAD_DIM ** -0.5 * LOG2E)).astype(BF16)
        kc = qk_norm_rope(p[:, col0 + Q_WIDTH:col0 + Q_WIDTH + KV_WIDTH], gk)
        put_kv(kind, 0, kc, 0.0)
        put_kv(kind, 2, p[:, col0 + Q_WIDTH + KV_WIDTH:col0 + Q_WIDTH + 2 * KV_WIDTH], 1.0)


def _mod_specs(layer):
    return [pl.BlockSpec((None, None, N_ADA, D_MODEL), lambda b, t: (layer, b, 0, 0)),
            pl.BlockSpec((None, None, N_ADA, D_MODEL), lambda b, t: (layer, BATCH, 0, 0))]


def _attn_in(xs, mod, layer, norm_g, w_in, qk_g, cos_t, sin_t):
    return pl.pallas_call(
        _attn_in_kernel,
        grid=(BATCH, STEPS_PER_SAMPLE),
        in_specs=[
            pl.BlockSpec((1, STEP_ROWS, D_MODEL), lambda b, t: (b, t, 0)),
            *_mod_specs(layer),
            pl.BlockSpec((1, D_MODEL), lambda b, t: (0, 0)),
            pl.BlockSpec((D_MODEL, ATT_IN_WIDTH), lambda b, t: (0, 0)),
            pl.BlockSpec((4, LANES), lambda b, t: (0, 0)),
            pl.BlockSpec((STEP_ROWS, LANES), lambda b, t: (t, 0)),
            pl.BlockSpec((STEP_ROWS, LANES), lambda b, t: (t, 0)),
        ],
        out_specs=[
            pl.BlockSpec((1, STEP_ROWS, 2 * Q_WIDTH), lambda b, t: (b, t, 0)),
            pl.BlockSpec((1, 2, N_KV_HEADS, 4, STEP_ROWS, LANES), lambda b, t: (b, 0, 0, 0, t, 0)),
        ],
        out_shape=[
            jax.ShapeDtypeStruct((BATCH, TOK, 2 * Q_WIDTH), BF16),
            jax.ShapeDtypeStruct((BATCH, 2, N_KV_HEADS, 4, TOK, LANES), BF16),
        ],
        compiler_params=pltpu.CompilerParams(vmem_limit_bytes=VMEM_LIMIT),
        name="attn_in",
    )(xs, mod, mod, norm_g, w_in, qk_g, cos_t, sin_t)


def _softmax_pv(parts, shift, sink=None):
    if shift is None:
        for s, _ in parts:
            mx = jnp.max(s, axis=-1, keepdims=True)
            shift = mx if shift is None else jnp.maximum(shift, mx)
        if sink is not None:
            shift = jnp.maximum(shift, sink)
    acc = None
    for s, v in parts:
        pv = _dot(jnp.exp2(s - shift).astype(BF16), v)
        acc = pv if acc is None else acc + pv
    return acc, (jnp.exp2(sink - shift) if sink is not None else None)


def _attn_body(par_ref, q_ref, kv_ref, o_ref, bounded):
    t = pl.program_id(1)
    half = lax.broadcasted_iota(I32, (2 * Q_TILE, 1), 0) < Q_TILE
    low = lax.broadcasted_iota(I32, (2 * Q_TILE, LANES), 1) < HEAD_DIM

    def q_rows(kind, g):
        c0 = kind * Q_WIDTH + g * 2 * LANES
        return jnp.concatenate([q_ref[0, :, c0:c0 + LANES], q_ref[0, :, c0 + LANES:c0 + 2 * LANES]],
                               axis=0)

    def heads(kind, g, parts_of, with_sink):
        outs = []
        lhs = q_rows(kind, g)
        for parity in range(2):
            sink = None
            if with_sink:
                sink = jnp.where(half, par_ref[4 * g + parity], par_ref[4 * g + 2 + parity])
            acc, sink_term = _softmax_pv(parts_of(lhs, parity), par_ref[PAR_SHIFT + kind] if bounded
                                         else None, sink)
            denom = acc[:, (1 - parity) * HEAD_DIM:(1 - parity) * HEAD_DIM + 1]
            if sink_term is not None:
                denom = denom + sink_term
            outs.append(acc * (1.0 / denom))
        o = jnp.where(low, outs[0], outs[1])
        c0 = kind * Q_WIDTH + g * 2 * LANES
        o_ref[0, :, c0:c0 + LANES] = o[:Q_TILE].astype(BF16)
        o_ref[0, :, c0 + LANES:c0 + 2 * LANES] = o[Q_TILE:].astype(BF16)

    @pl.when(t < N_LAT_Q_TILES)
    def _latent():
        for g in range(N_KV_HEADS):
            heads(0, g, lambda lhs, parity: [(_dot_nt(lhs, kv_ref[0, 0, g, parity]),
                                              kv_ref[0, 0, g, 2 + parity])], False)

        ks = pl.multiple_of(jnp.clip(t * Q_TILE - WINDOW, 0, SEQ - WIN_SPAN), Q_TILE)
        qi = lax.broadcasted_iota(I32, (2 * Q_TILE, WIN_SPAN), 0) & (Q_TILE - 1)
        kj = lax.broadcasted_iota(I32, (2 * Q_TILE, WIN_SPAN), 1)
        band = jnp.abs(qi - kj + (t * Q_TILE - ks)) <= WINDOW

        def window_parts(g):
            def parts_of(lhs, parity):
                s_w = _dot_nt(lhs, kv_ref[0, 1, g, parity, pl.ds(ks, WIN_SPAN), :])
                s_c = _dot_nt(lhs, kv_ref[0, 1, g, parity, SEQ:TOK, :])
                return [(jnp.where(band, s_w, NEG_INF), kv_ref[0, 1, g, 2 + parity, pl.ds(ks, WIN_SPAN), :]),
                        (s_c, kv_ref[0, 1, g, 2 + parity, SEQ:TOK, :])]
            return parts_of

        for g in range(N_KV_HEADS):
            heads(1, g, window_parts(g), True)

    @pl.when(t >= N_LAT_Q_TILES)
    def _context():
        for kind in range(2):
            for g in range(N_KV_HEADS):
                heads(kind, g,
                      lambda lhs, parity: [(_dot_nt(lhs, kv_ref[0, kind, g, parity, SEQ:TOK, :]),
                                            kv_ref[0, kind, g, 2 + parity, SEQ:TOK, :])],
                      kind == 1)


def _attn_kernel(par_ref, q_ref, kv_ref, o_ref):
    bounded = par_ref[PAR_BOUNDED] > 0.5

    @pl.when(bounded)
    def _bounded():
        _attn_body(par_ref, q_ref, kv_ref, o_ref, True)

    @pl.when(jnp.logical_not(bounded))
    def _exact_max():
        _attn_body(par_ref, q_ref, kv_ref, o_ref, False)


def _attention_params(qg_a, kg_a, qg_b, kg_b, sink):
    def bound(qg, kg):
        return 1.01 * LOG2E * HEAD_DIM ** 0.5 * jnp.max(jnp.abs(qg)) * jnp.max(jnp.abs(kg))
    sink2 = sink * LOG2E
    shift_a = bound(qg_a, kg_a)
    shift_b = jnp.maximum(bound(qg_b, kg_b), jnp.max(sink2))
    ok = jnp.logical_and(2.0 * shift_a < MAX_SHIFT_SPAN, shift_b + bound(qg_b, kg_b) < MAX_SHIFT_SPAN)
    return jnp.concatenate([sink2, jnp.stack([shift_a, shift_b, ok.astype(F32)]),
                            jnp.zeros((PAR_LEN - PAR_BOUNDED - 1,), F32)])


def _attention(q, kv, sink, n_q_tiles):
    return pl.pallas_call(
        _attn_kernel,
        grid=(BATCH, n_q_tiles),
        in_specs=[
            pl.BlockSpec(memory_space=pltpu.SMEM),
            pl.BlockSpec((1, Q_TILE, 2 * Q_WIDTH), lambda b, t: (b, t, 0)),
            pl.BlockSpec((1, 2, N_KV_HEADS, 4, TOK, LANES), lambda b, t: (b, 0, 0, 0, 0, 0)),
        ],
        out_specs=pl.BlockSpec((1, Q_TILE, 2 * Q_WIDTH), lambda b, t: (b, t, 0)),
        out_shape=jax.ShapeDtypeStruct((BATCH, TOK, 2 * Q_WIDTH), BF16),
        compiler_params=pltpu.CompilerParams(vmem_limit_bytes=VMEM_LIMIT),
        name="attention",
    )(sink, q, kv)


def _conv_in_kernel(x_ref, mod_ref, modc_ref, g_ref, w_ref, u_ref, bg_ref):
    h = _modulated_norm(x_ref, g_ref, _tile_mods(mod_ref, modc_ref), 0, 1)
    bg_ref[0] = _dot(h, w_ref[:, 0:D_MODEL]).astype(BF16)
    cg = _dot(h, w_ref[:, D_MODEL:2 * D_MODEL])
    v = _dot(h, w_ref[:, 2 * D_MODEL:3 * D_MODEL])
    u_ref[0] = (cg * v).astype(BF16)


def _conv_in(xs, mod, layer, norm_g, w_in):
    step = pl.BlockSpec((1, STEP_ROWS, D_MODEL), lambda b, t: (b, t, 0))
    return pl.pallas_call(
        _conv_in_kernel,
        grid=(BATCH, STEPS_PER_SAMPLE),
        in_specs=[
            step,
            *_mod_specs(layer),
            pl.BlockSpec((1, D_MODEL), lambda b, t: (0, 0)),
            pl.BlockSpec((D_MODEL, 3 * D_MODEL), lambda b, t: (0, 0)),
        ],
        out_specs=[step, step],
        out_shape=[
            jax.ShapeDtypeStruct((BATCH, TOK, D_MODEL), BF16),
            jax.ShapeDtypeStruct((BATCH, TOK, D_MODEL), BF16),
        ],
        compiler_params=pltpu.CompilerParams(vmem_limit_bytes=VMEM_LIMIT),
        name="conv_in",
    )(xs, mod, mod, norm_g, w_in)


def _tail(lhs, w_ref, x_ref, mod_ref, modc_ref, g_ref, rw_ref, x1_ref, h2_ref, aff_ref):
    y = _dot(lhs, w_ref[...])
    h2_hi, h2_lo = [], []
    for k, m in enumerate(_tile_mods(mod_ref, modc_ref)):
        x1 = x_ref[0, _tile_rows(k), :] + m[2:3, :] * y[_tile_rows(k)]
        x1_ref[0, _tile_rows(k), :] = x1
        h2 = _rmsnorm_rows(x1, g_ref[...])
        h2 = h2 * (1.0 + m[4:5, :]) + m[3:4, :]
        hi = h2.astype(BF16)
        h2_ref[0, _tile_rows(k), :] = hi
        h2_hi.append(hi)
        h2_lo.append((h2 - hi.astype(F32)).astype(BF16))
    rw = rw_ref[...]
    part = _dot_nt(rw, jnp.concatenate(h2_hi, axis=0))
    logits = (part[0:N_EXPERTS] + part[N_EXPERTS:2 * N_EXPERTS]
              + _dot_nt(rw[0:N_EXPERTS], jnp.concatenate(h2_lo, axis=0)))
    e = jnp.exp(logits - jnp.max(logits, axis=0, keepdims=True))
    aff_ref[0] = e / jnp.sum(e, axis=0, keepdims=True)


def _attn_out_kernel(o_ref, w_ref, x_ref, mod_ref, modc_ref, g_ref, rw_ref, x1_ref, h2_ref, aff_ref):
    _tail(o_ref[0], w_ref, x_ref, mod_ref, modc_ref, g_ref, rw_ref, x1_ref, h2_ref, aff_ref)


def _conv_out_kernel(u_ref, up_ref, un_ref, bg_ref, ck_ref, cb_ref, w_ref, x_ref, mod_ref, modc_ref,
                     g_ref, rw_ref, x1_ref, h2_ref, aff_ref):
    u = u_ref[0].astype(F32)
    row = lax.broadcasted_iota(I32, (STEP_ROWS, 1), 0)
    pos = row + pl.program_id(1) * STEP_ROWS
    seq_first = jnp.logical_or(pos == 0, pos == SEQ)
    seq_last = jnp.logical_or(pos == SEQ - 1, pos == TOK - 1)
    u_prev = jnp.where(row == 0, up_ref[0, HALO - 1:HALO, :].astype(F32), pltpu.roll(u, 1, axis=0))
    u_next = jnp.where(row == STEP_ROWS - 1, un_ref[0, 0:1, :].astype(F32),
                       pltpu.roll(u, STEP_ROWS - 1, axis=0))
    u_prev = jnp.where(seq_first, 0.0, u_prev)
    u_next = jnp.where(seq_last, 0.0, u_next)
    y = ck_ref[0:1, :] * u_prev + ck_ref[1:2, :] * u + ck_ref[2:3, :] * u_next + cb_ref[...]
    lhs = (bg_ref[0].astype(F32) * y).astype(BF16)
    _tail(lhs, w_ref, x_ref, mod_ref, modc_ref, g_ref, rw_ref, x1_ref, h2_ref, aff_ref)


def _mixer_out(xs, mod, layer, w_out, norm_g, rw_split, attn_o=None, conv=None):
    step = pl.BlockSpec((1, STEP_ROWS, D_MODEL), lambda b, t: (b, t, 0))
    common_specs = [
        pl.BlockSpec((D_MODEL, D_MODEL), lambda b, t: (0, 0)),
        step,
        *_mod_specs(layer),
        pl.BlockSpec((1, D_MODEL), lambda b, t: (0, 0)),
        pl.BlockSpec((2 * N_EXPERTS, D_MODEL), lambda b, t: (0, 0)),
    ]
    if conv is None:
        body, specs, args = _attn_out_kernel, [step], [attn_o]
    else:
        u, bg, ck, cb = conv
        halo_per_step = STEP_ROWS // HALO
        last_halo = TOK // HALO - 1
        body = _conv_out_kernel
        specs = [
            step,
            pl.BlockSpec((1, HALO, D_MODEL), lambda b, t: (b, jnp.maximum(t * halo_per_step - 1, 0), 0)),
            pl.BlockSpec((1, HALO, D_MODEL),
                         lambda b, t: (b, jnp.minimum((t + 1) * halo_per_step, last_halo), 0)),
            step,
            pl.BlockSpec((3, D_MODEL), lambda b, t: (0, 0)),
            pl.BlockSpec((1, D_MODEL), lambda b, t: (0, 0)),
        ]
        args = [u, u, u, bg, ck, cb]
    return pl.pallas_call(
        body,
        grid=(BATCH, STEPS_PER_SAMPLE),
        in_specs=specs + common_specs,
        out_specs=[
            step,
            step,
            pl.BlockSpec((1, N_EXPERTS, STEP_ROWS), lambda b, t: (b, 0, t)),
        ],
        out_shape=[
            jax.ShapeDtypeStruct((BATCH, TOK, D_MODEL), F32),
            jax.ShapeDtypeStruct((BATCH, TOK, D_MODEL), BF16),
            jax.ShapeDtypeStruct((BATCH, N_EXPERTS, TOK), F32),
        ],
        compiler_params=pltpu.CompilerParams(vmem_limit_bytes=VMEM_LIMIT),
        name="mixer_out",
    )(*args, w_out, xs, mod, mod, norm_g, rw_split)


def _select_slots(v, k):
    rows, n = v.shape
    u = pltpu.bitcast(v, I32)

    def body(i, thr):
        cand = thr | jnp.left_shift(jnp.int32(1), 30 - i)
        cnt = jnp.sum(jnp.where(u >= cand, 1.0, 0.0), axis=1, keepdims=True)
        return jnp.where(cnt >= k, cand, thr)

    thr = lax.fori_loop(0, 31, body, jnp.zeros((rows, 1), I32))
    gt = u > thr
    eq = u == thr
    need = k - jnp.sum(jnp.where(gt, 1.0, 0.0), axis=1, keepdims=True)
    r = lax.broadcasted_iota(I32, (PREFIX_CHUNK, PREFIX_CHUNK), 0)
    c = lax.broadcasted_iota(I32, (PREFIX_CHUNK, PREFIX_CHUNK), 1)
    before = jnp.where(r < c, 1.0, 0.0).astype(BF16)

    def prefix_count(flags):
        out, carry = [], jnp.zeros((rows, 1), F32)
        for j in range(n // PREFIX_CHUNK):
            chunk = flags[:, j * PREFIX_CHUNK:(j + 1) * PREFIX_CHUNK]
            out.append(_dot(chunk.astype(BF16), before) + carry)
            carry = carry + jnp.sum(chunk, axis=1, keepdims=True)
        return jnp.concatenate(out, axis=1)

    eq_rank = prefix_count(jnp.where(eq, 1.0, 0.0))
    sel = jnp.where(gt, 1.0, jnp.where(eq, jnp.where(eq_rank < need, 1.0, 0.0), 0.0))
    slot = prefix_count(sel)
    return jnp.where(sel > 0.5, slot, -1.0).astype(I32), sel


def _topk_kernel(a_ref, pos_ref, starts_ref):
    pos_l, sel_l = _select_slots(a_ref[:, 0:SEQ], CAP_LAT)
    pos_c, _ = _select_slots(a_ref[:, SEQ:TOK], CAP_CTX)
    pos_ref[:, 0:SEQ] = pos_l
    pos_ref[:, SEQ:TOK] = pos_c
    tok = lax.broadcasted_iota(I32, (SEQ, LANES), 0)
    tile = lax.broadcasted_iota(I32, (SEQ, LANES), 1)
    earlier = jnp.where(tok < tile * ROW_TILE, 1.0, 0.0).astype(BF16)
    starts_ref[...] = _dot(sel_l.astype(BF16), earlier).astype(I32)


def _topk(aff_t):
    rows = BATCH * N_EXPERTS
    return pl.pallas_call(
        _topk_kernel,
        out_shape=[jax.ShapeDtypeStruct((rows, TOK), I32), jax.ShapeDtypeStruct((rows, LANES), I32)],
        compiler_params=pltpu.CompilerParams(vmem_limit_bytes=VMEM_LIMIT),
        name="topk",
    )(aff_t.reshape(rows, TOK))


def _slot_window(starts_ref, b, e, t):
    base = (b * N_EXPERTS + e) * STARTS_STRIDE + t
    s = starts_ref[base]
    n = starts_ref[base + 1] - s
    a = jnp.minimum(s - (s & (SLOT_ALIGN - 1)), CAP_LAT - WIN_SLOTS)
    return pl.multiple_of(a, SLOT_ALIGN), (s - a + n) <= WIN_SLOTS


def _all_windows(starts_ref, b, t):
    wins = [_slot_window(starts_ref, b, e, t) for e in range(N_EXPERTS)]
    fits = functools.reduce(jnp.logical_and, [ok for _, ok in wins])
    return [a for a, _ in wins], fits


def _moe_gather_kernel(starts_ref, pos_ref, h_ref, xl_ref, xc_ref):
    b = pl.program_id(0)
    step = pl.program_id(1)
    last = step == STEPS_PER_SAMPLE - 1
    wins = [_all_windows(starts_ref, b, step * TILES_PER_STEP + k) for k in range(TILES_PER_STEP)]
    lat_fit = functools.reduce(jnp.logical_and, [fits for _, fits in wins[:-1]])
    all_fit = jnp.logical_and(lat_fit, wins[-1][1])

    def tile_pos(k, e):
        return pos_ref[0, e:e + 1, _tile_rows(k)]

    def windowed(k):
        starts = wins[k][0]
        slot = lax.broadcasted_iota(I32, (WIN_SLOTS, ROW_TILE), 0)
        picks = [jnp.where(slot == tile_pos(k, e) - starts[e], 1.0, 0.0).astype(BF16)
                 for e in range(N_EXPERTS)]
        y = _dot(jnp.concatenate(picks, axis=0), h_ref[0, _tile_rows(k), :]).astype(BF16)
        for e in range(N_EXPERTS):
            xl_ref[0, e, pl.ds(starts[e], WIN_SLOTS), :] += y[e * WIN_SLOTS:(e + 1) * WIN_SLOTS]

    def full(k):
        slot = lax.broadcasted_iota(I32, (CAP_LAT, ROW_TILE), 0)
        for e in range(N_EXPERTS):
            pick = jnp.where(slot == tile_pos(k, e), 1.0, 0.0).astype(BF16)
            xl_ref[0, e] += _dot(pick, h_ref[0, _tile_rows(k), :]).astype(BF16)

    def context(k):
        slot = lax.broadcasted_iota(I32, (CAP_CTX, ROW_TILE), 0)
        picks = [jnp.where(slot == tile_pos(k, e), 1.0, 0.0).astype(BF16) for e in range(N_EXPERTS)]
        y = _dot(jnp.concatenate(picks, axis=0), h_ref[0, _tile_rows(k), :]).astype(BF16)
        for e in range(N_EXPERTS):
            xc_ref[0, e] = y[e * CAP_CTX:(e + 1) * CAP_CTX]

    @pl.when(step == 0)
    def _zero():
        xl_ref[...] = jnp.zeros(xl_ref.shape, BF16)

    @pl.when(jnp.logical_and(jnp.logical_not(last), all_fit))
    def _windowed_step():
        for k in range(TILES_PER_STEP):
            windowed(k)

    @pl.when(jnp.logical_and(jnp.logical_not(last), jnp.logical_not(all_fit)))
    def _full_step():
        for k in range(TILES_PER_STEP):
            full(k)

    @pl.when(jnp.logical_and(last, lat_fit))
    def _windowed_last_step():
        for k in range(TILES_PER_STEP - 1):
            windowed(k)
        context(TILES_PER_STEP - 1)

    @pl.when(jnp.logical_and(last, jnp.logical_not(lat_fit)))
    def _full_last_step():
        for k in range(TILES_PER_STEP - 1):
            full(k)
        context(TILES_PER_STEP - 1)


def _moe_gather(starts, pos, h2):
    return pl.pallas_call(
        _moe_gather_kernel,
        grid_spec=pltpu.PrefetchScalarGridSpec(
            num_scalar_prefetch=1,
            grid=(BATCH, STEPS_PER_SAMPLE),
            in_specs=[
                pl.BlockSpec((1, N_EXPERTS, STEP_ROWS), lambda b, t, s: (b, 0, t)),
                pl.BlockSpec((1, STEP_ROWS, D_MODEL), lambda b, t, s: (b, t, 0)),
            ],
            out_specs=[
                pl.BlockSpec((1, N_EXPERTS, CAP_LAT, D_MODEL), lambda b, t, s: (b, 0, 0, 0)),
                pl.BlockSpec((1, N_EXPERTS, CAP_CTX, D_MODEL), lambda b, t, s: (b, 0, 0, 0)),
            ],
        ),
        out_shape=[
            jax.ShapeDtypeStruct((BATCH, N_EXPERTS, CAP_LAT, D_MODEL), BF16),
            jax.ShapeDtypeStruct((BATCH, N_EXPERTS, CAP_CTX, D_MODEL), BF16),
        ],
        compiler_params=pltpu.CompilerParams(vmem_limit_bytes=VMEM_LIMIT),
        name="moe_gather",
    )(starts, pos, h2)


def _moe_ffn_kernel(xl_ref, xc_ref, wg_ref, wu_ref, wd_ref, yl_ref, yc_ref, wg_bf, wu_bf, wd_bf):
    @pl.when(pl.program_id(1) == 0)
    def _cast_expert_weights():
        wg_bf[...] = wg_ref[0].astype(BF16)
        wu_bf[...] = wu_ref[0].astype(BF16)
        wd_bf[...] = wd_ref[0].astype(BF16)

    xe = jnp.concatenate([r[i, 0] for i in range(FFN_SAMPLES) for r in (xl_ref, xc_ref)], axis=0)
    hid = (_silu(_dot(xe, wg_bf[...])) * _dot(xe, wu_bf[...])).astype(BF16)
    ye = _dot(hid, wd_bf[...]).astype(BF16)
    cap = CAP_LAT + CAP_CTX
    for i in range(FFN_SAMPLES):
        yl_ref[i, 0] = ye[i * cap:i * cap + CAP_LAT]
        yc_ref[i, 0] = ye[i * cap + CAP_LAT:(i + 1) * cap]


def _moe_ffn(xe_l, xe_c, layer, wg, wu, wd):
    wspec = pl.BlockSpec((None, 1, D_MODEL, D_MODEL), lambda e, b: (layer, e, 0, 0))
    lat = pl.BlockSpec((FFN_SAMPLES, 1, CAP_LAT, D_MODEL), lambda e, b: (b, e, 0, 0))
    ctx = pl.BlockSpec((FFN_SAMPLES, 1, CAP_CTX, D_MODEL), lambda e, b: (b, e, 0, 0))
    return pl.pallas_call(
        _moe_ffn_kernel,
        grid=(N_EXPERTS, BATCH // FFN_SAMPLES),
        in_specs=[lat, ctx, wspec, wspec, wspec],
        out_specs=[lat, ctx],
        out_shape=[
            jax.ShapeDtypeStruct((BATCH, N_EXPERTS, CAP_LAT, D_MODEL), BF16),
            jax.ShapeDtypeStruct((BATCH, N_EXPERTS, CAP_CTX, D_MODEL), BF16),
        ],
        scratch_shapes=[pltpu.VMEM((D_MODEL, D_MODEL), BF16)] * 3,
        compiler_params=pltpu.CompilerParams(vmem_limit_bytes=VMEM_LIMIT),
        name="moe_ffn",
    )(xe_l, xe_c, wg, wu, wd)


def _moe_combine_kernel(starts_ref, pos_ref, aff_ref, yl_ref, yc_ref, x_ref, mod_ref, modc_ref, o_ref,
                        gw_ref, yw_ref, gl_ref, gc_ref, *, tiles, with_context):
    b = pl.program_id(0)
    step = pl.program_id(1)
    wins = [_all_windows(starts_ref, b, step * tiles + k) for k in range(tiles)]
    lat_fit = functools.reduce(jnp.logical_and, [fits for _, fits in wins[:-1]])
    all_fit = jnp.logical_and(lat_fit, wins[-1][1])

    def gates(g_ref, row0, cap, k, e, first_slot):
        slot = lax.broadcasted_iota(I32, (cap, ROW_TILE), 0)
        g_ref[row0 + e * cap:row0 + (e + 1) * cap, :] = jnp.where(
            slot == pos_ref[0, e:e + 1, _tile_rows(k)] - first_slot,
            aff_ref[0, e:e + 1, _tile_rows(k)], 0.0).astype(BF16)

    def finish(k, moe, m_ref):
        o_ref[0, _tile_rows(k), :] = x_ref[0, _tile_rows(k), :] + m_ref[5:6, :] * moe

    def windowed(k):
        starts = wins[k][0]
        base = k * N_EXPERTS * WIN_SLOTS
        for e in range(N_EXPERTS):
            gates(gw_ref, base, WIN_SLOTS, k, e, starts[e])
            row0 = pl.multiple_of(e * CAP_LAT + starts[e], SLOT_ALIGN)
            yw_ref[base + e * WIN_SLOTS:base + (e + 1) * WIN_SLOTS, :] = yl_ref[0, pl.ds(row0, WIN_SLOTS), :]
        finish(k, _dot_tn(gw_ref[base:base + N_EXPERTS * WIN_SLOTS, :],
                          yw_ref[base:base + N_EXPERTS * WIN_SLOTS, :]), mod_ref)

    def full(k):
        for e in range(N_EXPERTS):
            gates(gl_ref, 0, CAP_LAT, k, e, 0)
        finish(k, _dot_tn(gl_ref[...], yl_ref[0]), mod_ref)

    def context(k):
        for e in range(N_EXPERTS):
            gates(gc_ref, 0, CAP_CTX, k, e, 0)
        finish(k, _dot_tn(gc_ref[...], yc_ref[0]), modc_ref)

    last = step == pl.num_programs(1) - 1 if with_context else False
    not_last = jnp.logical_not(last) if with_context else True

    @pl.when(jnp.logical_and(not_last, all_fit))
    def _windowed_step():
        for k in range(tiles):
            windowed(k)

    @pl.when(jnp.logical_and(not_last, jnp.logical_not(all_fit)))
    def _full_step():
        for k in range(tiles):
            full(k)

    if with_context:
        @pl.when(jnp.logical_and(last, lat_fit))
        def _windowed_last_step():
            for k in range(tiles - 1):
                windowed(k)
            context(tiles - 1)

        @pl.when(jnp.logical_and(last, jnp.logical_not(lat_fit)))
        def _full_last_step():
            for k in range(tiles - 1):
                full(k)
            context(tiles - 1)


def _moe_combine(starts, pos, aff_t, ye_l, ye_c, x1, mod, layer, with_context):
    tiles = TILES_PER_STEP if with_context else 2
    rows = TOK if with_context else SEQ
    step = pl.BlockSpec((1, tiles * ROW_TILE, D_MODEL), lambda b, t, s: (b, t, 0))
    sel = pl.BlockSpec((1, N_EXPERTS, tiles * ROW_TILE), lambda b, t, s: (b, 0, t))
    return pl.pallas_call(
        functools.partial(_moe_combine_kernel, tiles=tiles, with_context=with_context),
        grid_spec=pltpu.PrefetchScalarGridSpec(
            num_scalar_prefetch=1,
            grid=(BATCH, rows // (tiles * ROW_TILE)),
            in_specs=[
                sel, sel,
                pl.BlockSpec((1, N_EXPERTS * CAP_LAT, D_MODEL), lambda b, t, s: (b, 0, 0)),
                pl.BlockSpec((1, N_EXPERTS * CAP_CTX, D_MODEL), lambda b, t, s: (b, 0, 0)),
                step,
                pl.BlockSpec((None, None, N_ADA, D_MODEL), lambda b, t, s: (layer, b, 0, 0)),
                pl.BlockSpec((None, None, N_ADA, D_MODEL), lambda b, t, s: (layer, BATCH, 0, 0)),
            ],
            out_specs=step,
            scratch_shapes=[
                pltpu.VMEM((tiles * N_EXPERTS * WIN_SLOTS, ROW_TILE), BF16),
                pltpu.VMEM((tiles * N_EXPERTS * WIN_SLOTS, D_MODEL), BF16),
                pltpu.VMEM((N_EXPERTS * CAP_LAT, ROW_TILE), BF16),
                pltpu.VMEM((N_EXPERTS * CAP_CTX, ROW_TILE), BF16),
            ],
        ),
        out_shape=jax.ShapeDtypeStruct((BATCH, rows, D_MODEL), F32),
        compiler_params=pltpu.CompilerParams(vmem_limit_bytes=VMEM_LIMIT),
        name="moe_combine",
    )(starts, pos, aff_t, ye_l.reshape(BATCH, N_EXPERTS * CAP_LAT, D_MODEL),
      ye_c.reshape(BATCH, N_EXPERTS * CAP_CTX, D_MODEL), x1, mod, mod)


def _rope_tables():
    rows = SEQ // GRID_W
    row = jnp.repeat(jnp.arange(rows, dtype=F32), GRID_W)
    col = jnp.tile(jnp.arange(GRID_W, dtype=F32), rows)
    axis_dim = HEAD_DIM // 2
    inv_freq = ROPE_THETA ** (-jnp.arange(0, axis_dim, 2, dtype=F32) / axis_dim)
    ang_r = row[:, None] * inv_freq[None, :]
    ang_c = col[:, None] * inv_freq[None, :]
    cos_h = jnp.concatenate([jnp.cos(ang_r), jnp.cos(ang_r), jnp.cos(ang_c), jnp.cos(ang_c)], axis=-1)
    sin_h = jnp.concatenate([-jnp.sin(ang_r), jnp.sin(ang_r), -jnp.sin(ang_c), jnp.sin(ang_c)], axis=-1)
    cos_t = jnp.concatenate([jnp.tile(cos_h, (1, 2)), jnp.ones((CTX_LEN, LANES), F32)], axis=0)
    sin_t = jnp.concatenate([jnp.tile(sin_h, (1, 2)), jnp.zeros((CTX_LEN, LANES), F32)], axis=0)
    return cos_t, sin_t


def kernel(x, c, ctx, c_ctx, ada_w, ada_b, norm1_g, norm2_g, attn_w_in, attn_w_out, qnorm_a, knorm_a,
           qnorm_b, knorm_b, sink_b, conv_w_in, conv_k, conv_b, conv_w_out, router_w, moe_w_gate,
           moe_w_up, moe_w_down):
    assert x.shape == (BATCH, SEQ, D_MODEL) and ctx.shape == (BATCH, CTX_LEN, D_MODEL)
    cos_t, sin_t = _rope_tables()
    mod_rows = 24
    c_all = jnp.concatenate([c, c_ctx[None, :], jnp.zeros((mod_rows - BATCH - 1, D_MODEL), F32)], axis=0)
    mod = _adaln(c_all, ada_w, ada_b).reshape(DEPTH, mod_rows, N_ADA, D_MODEL)
    xs = jnp.concatenate([x, ctx], axis=1)

    for i in range(DEPTH):
        j = i // 2
        last = i == DEPTH - 1
        g1 = norm1_g[i].reshape(1, D_MODEL)
        g2 = norm2_g[i].reshape(1, D_MODEL)
        rw_t = router_w[i].T
        rw_hi = rw_t.astype(BF16)
        rw_split = jnp.concatenate([rw_hi, (rw_t - rw_hi.astype(F32)).astype(BF16)], axis=0)
        if i % 2 == 0:
            tile2 = lambda g: jnp.tile(g.reshape(1, HEAD_DIM), (1, 2))
            qk_g = jnp.concatenate([tile2(qnorm_a[j]), tile2(knorm_a[j]), tile2(qnorm_b[j]),
                                    tile2(knorm_b[j])], axis=0)
            q, kv = _attn_in(xs, mod, i, g1, attn_w_in[j].astype(BF16), qk_g, cos_t, sin_t)
            o = _attention(q, kv, _attention_params(qnorm_a[j], knorm_a[j], qnorm_b[j], knorm_b[j],
                                                    sink_b[j]), N_Q_TILES)
            x1, h2, aff_t = _mixer_out(xs, mod, i, attn_w_out[j].astype(BF16), g2, rw_split, attn_o=o)
        else:
            u, bg = _conv_in(xs, mod, i, g1, conv_w_in[j].astype(BF16))
            x1, h2, aff_t = _mixer_out(xs, mod, i, conv_w_out[j].astype(BF16), g2, rw_split,
                                       conv=(u, bg, conv_k[j], conv_b[j].reshape(1, D_MODEL)))
        pos, starts = _topk(aff_t)
        pos = pos.reshape(BATCH, N_EXPERTS, TOK)
        starts = starts[:, :STARTS_STRIDE].reshape(-1)
        xe_l, xe_c = _moe_gather(starts, pos, h2)
        ye_l, ye_c = _moe_ffn(xe_l, xe_c, i, moe_w_gate, moe_w_up, moe_w_down)
        xs = _moe_combine(starts, pos, aff_t, ye_l, ye_c, x1, mod, i, with_context=not last)
    return xs
```

```python
import functools

import jax
import jax.numpy as jnp
import numpy as np
from jax import lax
from jax.experimental import pallas as pl
from jax.experimental.pallas import tpu as pltpu

F32 = jnp.float32
BF16 = jnp.bfloat16
I32 = jnp.int32

D_MODEL = 1024
BATCH = 16
SEQ = 2048
CTX_LEN = 256
TOK = SEQ + CTX_LEN
DEPTH = 4
GRID_W = 64
HEAD_DIM = 64
N_Q_HEADS = 8
N_KV_HEADS = 2
Q_WIDTH = N_Q_HEADS * HEAD_DIM
KV_WIDTH = N_KV_HEADS * HEAD_DIM
ATT_IN_WIDTH = 2 * (Q_WIDTH + 2 * KV_WIDTH)
WINDOW = 128
ROPE_THETA = 10000.0
N_EXPERTS = 16
CAP_LAT = 2 * SEQ // N_EXPERTS
CAP_CTX = 2 * CTX_LEN // N_EXPERTS
N_ADA = 6
EPS = 1e-6
NEG_INF = -1e30

LANES = 128
ROW_TILE = 256
TILES_PER_STEP = 3
STEP_ROWS = TILES_PER_STEP * ROW_TILE
STEPS_PER_SAMPLE = (SEQ + CTX_LEN) // STEP_ROWS
N_ROW_TILES = TOK // ROW_TILE
N_LAT_ROW_TILES = SEQ // ROW_TILE
Q_TILE = 256
N_Q_TILES = TOK // Q_TILE
N_LAT_Q_TILES = SEQ // Q_TILE
WIN_Q_ROWS = 128
WIN_SPAN = WIN_Q_ROWS + 2 * WINDOW
HALO = 16
PREFIX_CHUNK = 256
SLOT_ALIGN = 16
WIN_SLOTS = 64
STARTS_STRIDE = 16
FFN_SAMPLES = 4
VMEM_LIMIT = 56 * 1024 * 1024
VMEM_LIMIT_FUSED = 62 * 1024 * 1024

LOG2E = 1.4426950408889634
PAR_SHIFT = 8
PAR_BOUNDED = 10
PAR_LEN = 16
MAX_SHIFT_SPAN = 100.0

HIGHEST = lax.Precision.HIGHEST


def _dot(a, b, precision=None):
    return jnp.dot(a, b, preferred_element_type=F32, precision=precision)


def _dot_nt(a, b, precision=None):
    return lax.dot_general(a, b, (((1,), (1,)), ((), ())), preferred_element_type=F32,
                           precision=precision)


def _dot_tn(a, b):
    return lax.dot_general(a, b, (((0,), (0,)), ((), ())), preferred_element_type=F32)


def _rmsnorm_rows(x, g):
    return x * lax.rsqrt(jnp.mean(x * x, axis=-1, keepdims=True) + EPS) * g


def _silu(x):
    return x * (1.0 / (1.0 + jnp.exp(-x)))


def _adaln_kernel(c_ref, w_ref, b_ref, o_ref):
    o_ref[0] = _dot(_silu(c_ref[...]), w_ref[0], precision=HIGHEST) + b_ref[0]


def _adaln(c_all, ada_w, ada_b):
    rows = c_all.shape[0]
    return pl.pallas_call(
        _adaln_kernel,
        grid=(DEPTH, N_ADA),
        in_specs=[
            pl.BlockSpec((rows, D_MODEL), lambda i, j: (0, 0)),
            pl.BlockSpec((1, D_MODEL, D_MODEL), lambda i, j: (i, 0, j)),
            pl.BlockSpec((1, 1, D_MODEL), lambda i, j: (i, 0, j)),
        ],
        out_specs=pl.BlockSpec((1, rows, D_MODEL), lambda i, j: (i, 0, j)),
        out_shape=jax.ShapeDtypeStruct((DEPTH, rows, N_ADA * D_MODEL), F32),
        name="adaln",
    )(c_all, ada_w, ada_b.reshape(DEPTH, 1, N_ADA * D_MODEL))


def _tile_mods(mod_ref, modc_ref):
    last_step = pl.program_id(1) == STEPS_PER_SAMPLE - 1
    mods = [mod_ref[...]] * (TILES_PER_STEP - 1)
    return mods + [jnp.where(last_step, modc_ref[...], mod_ref[...])]


def _tile_rows(k):
    return slice(k * ROW_TILE, (k + 1) * ROW_TILE)


def _modulated_norm(x_ref, g_ref, mods, shift_row, scale_row):
    hs = []
    for k, m in enumerate(mods):
        h = _rmsnorm_rows(x_ref[0, _tile_rows(k), :], g_ref[...])
        hs.append((h * (1.0 + m[scale_row:scale_row + 1, :]) + m[shift_row:shift_row + 1, :]).astype(BF16))
    return jnp.concatenate(hs, axis=0)


def _attn_in_kernel(x_ref, mod_ref, modc_ref, g_ref, w_ref, qkg_ref, cos_ref, sin_ref, q_ref, kv_ref):
    h = _modulated_norm(x_ref, g_ref, _tile_mods(mod_ref, modc_ref), 0, 1)
    p = _dot(h, w_ref[...])

    lane = lax.broadcasted_iota(I32, (STEP_ROWS, LANES), 1)
    r = lax.broadcasted_iota(I32, (LANES, LANES), 0)
    c = lax.broadcasted_iota(I32, (LANES, LANES), 1)
    seg_mean = jnp.where((r >> 6) == (c >> 6), 1.0 / HEAD_DIM, 0.0).astype(BF16)
    cos = cos_ref[...]
    sin = sin_ref[...]
    first_half = (lane & (HEAD_DIM // 2 - 1)) < (HEAD_DIM // 4)
    low = lane < HEAD_DIM

    def qk_norm_rope(xc, g):
        sq = xc * xc
        sq_hi = sq.astype(BF16)
        sq_lo = (sq - sq_hi.astype(F32)).astype(BF16)
        ms = _dot(sq_hi, seg_mean) + _dot(sq_lo, seg_mean)
        xn = xc * lax.rsqrt(ms + EPS) * g
        partner = jnp.where(first_half, pltpu.roll(xn, LANES - HEAD_DIM // 4, axis=1),
                            pltpu.roll(xn, HEAD_DIM // 4, axis=1))
        return xn * cos + partner * sin

    def put_kv(kind, base, vals, pad):
        rolled = pltpu.roll(vals, HEAD_DIM, axis=1)
        kv_ref[0, kind, 0, base + 0] = jnp.where(low, vals, pad).astype(BF16)
        kv_ref[0, kind, 0, base + 1] = jnp.where(low, pad, rolled).astype(BF16)
        kv_ref[0, kind, 1, base + 0] = jnp.where(low, rolled, pad).astype(BF16)
        kv_ref[0, kind, 1, base + 1] = jnp.where(low, pad, vals).astype(BF16)

    for kind in range(2):
        col0 = kind * (Q_WIDTH + 2 * KV_WIDTH)
        gq = qkg_ref[2 * kind:2 * kind + 1, :]
        gk = qkg_ref[2 * kind + 1:2 * kind + 2, :]
        for ci in range(Q_WIDTH // LANES):
            qc = qk_norm_rope(p[:, col0 + ci * LANES:col0 + (ci + 1) * LANES], gq)
            q_ref[0, :, kind * Q_WIDTH + ci * LANES:kind * Q_WIDTH + (ci + 1) * LANES] = (
                qc * (HEAD_DIM ** -0.5 * LOG2E)).astype(BF16)
        kc = qk_norm_rope(p[:, col0 + Q_WIDTH:col0 + Q_WIDTH + KV_WIDTH], gk)
        put_kv(kind, 0, kc, 0.0)
        put_kv(kind, 2, p[:, col0 + Q_WIDTH + KV_WIDTH:col0 + Q_WIDTH + 2 * KV_WIDTH], 1.0)


def _mod_specs(layer):
    return [pl.BlockSpec((None, None, N_ADA, D_MODEL), lambda b, t, *_: (layer, b, 0, 0)),
            pl.BlockSpec((None, None, N_ADA, D_MODEL), lambda b, t, *_: (layer, BATCH, 0, 0))]


def _step_spec(width=D_MODEL):
    return pl.BlockSpec((1, STEP_ROWS, width), lambda b, t, *_: (b, t, 0))


def _const_spec(shape):
    return pl.BlockSpec(shape, lambda b, t, *_: (0,) * len(shape), pipeline_mode=pl.Buffered(1))


class _Projection:
    def __init__(self, name, body, in_specs, args, out_specs, out_shape):
        self.name, self.body, self.in_specs, self.args = name, body, in_specs, args
        self.out_specs, self.out_shape = out_specs, out_shape

    def __call__(self, xs):
        return pl.pallas_call(
            self.body,
            grid=(BATCH, STEPS_PER_SAMPLE),
            in_specs=[_step_spec()] + self.in_specs,
            out_specs=self.out_specs,
            out_shape=self.out_shape,
            compiler_params=pltpu.CompilerParams(vmem_limit_bytes=VMEM_LIMIT),
            name=self.name,
        )(xs, *self.args)


def _attn_in(mod, layer, norm_g, w_in, qk_g, cos_t, sin_t):
    return _Projection(
        "attn_in", _attn_in_kernel,
        in_specs=[
            *_mod_specs(layer),
            _const_spec((1, D_MODEL)),
            _const_spec((D_MODEL, ATT_IN_WIDTH)),
            _const_spec((4, LANES)),
            pl.BlockSpec((STEP_ROWS, LANES), lambda b, t, *_: (t, 0)),
            pl.BlockSpec((STEP_ROWS, LANES), lambda b, t, *_: (t, 0)),
        ],
        args=[mod, mod, norm_g, w_in, qk_g, cos_t, sin_t],
        out_specs=[
            _step_spec(2 * Q_WIDTH),
            pl.BlockSpec((1, 2, N_KV_HEADS, 4, STEP_ROWS, LANES), lambda b, t, *_: (b, 0, 0, 0, t, 0)),
        ],
        out_shape=[
            jax.ShapeDtypeStruct((BATCH, TOK, 2 * Q_WIDTH), BF16),
            jax.ShapeDtypeStruct((BATCH, 2, N_KV_HEADS, 4, TOK, LANES), BF16),
        ])


def _softmax_pv(parts, shift, sink=None):
    if shift is None:
        for s, _ in parts:
            mx = jnp.max(s, axis=-1, keepdims=True)
            shift = mx if shift is None else jnp.maximum(shift, mx)
        if sink is not None:
            shift = jnp.maximum(shift, sink)
    acc = None
    for s, v in parts:
        pv = _dot(jnp.exp2(s - shift).astype(BF16), v)
        acc = pv if acc is None else acc + pv
    return acc, (jnp.exp2(sink - shift) if sink is not None else None)


def _attn_body(par_ref, q_ref, kv_ref, o_ref, bounded):
    t = pl.program_id(1)

    def heads(kind, g, r0, n, parts_of, with_sink):
        c0 = kind * Q_WIDTH + g * 2 * LANES
        lhs = jnp.concatenate([q_ref[0, r0:r0 + n, c0:c0 + LANES],
                               q_ref[0, r0:r0 + n, c0 + LANES:c0 + 2 * LANES]], axis=0)
        first_pair = lax.broadcasted_iota(I32, (2 * n, 1), 0) < n
        outs = []
        for parity in range(2):
            sink = None
            if with_sink:
                sink = jnp.where(first_pair, par_ref[4 * g + parity], par_ref[4 * g + 2 + parity])
            acc, sink_term = _softmax_pv(parts_of(lhs, parity), par_ref[PAR_SHIFT + kind] if bounded
                                         else None, sink)
            denom = acc[:, (1 - parity) * HEAD_DIM:(1 - parity) * HEAD_DIM + 1]
            if sink_term is not None:
                denom = denom + sink_term
            outs.append(acc * (1.0 / denom))
        low = lax.broadcasted_iota(I32, (2 * n, LANES), 1) < HEAD_DIM
        o = jnp.where(low, outs[0], outs[1])
        o_ref[0, r0:r0 + n, c0:c0 + LANES] = o[:n].astype(BF16)
        o_ref[0, r0:r0 + n, c0 + LANES:c0 + 2 * LANES] = o[n:].astype(BF16)

    @pl.when(t < N_LAT_Q_TILES)
    def _latent():
        for g in range(N_KV_HEADS):
            heads(0, g, 0, Q_TILE, lambda lhs, parity: [(_dot_nt(lhs, kv_ref[0, 0, g, parity]),
                                                         kv_ref[0, 0, g, 2 + parity])], False)

        for sub in range(Q_TILE // WIN_Q_ROWS):
            q0 = t * Q_TILE + sub * WIN_Q_ROWS
            ks = pl.multiple_of(jnp.clip(q0 - WINDOW, 0, SEQ - WIN_SPAN), WIN_Q_ROWS)
            qi = lax.broadcasted_iota(I32, (2 * WIN_Q_ROWS, WIN_SPAN), 0) & (WIN_Q_ROWS - 1)
            kj = lax.broadcasted_iota(I32, (2 * WIN_Q_ROWS, WIN_SPAN), 1)
            band = jnp.abs(qi - kj + (q0 - ks)) <= WINDOW

            def window_parts(g):
                def parts_of(lhs, parity):
                    s_w = _dot_nt(lhs, kv_ref[0, 1, g, parity, pl.ds(ks, WIN_SPAN), :])
                    s_c = _dot_nt(lhs, kv_ref[0, 1, g, parity, SEQ:TOK, :])
                    return [(jnp.where(band, s_w, NEG_INF),
                             kv_ref[0, 1, g, 2 + parity, pl.ds(ks, WIN_SPAN), :]),
                            (s_c, kv_ref[0, 1, g, 2 + parity, SEQ:TOK, :])]
                return parts_of

            for g in range(N_KV_HEADS):
                heads(1, g, sub * WIN_Q_ROWS, WIN_Q_ROWS, window_parts(g), True)

    @pl.when(t >= N_LAT_Q_TILES)
    def _context():
        for kind in range(2):
            for g in range(N_KV_HEADS):
                heads(kind, g, 0, Q_TILE,
                      lambda lhs, parity: [(_dot_nt(lhs, kv_ref[0, kind, g, parity, SEQ:TOK, :]),
                                            kv_ref[0, kind, g, 2 + parity, SEQ:TOK, :])],
                      kind == 1)


def _attn_kernel(par_ref, q_ref, kv_ref, o_ref):
    bounded = par_ref[PAR_BOUNDED] > 0.5

    @pl.when(bounded)
    def _bounded():
        _attn_body(par_ref, q_ref, kv_ref, o_ref, True)

    @pl.when(jnp.logical_not(bounded))
    def _exact_max():
        _attn_body(par_ref, q_ref, kv_ref, o_ref, False)


def _attention_params(qg_a, kg_a, qg_b, kg_b, sink):
    def bound(qg, kg):
        return 1.01 * LOG2E * HEAD_DIM ** 0.5 * jnp.max(jnp.abs(qg)) * jnp.max(jnp.abs(kg))
    sink2 = sink * LOG2E
    shift_a = bound(qg_a, kg_a)
    shift_b = jnp.maximum(bound(qg_b, kg_b), jnp.max(sink2))
    ok = jnp.logical_and(2.0 * shift_a < MAX_SHIFT_SPAN, shift_b + bound(qg_b, kg_b) < MAX_SHIFT_SPAN)
    return jnp.concatenate([sink2, jnp.stack([shift_a, shift_b, ok.astype(F32)]),
                            jnp.zeros((PAR_LEN - PAR_BOUNDED - 1,), F32)])


def _attention(q, kv, sink, n_q_tiles):
    return pl.pallas_call(
        _attn_kernel,
        grid=(BATCH, n_q_tiles),
        in_specs=[
            pl.BlockSpec(memory_space=pltpu.SMEM),
            pl.BlockSpec((1, Q_TILE, 2 * Q_WIDTH), lambda b, t: (b, t, 0)),
            pl.BlockSpec((1, 2, N_KV_HEADS, 4, TOK, LANES), lambda b, t: (b, 0, 0, 0, 0, 0)),
        ],
        out_specs=pl.BlockSpec((1, Q_TILE, 2 * Q_WIDTH), lambda b, t: (b, t, 0)),
        out_shape=jax.ShapeDtypeStruct((BATCH, TOK, 2 * Q_WIDTH), BF16),
        compiler_params=pltpu.CompilerParams(vmem_limit_bytes=VMEM_LIMIT),
        name="attention",
    )(sink, q, kv)


def _conv_in_kernel(x_ref, mod_ref, modc_ref, g_ref, w_ref, u_ref, bg_ref):
    h = _modulated_norm(x_ref, g_ref, _tile_mods(mod_ref, modc_ref), 0, 1)
    bg_ref[0] = _dot(h, w_ref[:, 0:D_MODEL]).astype(BF16)
    cg = _dot(h, w_ref[:, D_MODEL:2 * D_MODEL])
    v = _dot(h, w_ref[:, 2 * D_MODEL:3 * D_MODEL])
    u_ref[0] = (cg * v).astype(BF16)


def _conv_in(mod, layer, norm_g, w_in):
    return _Projection(
        "conv_in", _conv_in_kernel,
        in_specs=[*_mod_specs(layer), _const_spec((1, D_MODEL)), _const_spec((D_MODEL, 3 * D_MODEL))],
        args=[mod, mod, norm_g, w_in],
        out_specs=[_step_spec(), _step_spec()],
        out_shape=[
            jax.ShapeDtypeStruct((BATCH, TOK, D_MODEL), BF16),
            jax.ShapeDtypeStruct((BATCH, TOK, D_MODEL), BF16),
        ])


def _tail(lhs, w_ref, x_ref, mod_ref, modc_ref, g_ref, rw_ref, x1_ref, h2_ref, aff_ref):
    y = _dot(lhs, w_ref[...])
    h2_hi, h2_lo = [], []
    for k, m in enumerate(_tile_mods(mod_ref, modc_ref)):
        x1 = x_ref[0, _tile_rows(k), :] + m[2:3, :] * y[_tile_rows(k)]
        x1_ref[0, _tile_rows(k), :] = x1
        h2 = _rmsnorm_rows(x1, g_ref[...])
        h2 = h2 * (1.0 + m[4:5, :]) + m[3:4, :]
        hi = h2.astype(BF16)
        h2_ref[0, _tile_rows(k), :] = hi
        h2_hi.append(hi)
        h2_lo.append((h2 - hi.astype(F32)).astype(BF16))
    rw = rw_ref[...]
    part = _dot_nt(rw, jnp.concatenate(h2_hi, axis=0))
    logits = (part[0:N_EXPERTS] + part[N_EXPERTS:2 * N_EXPERTS]
              + _dot_nt(rw[0:N_EXPERTS], jnp.concatenate(h2_lo, axis=0)))
    e = jnp.exp(logits - jnp.max(logits, axis=0, keepdims=True))
    aff_ref[0] = e / jnp.sum(e, axis=0, keepdims=True)


def _attn_out_kernel(o_ref, w_ref, x_ref, mod_ref, modc_ref, g_ref, rw_ref, x1_ref, h2_ref, aff_ref):
    _tail(o_ref[0], w_ref, x_ref, mod_ref, modc_ref, g_ref, rw_ref, x1_ref, h2_ref, aff_ref)


def _conv_out_kernel(u_ref, up_ref, un_ref, bg_ref, ck_ref, cb_ref, w_ref, x_ref, mod_ref, modc_ref,
                     g_ref, rw_ref, x1_ref, h2_ref, aff_ref):
    u = u_ref[0].astype(F32)
    row = lax.broadcasted_iota(I32, (STEP_ROWS, 1), 0)
    pos = row + pl.program_id(1) * STEP_ROWS
    seq_first = jnp.logical_or(pos == 0, pos == SEQ)
    seq_last = jnp.logical_or(pos == SEQ - 1, pos == TOK - 1)
    u_prev = jnp.where(row == 0, up_ref[0, HALO - 1:HALO, :].astype(F32), pltpu.roll(u, 1, axis=0))
    u_next = jnp.where(row == STEP_ROWS - 1, un_ref[0, 0:1, :].astype(F32),
                       pltpu.roll(u, STEP_ROWS - 1, axis=0))
    u_prev = jnp.where(seq_first, 0.0, u_prev)
    u_next = jnp.where(seq_last, 0.0, u_next)
    y = ck_ref[0:1, :] * u_prev + ck_ref[1:2, :] * u + ck_ref[2:3, :] * u_next + cb_ref[...]
    lhs = (bg_ref[0].astype(F32) * y).astype(BF16)
    _tail(lhs, w_ref, x_ref, mod_ref, modc_ref, g_ref, rw_ref, x1_ref, h2_ref, aff_ref)


def _mixer_out(xs, mod, layer, w_out, norm_g, rw_split, attn_o=None, conv=None):
    step = pl.BlockSpec((1, STEP_ROWS, D_MODEL), lambda b, t: (b, t, 0))
    common_specs = [
        pl.BlockSpec((D_MODEL, D_MODEL), lambda b, t: (0, 0)),
        step,
        *_mod_specs(layer),
        pl.BlockSpec((1, D_MODEL), lambda b, t: (0, 0)),
        pl.BlockSpec((2 * N_EXPERTS, D_MODEL), lambda b, t: (0, 0)),
    ]
    if conv is None:
        body, specs, args = _attn_out_kernel, [step], [attn_o]
    else:
        u, bg, ck, cb = conv
        halo_per_step = STEP_ROWS // HALO
        last_halo = TOK // HALO - 1
        body = _conv_out_kernel
        specs = [
            step,
            pl.BlockSpec((1, HALO, D_MODEL), lambda b, t: (b, jnp.maximum(t * halo_per_step - 1, 0), 0)),
            pl.BlockSpec((1, HALO, D_MODEL),
                         lambda b, t: (b, jnp.minimum((t + 1) * halo_per_step, last_halo), 0)),
            step,
            pl.BlockSpec((3, D_MODEL), lambda b, t: (0, 0)),
            pl.BlockSpec((1, D_MODEL), lambda b, t: (0, 0)),
        ]
        args = [u, u, u, bg, ck, cb]
    return pl.pallas_call(
        body,
        grid=(BATCH, STEPS_PER_SAMPLE),
        in_specs=specs + common_specs,
        out_specs=[
            step,
            step,
            pl.BlockSpec((1, N_EXPERTS, STEP_ROWS), lambda b, t: (b, 0, t)),
        ],
        out_shape=[
            jax.ShapeDtypeStruct((BATCH, TOK, D_MODEL), F32),
            jax.ShapeDtypeStruct((BATCH, TOK, D_MODEL), BF16),
            jax.ShapeDtypeStruct((BATCH, N_EXPERTS, TOK), F32),
        ],
        compiler_params=pltpu.CompilerParams(vmem_limit_bytes=VMEM_LIMIT),
        name="mixer_out",
    )(*args, w_out, xs, mod, mod, norm_g, rw_split)


def _select_slots(v, k):
    rows, n = v.shape
    u = pltpu.bitcast(v, I32)

    def body(i, thr):
        cand = thr | jnp.left_shift(jnp.int32(1), 30 - i)
        cnt = jnp.sum(jnp.where(u >= cand, 1.0, 0.0), axis=1, keepdims=True)
        return jnp.where(cnt >= k, cand, thr)

    thr = lax.fori_loop(0, 31, body, jnp.zeros((rows, 1), I32))
    gt = u > thr
    eq = u == thr
    need = k - jnp.sum(jnp.where(gt, 1.0, 0.0), axis=1, keepdims=True)
    r = lax.broadcasted_iota(I32, (PREFIX_CHUNK, PREFIX_CHUNK), 0)
    c = lax.broadcasted_iota(I32, (PREFIX_CHUNK, PREFIX_CHUNK), 1)
    before = jnp.where(r < c, 1.0, 0.0).astype(BF16)

    def prefix_count(flags):
        out, carry = [], jnp.zeros((rows, 1), F32)
        for j in range(n // PREFIX_CHUNK):
            chunk = flags[:, j * PREFIX_CHUNK:(j + 1) * PREFIX_CHUNK]
            out.append(_dot(chunk.astype(BF16), before) + carry)
            carry = carry + jnp.sum(chunk, axis=1, keepdims=True)
        return jnp.concatenate(out, axis=1)

    eq_rank = prefix_count(jnp.where(eq, 1.0, 0.0))
    sel = jnp.where(gt, 1.0, jnp.where(eq, jnp.where(eq_rank < need, 1.0, 0.0), 0.0))
    slot = prefix_count(sel)
    return jnp.where(sel > 0.5, slot, -1.0).astype(I32), sel


def _topk_kernel(a_ref, pos_ref, starts_ref):
    pos_l, sel_l = _select_slots(a_ref[:, 0:SEQ], CAP_LAT)
    pos_c, _ = _select_slots(a_ref[:, SEQ:TOK], CAP_CTX)
    pos_ref[:, 0:SEQ] = pos_l
    pos_ref[:, SEQ:TOK] = pos_c
    tok = lax.broadcasted_iota(I32, (SEQ, LANES), 0)
    tile = lax.broadcasted_iota(I32, (SEQ, LANES), 1)
    earlier = jnp.where(tok < tile * ROW_TILE, 1.0, 0.0).astype(BF16)
    starts_ref[...] = _dot(sel_l.astype(BF16), earlier).astype(I32)


def _topk(aff_t):
    rows = BATCH * N_EXPERTS
    return pl.pallas_call(
        _topk_kernel,
        out_shape=[jax.ShapeDtypeStruct((rows, TOK), I32), jax.ShapeDtypeStruct((rows, LANES), I32)],
        compiler_params=pltpu.CompilerParams(vmem_limit_bytes=VMEM_LIMIT),
        name="topk",
    )(aff_t.reshape(rows, TOK))


def _slot_window(starts_ref, b, e, t):
    base = (b * N_EXPERTS + e) * STARTS_STRIDE + t
    s = starts_ref[base]
    n = starts_ref[base + 1] - s
    a = jnp.minimum(s - (s & (SLOT_ALIGN - 1)), CAP_LAT - WIN_SLOTS)
    return pl.multiple_of(a, SLOT_ALIGN), (s - a + n) <= WIN_SLOTS


def _all_windows(starts_ref, b, t):
    wins = [_slot_window(starts_ref, b, e, t) for e in range(N_EXPERTS)]
    fits = functools.reduce(jnp.logical_and, [ok for _, ok in wins])
    return [a for a, _ in wins], fits


def _moe_gather_kernel(starts_ref, pos_ref, h_ref, xl_ref, xc_ref):
    b = pl.program_id(0)
    step = pl.program_id(1)
    last = step == STEPS_PER_SAMPLE - 1
    wins = [_all_windows(starts_ref, b, step * TILES_PER_STEP + k) for k in range(TILES_PER_STEP)]
    lat_fit = functools.reduce(jnp.logical_and, [fits for _, fits in wins[:-1]])
    all_fit = jnp.logical_and(lat_fit, wins[-1][1])

    def tile_pos(k, e):
        return pos_ref[0, e:e + 1, _tile_rows(k)]

    def windowed(k):
        starts = wins[k][0]
        slot = lax.broadcasted_iota(I32, (WIN_SLOTS, ROW_TILE), 0)
        picks = [jnp.where(slot == tile_pos(k, e) - starts[e], 1.0, 0.0).astype(BF16)
                 for e in range(N_EXPERTS)]
        y = _dot(jnp.concatenate(picks, axis=0), h_ref[0, _tile_rows(k), :]).astype(BF16)
        for e in range(N_EXPERTS):
            xl_ref[0, e, pl.ds(starts[e], WIN_SLOTS), :] += y[e * WIN_SLOTS:(e + 1) * WIN_SLOTS]

    def full(k):
        slot = lax.broadcasted_iota(I32, (CAP_LAT, ROW_TILE), 0)
        for e in range(N_EXPERTS):
            pick = jnp.where(slot == tile_pos(k, e), 1.0, 0.0).astype(BF16)
            xl_ref[0, e] += _dot(pick, h_ref[0, _tile_rows(k), :]).astype(BF16)

    def context(k):
        slot = lax.broadcasted_iota(I32, (CAP_CTX, ROW_TILE), 0)
        picks = [jnp.where(slot == tile_pos(k, e), 1.0, 0.0).astype(BF16) for e in range(N_EXPERTS)]
        y = _dot(jnp.concatenate(picks, axis=0), h_ref[0, _tile_rows(k), :]).astype(BF16)
        for e in range(N_EXPERTS):
            xc_ref[0, e] = y[e * CAP_CTX:(e + 1) * CAP_CTX]

    @pl.when(step == 0)
    def _zero():
        xl_ref[...] = jnp.zeros(xl_ref.shape, BF16)

    @pl.when(jnp.logical_and(jnp.logical_not(last), all_fit))
    def _windowed_step():
        for k in range(TILES_PER_STEP):
            windowed(k)

    @pl.when(jnp.logical_and(jnp.logical_not(last), jnp.logical_not(all_fit)))
    def _full_step():
        for k in range(TILES_PER_STEP):
            full(k)

    @pl.when(jnp.logical_and(last, lat_fit))
    def _windowed_last_step():
        for k in range(TILES_PER_STEP - 1):
            windowed(k)
        context(TILES_PER_STEP - 1)

    @pl.when(jnp.logical_and(last, jnp.logical_not(lat_fit)))
    def _full_last_step():
        for k in range(TILES_PER_STEP - 1):
            full(k)
        context(TILES_PER_STEP - 1)


def _moe_gather(starts, pos, h2):
    return pl.pallas_call(
        _moe_gather_kernel,
        grid_spec=pltpu.PrefetchScalarGridSpec(
            num_scalar_prefetch=1,
            grid=(BATCH, STEPS_PER_SAMPLE),
            in_specs=[
                pl.BlockSpec((1, N_EXPERTS, STEP_ROWS), lambda b, t, s: (b, 0, t)),
                pl.BlockSpec((1, STEP_ROWS, D_MODEL), lambda b, t, s: (b, t, 0)),
            ],
            out_specs=[
                pl.BlockSpec((1, N_EXPERTS, CAP_LAT, D_MODEL), lambda b, t, s: (b, 0, 0, 0)),
                pl.BlockSpec((1, N_EXPERTS, CAP_CTX, D_MODEL), lambda b, t, s: (b, 0, 0, 0)),
            ],
        ),
        out_shape=[
            jax.ShapeDtypeStruct((BATCH, N_EXPERTS, CAP_LAT, D_MODEL), BF16),
            jax.ShapeDtypeStruct((BATCH, N_EXPERTS, CAP_CTX, D_MODEL), BF16),
        ],
        compiler_params=pltpu.CompilerParams(vmem_limit_bytes=VMEM_LIMIT),
        name="moe_gather",
    )(starts, pos, h2)


def _moe_ffn_kernel(xl_ref, xc_ref, wg_ref, wu_ref, wd_ref, yl_ref, yc_ref, wg_bf, wu_bf, wd_bf):
    @pl.when(pl.program_id(1) == 0)
    def _cast_expert_weights():
        wg_bf[...] = wg_ref[0].astype(BF16)
        wu_bf[...] = wu_ref[0].astype(BF16)
        wd_bf[...] = wd_ref[0].astype(BF16)

    xe = jnp.concatenate([r[i, 0] for i in range(FFN_SAMPLES) for r in (xl_ref, xc_ref)], axis=0)
    hid = (_silu(_dot(xe, wg_bf[...])) * _dot(xe, wu_bf[...])).astype(BF16)
    ye = _dot(hid, wd_bf[...]).astype(BF16)
    cap = CAP_LAT + CAP_CTX
    for i in range(FFN_SAMPLES):
        yl_ref[i, 0] = ye[i * cap:i * cap + CAP_LAT]
        yc_ref[i, 0] = ye[i * cap + CAP_LAT:(i + 1) * cap]


def _moe_ffn(xe_l, xe_c, layer, wg, wu, wd):
    wspec = pl.BlockSpec((None, 1, D_MODEL, D_MODEL), lambda e, b: (layer, e, 0, 0))
    lat = pl.BlockSpec((FFN_SAMPLES, 1, CAP_LAT, D_MODEL), lambda e, b: (b, e, 0, 0))
    ctx = pl.BlockSpec((FFN_SAMPLES, 1, CAP_CTX, D_MODEL), lambda e, b: (b, e, 0, 0))
    return pl.pallas_call(
        _moe_ffn_kernel,
        grid=(N_EXPERTS, BATCH // FFN_SAMPLES),
        in_specs=[lat, ctx, wspec, wspec, wspec],
        out_specs=[lat, ctx],
        out_shape=[
            jax.ShapeDtypeStruct((BATCH, N_EXPERTS, CAP_LAT, D_MODEL), BF16),
            jax.ShapeDtypeStruct((BATCH, N_EXPERTS, CAP_CTX, D_MODEL), BF16),
        ],
        scratch_shapes=[pltpu.VMEM((D_MODEL, D_MODEL), BF16)] * 3,
        compiler_params=pltpu.CompilerParams(vmem_limit_bytes=VMEM_LIMIT),
        name="moe_ffn",
    )(xe_l, xe_c, wg, wu, wd)


def _moe_combine_kernel(starts_ref, pos_ref, aff_ref, yl_ref, yc_ref, x_ref, mod_ref, modc_ref, o_ref,
                        g_ref, yw_ref, *, tiles, with_context):
    b = pl.program_id(0)
    step = pl.program_id(1)
    wins = [_all_windows(starts_ref, b, step * tiles + k) for k in range(tiles)]
    lat_fit = functools.reduce(jnp.logical_and, [fits for _, fits in wins[:-1]])
    all_fit = jnp.logical_and(lat_fit, wins[-1][1])

    def gates(g_ref, row0, cap, k, e, first_slot):
        slot = lax.broadcasted_iota(I32, (cap, ROW_TILE), 0)
        g_ref[row0 + e * cap:row0 + (e + 1) * cap, :] = jnp.where(
            slot == pos_ref[0, e:e + 1, _tile_rows(k)] - first_slot,
            aff_ref[0, e:e + 1, _tile_rows(k)], 0.0).astype(BF16)

    def finish(k, moe, m_ref):
        o_ref[0, _tile_rows(k), :] = x_ref[0, _tile_rows(k), :] + m_ref[5:6, :] * moe

    def windowed(k):
        starts = wins[k][0]
        base = k * N_EXPERTS * WIN_SLOTS
        for e in range(N_EXPERTS):
            gates(g_ref, base, WIN_SLOTS, k, e, starts[e])
            row0 = pl.multiple_of(e * CAP_LAT + starts[e], SLOT_ALIGN)
            yw_ref[base + e * WIN_SLOTS:base + (e + 1) * WIN_SLOTS, :] = yl_ref[0, pl.ds(row0, WIN_SLOTS), :]
        finish(k, _dot_tn(g_ref[base:base + N_EXPERTS * WIN_SLOTS, :],
                          yw_ref[base:base + N_EXPERTS * WIN_SLOTS, :]), mod_ref)

    def full(k):
        for e in range(N_EXPERTS):
            gates(g_ref, 0, CAP_LAT, k, e, 0)
        finish(k, _dot_tn(g_ref[...], yl_ref[0]), mod_ref)

    def context(k):
        base = N_EXPERTS * (CAP_LAT - CAP_CTX)
        for e in range(N_EXPERTS):
            gates(g_ref, base, CAP_CTX, k, e, 0)
        finish(k, _dot_tn(g_ref[base:base + N_EXPERTS * CAP_CTX, :], yc_ref[0]), modc_ref)

    last = step == pl.num_programs(1) - 1 if with_context else False
    not_last = jnp.logical_not(last) if with_context else True

    @pl.when(jnp.logical_and(not_last, all_fit))
    def _windowed_step():
        for k in range(tiles):
            windowed(k)

    @pl.when(jnp.logical_and(not_last, jnp.logical_not(all_fit)))
    def _full_step():
        for k in range(tiles):
            full(k)

    if with_context:
        @pl.when(jnp.logical_and(last, lat_fit))
        def _windowed_last_step():
            for k in range(tiles - 1):
                windowed(k)
            context(tiles - 1)

        @pl.when(jnp.logical_and(last, jnp.logical_not(lat_fit)))
        def _full_last_step():
            for k in range(tiles - 1):
                full(k)
            context(tiles - 1)


N_COMBINE_INPUTS = 8
N_COMBINE_SCRATCH = 2


def _combine_project_kernel(*refs, projection_body, n_proj_inputs):
    n_in = N_COMBINE_INPUTS + n_proj_inputs
    combine_in, proj_in = refs[:N_COMBINE_INPUTS], refs[N_COMBINE_INPUTS:n_in]
    x_out, proj_out = refs[n_in], refs[n_in + 1:len(refs) - N_COMBINE_SCRATCH]
    scratch = refs[len(refs) - N_COMBINE_SCRATCH:]
    _moe_combine_kernel(*combine_in, x_out, *scratch, tiles=TILES_PER_STEP, with_context=True)
    projection_body(x_out, *proj_in, *proj_out)


def _moe_combine(starts, pos, aff_t, ye_l, ye_c, x1, mod, layer, with_context, projection=None):
    tiles = TILES_PER_STEP if with_context else 2
    rows = TOK if with_context else SEQ
    step = pl.BlockSpec((1, tiles * ROW_TILE, D_MODEL), lambda b, t, s: (b, t, 0))
    sel = pl.BlockSpec((1, N_EXPERTS, tiles * ROW_TILE), lambda b, t, s: (b, 0, t))
    x_shape = jax.ShapeDtypeStruct((BATCH, rows, D_MODEL), F32)
    if projection is None:
        body = functools.partial(_moe_combine_kernel, tiles=tiles, with_context=with_context)
        extra_specs, extra_args, out_specs, out_shape = [], [], step, x_shape
    else:
        assert with_context and tiles * ROW_TILE == STEP_ROWS
        body = functools.partial(_combine_project_kernel, projection_body=projection.body,
                                 n_proj_inputs=len(projection.args))
        extra_specs, extra_args = projection.in_specs, projection.args
        out_specs, out_shape = [step] + projection.out_specs, [x_shape] + projection.out_shape
    return pl.pallas_call(
        body,
        grid_spec=pltpu.PrefetchScalarGridSpec(
            num_scalar_prefetch=1,
            grid=(BATCH, rows // (tiles * ROW_TILE)),
            in_specs=[
                sel, sel,
                pl.BlockSpec((1, N_EXPERTS * CAP_LAT, D_MODEL), lambda b, t, s: (b, 0, 0)),
                pl.BlockSpec((1, N_EXPERTS * CAP_CTX, D_MODEL), lambda b, t, s: (b, 0, 0)),
                step,
                *_mod_specs(layer),
            ] + extra_specs,
            out_specs=out_specs,
            scratch_shapes=[
                pltpu.VMEM((N_EXPERTS * CAP_LAT, ROW_TILE), BF16),
                pltpu.VMEM((tiles * N_EXPERTS * WIN_SLOTS, D_MODEL), BF16),
            ],
        ),
        out_shape=out_shape,
        compiler_params=pltpu.CompilerParams(vmem_limit_bytes=VMEM_LIMIT_FUSED),
        name="moe_combine" if projection is None else "moe_combine_" + projection.name,
    )(starts, pos, aff_t, ye_l.reshape(BATCH, N_EXPERTS * CAP_LAT, D_MODEL),
      ye_c.reshape(BATCH, N_EXPERTS * CAP_CTX, D_MODEL), x1, mod, mod, *extra_args)


def _rope_tables():
    rows = SEQ // GRID_W
    row = jnp.repeat(jnp.arange(rows, dtype=F32), GRID_W)
    col = jnp.tile(jnp.arange(GRID_W, dtype=F32), rows)
    axis_dim = HEAD_DIM // 2
    inv_freq = ROPE_THETA ** (-jnp.arange(0, axis_dim, 2, dtype=F32) / axis_dim)
    ang_r = row[:, None] * inv_freq[None, :]
    ang_c = col[:, None] * inv_freq[None, :]
    cos_h = jnp.concatenate([jnp.cos(ang_r), jnp.cos(ang_r), jnp.cos(ang_c), jnp.cos(ang_c)], axis=-1)
    sin_h = jnp.concatenate([-jnp.sin(ang_r), jnp.sin(ang_r), -jnp.sin(ang_c), jnp.sin(ang_c)], axis=-1)
    cos_t = jnp.concatenate([jnp.tile(cos_h, (1, 2)), jnp.ones((CTX_LEN, LANES), F32)], axis=0)
    sin_t = jnp.concatenate([jnp.tile(sin_h, (1, 2)), jnp.zeros((CTX_LEN, LANES), F32)], axis=0)
    return cos_t, sin_t


def kernel(x, c, ctx, c_ctx, ada_w, ada_b, norm1_g, norm2_g, attn_w_in, attn_w_out, qnorm_a, knorm_a,
           qnorm_b, knorm_b, sink_b, conv_w_in, conv_k, conv_b, conv_w_out, router_w, moe_w_gate,
           moe_w_up, moe_w_down):
    assert x.shape == (BATCH, SEQ, D_MODEL) and ctx.shape == (BATCH, CTX_LEN, D_MODEL)
    cos_t, sin_t = _rope_tables()
    mod_rows = 24
    c_all = jnp.concatenate([c, c_ctx[None, :], jnp.zeros((mod_rows - BATCH - 1, D_MODEL), F32)], axis=0)
    mod = _adaln(c_all, ada_w, ada_b).reshape(DEPTH, mod_rows, N_ADA, D_MODEL)
    xs = jnp.concatenate([x, ctx], axis=1)

    def projection(i):
        j = i // 2
        g1 = norm1_g[i].reshape(1, D_MODEL)
        if i % 2 == 0:
            tile2 = lambda g: jnp.tile(g.reshape(1, HEAD_DIM), (1, 2))
            qk_g = jnp.concatenate([tile2(qnorm_a[j]), tile2(knorm_a[j]), tile2(qnorm_b[j]),
                                    tile2(knorm_b[j])], axis=0)
            return _attn_in(mod, i, g1, attn_w_in[j].astype(BF16), qk_g, cos_t, sin_t)
        return _conv_in(mod, i, g1, conv_w_in[j].astype(BF16))

    proj = projection(0)(xs)
    for i in range(DEPTH):
        j = i // 2
        last = i == DEPTH - 1
        g2 = norm2_g[i].reshape(1, D_MODEL)
        rw_t = router_w[i].T
        rw_hi = rw_t.astype(BF16)
        rw_split = jnp.concatenate([rw_hi, (rw_t - rw_hi.astype(F32)).astype(BF16)], axis=0)
        if i % 2 == 0:
            q, kv = proj
            o = _attention(q, kv, _attention_params(qnorm_a[j], knorm_a[j], qnorm_b[j], knorm_b[j],
                                                    sink_b[j]), N_Q_TILES)
            x1, h2, aff_t = _mixer_out(xs, mod, i, attn_w_out[j].astype(BF16), g2, rw_split, attn_o=o)
        else:
            u, bg = proj
            x1, h2, aff_t = _mixer_out(xs, mod, i, conv_w_out[j].astype(BF16), g2, rw_split,
                                       conv=(u, bg, conv_k[j], conv_b[j].reshape(1, D_MODEL)))
        pos, starts = _topk(aff_t)
        pos = pos.reshape(BATCH, N_EXPERTS, TOK)
        starts = starts[:, :STARTS_STRIDE].reshape(-1)
        xe_l, xe_c = _moe_gather(starts, pos, h2)
        ye_l, ye_c = _moe_ffn(xe_l, xe_c, i, moe_w_gate, moe_w_up, moe_w_down)
        if last:
            return _moe_combine(starts, pos, aff_t, ye_l, ye_c, x1, mod, i, with_context=False)
        xs, *proj = _moe_combine(starts, pos, aff_t, ye_l, ye_c, x1, mod, i, with_context=True,
                                 projection=projection(i + 1))
```

```python
import functools

import jax
import jax.numpy as jnp
import numpy as np
from jax import lax
from jax.experimental import pallas as pl
from jax.experimental.pallas import tpu as pltpu

F32 = jnp.float32
BF16 = jnp.bfloat16
I32 = jnp.int32

D_MODEL = 1024
BATCH = 16
SEQ = 2048
CTX_LEN = 256
TOK = SEQ + CTX_LEN
DEPTH = 4
GRID_W = 64
HEAD_DIM = 64
N_Q_HEADS = 8
N_KV_HEADS = 2
Q_WIDTH = N_Q_HEADS * HEAD_DIM
KV_WIDTH = N_KV_HEADS * HEAD_DIM
ATT_IN_WIDTH = 2 * (Q_WIDTH + 2 * KV_WIDTH)
WINDOW = 128
ROPE_THETA = 10000.0
N_EXPERTS = 16
CAP_LAT = 2 * SEQ // N_EXPERTS
CAP_CTX = 2 * CTX_LEN // N_EXPERTS
N_ADA = 6
EPS = 1e-6
NEG_INF = -1e30

LANES = 128
ROW_TILE = 256
TILES_PER_STEP = 3
STEP_ROWS = TILES_PER_STEP * ROW_TILE
STEPS_PER_SAMPLE = (SEQ + CTX_LEN) // STEP_ROWS
N_ROW_TILES = TOK // ROW_TILE
N_LAT_ROW_TILES = SEQ // ROW_TILE
Q_TILE = 256
N_Q_TILES = TOK // Q_TILE
N_LAT_Q_TILES = SEQ // Q_TILE
WIN_Q_ROWS = 128
WIN_SPAN = WIN_Q_ROWS + 2 * WINDOW
HALO = 16
PREFIX_CHUNK = 256
SLOT_ALIGN = 16
WIN_SLOTS = 64
STARTS_STRIDE = 16
FFN_SAMPLES = 4
VMEM_LIMIT = 56 * 1024 * 1024
VMEM_LIMIT_FUSED = 62 * 1024 * 1024

LOG2E = 1.4426950408889634
PAR_SHIFT = 8
PAR_BOUNDED = 10
PAR_LEN = 16
MAX_SHIFT_SPAN = 100.0

HIGHEST = lax.Precision.HIGHEST


def _dot(a, b, precision=None):
    return jnp.dot(a, b, preferred_element_type=F32, precision=precision)


def _dot_nt(a, b, precision=None):
    return lax.dot_general(a, b, (((1,), (1,)), ((), ())), preferred_element_type=F32,
                           precision=precision)


def _dot_tn(a, b):
    return lax.dot_general(a, b, (((0,), (0,)), ((), ())), preferred_element_type=F32)


def _rmsnorm_rows(x, g):
    return x * lax.rsqrt(jnp.mean(x * x, axis=-1, keepdims=True) + EPS) * g


def _silu(x):
    return x * (1.0 / (1.0 + jnp.exp(-x)))


def _adaln_kernel(c_ref, w_ref, b_ref, o_ref):
    o_ref[0] = _dot(_silu(c_ref[...]), w_ref[0], precision=HIGHEST) + b_ref[0]


def _adaln(c_all, ada_w, ada_b):
    rows = c_all.shape[0]
    return pl.pallas_call(
        _adaln_kernel,
        grid=(DEPTH, N_ADA),
        in_specs=[
            pl.BlockSpec((rows, D_MODEL), lambda i, j: (0, 0)),
            pl.BlockSpec((1, D_MODEL, D_MODEL), lambda i, j: (i, 0, j)),
            pl.BlockSpec((1, 1, D_MODEL), lambda i, j: (i, 0, j)),
        ],
        out_specs=pl.BlockSpec((1, rows, D_MODEL), lambda i, j: (i, 0, j)),
        out_shape=jax.ShapeDtypeStruct((DEPTH, rows, N_ADA * D_MODEL), F32),
        name="adaln",
    )(c_all, ada_w, ada_b.reshape(DEPTH, 1, N_ADA * D_MODEL))


def _tile_mods(mod_ref, modc_ref):
    last_step = pl.program_id(1) == STEPS_PER_SAMPLE - 1
    mods = [mod_ref[...]] * (TILES_PER_STEP - 1)
    return mods + [jnp.where(last_step, modc_ref[...], mod_ref[...])]


def _tile_rows(k):
    return slice(k * ROW_TILE, (k + 1) * ROW_TILE)


def _modulated_norm(x_ref, g_ref, mods, shift_row, scale_row):
    hs = []
    for k, m in enumerate(mods):
        h = _rmsnorm_rows(x_ref[0, _tile_rows(k), :], g_ref[...])
        hs.append((h * (1.0 + m[scale_row:scale_row + 1, :]) + m[shift_row:shift_row + 1, :]).astype(BF16))
    return jnp.concatenate(hs, axis=0)


def _attn_in_kernel(x_ref, mod_ref, modc_ref, g_ref, w_ref, qkg_ref, cos_ref, sin_ref, q_ref, kv_ref):
    h = _modulated_norm(x_ref, g_ref, _tile_mods(mod_ref, modc_ref), 0, 1)
    p = _dot(h, w_ref[...])

    lane = lax.broadcasted_iota(I32, (STEP_ROWS, LANES), 1)
    r = lax.broadcasted_iota(I32, (LANES, LANES), 0)
    c = lax.broadcasted_iota(I32, (LANES, LANES), 1)
    seg_mean = jnp.where((r >> 6) == (c >> 6), 1.0 / HEAD_DIM, 0.0).astype(BF16)
    cos = cos_ref[...]
    sin = sin_ref[...]
    first_half = (lane & (HEAD_DIM // 2 - 1)) < (HEAD_DIM // 4)
    low = lane < HEAD_DIM

    def qk_norm_rope(xc, g):
        ms = _dot((xc * xc).astype(BF16), seg_mean)
        xn = xc * lax.rsqrt(ms + EPS) * g
        partner = jnp.where(first_half, pltpu.roll(xn, LANES - HEAD_DIM // 4, axis=1),
                            pltpu.roll(xn, HEAD_DIM // 4, axis=1))
        return xn * cos + partner * sin

    def put_kv(kind, base, vals, pad):
        rolled = pltpu.roll(vals, HEAD_DIM, axis=1)
        kv_ref[0, kind, 0, base + 0] = jnp.where(low, vals, pad).astype(BF16)
        kv_ref[0, kind, 0, base + 1] = jnp.where(low, pad, rolled).astype(BF16)
        kv_ref[0, kind, 1, base + 0] = jnp.where(low, rolled, pad).astype(BF16)
        kv_ref[0, kind, 1, base + 1] = jnp.where(low, pad, vals).astype(BF16)

    for kind in range(2):
        col0 = kind * (Q_WIDTH + 2 * KV_WIDTH)
        gq = qkg_ref[2 * kind:2 * kind + 1, :]
        gk = qkg_ref[2 * kind + 1:2 * kind + 2, :]
        for ci in range(Q_WIDTH // LANES):
            qc = qk_norm_rope(p[:, col0 + ci * LANES:col0 + (ci + 1) * LANES], gq)
            q_ref[0, :, kind * Q_WIDTH + ci * LANES:kind * Q_WIDTH + (ci + 1) * LANES] = (
                qc * (HEAD_DIM ** -0.5 * LOG2E)).astype(BF16)
        kc = qk_norm_rope(p[:, col0 + Q_WIDTH:col0 + Q_WIDTH + KV_WIDTH], gk)
        put_kv(kind, 0, kc, 0.0)
        put_kv(kind, 2, p[:, col0 + Q_WIDTH + KV_WIDTH:col0 + Q_WIDTH + 2 * KV_WIDTH], 1.0)


def _mod_specs(layer):
    return [pl.BlockSpec((None, None, N_ADA, D_MODEL), lambda b, t, *_: (layer, b, 0, 0)),
            pl.BlockSpec((None, None, N_ADA, D_MODEL), lambda b, t, *_: (layer, BATCH, 0, 0))]


def _step_spec(width=D_MODEL):
    return pl.BlockSpec((1, STEP_ROWS, width), lambda b, t, *_: (b, t, 0))


def _const_spec(shape):
    return pl.BlockSpec(shape, lambda b, t, *_: (0,) * len(shape), pipeline_mode=pl.Buffered(1))


class _SplitStream:
    def __init__(self, x_ref, ctx_ref):
        self.x_ref, self.ctx_ref = x_ref, ctx_ref

    def __getitem__(self, idx):
        _, rows, cols = idx
        assert (rows.stop - rows.start) == ROW_TILE and rows.start % ROW_TILE == 0
        if rows.start != (TILES_PER_STEP - 1) * ROW_TILE:
            return self.x_ref[0, rows, cols]
        last_step = pl.program_id(1) == STEPS_PER_SAMPLE - 1
        return jnp.where(last_step, self.ctx_ref[0, :, cols], self.x_ref[0, rows, cols])


def _split_stream_body(body):
    return lambda x_ref, ctx_ref, *refs: body(_SplitStream(x_ref, ctx_ref), *refs)


def _ctx_spec():
    return pl.BlockSpec((1, CTX_LEN, D_MODEL), lambda b, t, *_: (b, 0, 0))


class _Projection:
    def __init__(self, name, body, in_specs, args, out_specs, out_shape):
        self.name, self.body, self.in_specs, self.args = name, body, in_specs, args
        self.out_specs, self.out_shape = out_specs, out_shape

    def __call__(self, x, ctx):
        return pl.pallas_call(
            _split_stream_body(self.body),
            grid=(BATCH, STEPS_PER_SAMPLE),
            in_specs=[_step_spec(), _ctx_spec()] + self.in_specs,
            out_specs=self.out_specs,
            out_shape=self.out_shape,
            compiler_params=pltpu.CompilerParams(vmem_limit_bytes=VMEM_LIMIT),
            name=self.name,
        )(x, ctx, *self.args)


def _attn_in(mod, layer, norm_g, w_in, qk_g, cos_t, sin_t):
    return _Projection(
        "attn_in", _attn_in_kernel,
        in_specs=[
            *_mod_specs(layer),
            _const_spec((1, D_MODEL)),
            _const_spec((D_MODEL, ATT_IN_WIDTH)),
            _const_spec((4, LANES)),
            pl.BlockSpec((STEP_ROWS, LANES), lambda b, t, *_: (t, 0)),
            pl.BlockSpec((STEP_ROWS, LANES), lambda b, t, *_: (t, 0)),
        ],
        args=[mod, mod, norm_g, w_in, qk_g, cos_t, sin_t],
        out_specs=[
            _step_spec(2 * Q_WIDTH),
            pl.BlockSpec((1, 2, N_KV_HEADS, 4, STEP_ROWS, LANES), lambda b, t, *_: (b, 0, 0, 0, t, 0)),
        ],
        out_shape=[
            jax.ShapeDtypeStruct((BATCH, TOK, 2 * Q_WIDTH), BF16),
            jax.ShapeDtypeStruct((BATCH, 2, N_KV_HEADS, 4, TOK, LANES), BF16),
        ])


def _softmax_pv(parts, shift, sink=None):
    if shift is None:
        for s, _ in parts:
            mx = jnp.max(s, axis=-1, keepdims=True)
            shift = mx if shift is None else jnp.maximum(shift, mx)
        if sink is not None:
            shift = jnp.maximum(shift, sink)
    acc = None
    for s, v in parts:
        pv = _dot(jnp.exp2(s - shift).astype(BF16), v)
        acc = pv if acc is None else acc + pv
    return acc, (jnp.exp2(sink - shift) if sink is not None else None)


def _attn_body(par_ref, q_ref, kv_ref, o_ref, bounded):
    t = pl.program_id(1)

    def heads(kind, g, r0, n, parts_of, with_sink):
        c0 = kind * Q_WIDTH + g * 2 * LANES
        lhs = jnp.concatenate([q_ref[0, r0:r0 + n, c0:c0 + LANES],
                               q_ref[0, r0:r0 + n, c0 + LANES:c0 + 2 * LANES]], axis=0)
        first_pair = lax.broadcasted_iota(I32, (2 * n, 1), 0) < n
        outs = []
        for parity in range(2):
            sink = None
            if with_sink:
                sink = jnp.where(first_pair, par_ref[4 * g + parity], par_ref[4 * g + 2 + parity])
            acc, sink_term = _softmax_pv(parts_of(lhs, parity), par_ref[PAR_SHIFT + kind] if bounded
                                         else None, sink)
            denom = acc[:, (1 - parity) * HEAD_DIM:(1 - parity) * HEAD_DIM + 1]
            if sink_term is not None:
                denom = denom + sink_term
            outs.append(acc * (1.0 / denom))
        low = lax.broadcasted_iota(I32, (2 * n, LANES), 1) < HEAD_DIM
        o = jnp.where(low, outs[0], outs[1])
        o_ref[0, r0:r0 + n, c0:c0 + LANES] = o[:n].astype(BF16)
        o_ref[0, r0:r0 + n, c0 + LANES:c0 + 2 * LANES] = o[n:].astype(BF16)

    @pl.when(t < N_LAT_Q_TILES)
    def _latent():
        for g in range(N_KV_HEADS):
            heads(0, g, 0, Q_TILE, lambda lhs, parity: [(_dot_nt(lhs, kv_ref[0, 0, g, parity]),
                                                         kv_ref[0, 0, g, 2 + parity])], False)

        for sub in range(Q_TILE // WIN_Q_ROWS):
            q0 = t * Q_TILE + sub * WIN_Q_ROWS
            ks = pl.multiple_of(jnp.clip(q0 - WINDOW, 0, SEQ - WIN_SPAN), WIN_Q_ROWS)
            qi = lax.broadcasted_iota(I32, (2 * WIN_Q_ROWS, WIN_SPAN), 0) & (WIN_Q_ROWS - 1)
            kj = lax.broadcasted_iota(I32, (2 * WIN_Q_ROWS, WIN_SPAN), 1)
            band = jnp.abs(qi - kj + (q0 - ks)) <= WINDOW

            def window_parts(g):
                def parts_of(lhs, parity):
                    s_w = _dot_nt(lhs, kv_ref[0, 1, g, parity, pl.ds(ks, WIN_SPAN), :])
                    s_c = _dot_nt(lhs, kv_ref[0, 1, g, parity, SEQ:TOK, :])
                    return [(jnp.where(band, s_w, NEG_INF),
                             kv_ref[0, 1, g, 2 + parity, pl.ds(ks, WIN_SPAN), :]),
                            (s_c, kv_ref[0, 1, g, 2 + parity, SEQ:TOK, :])]
                return parts_of

            for g in range(N_KV_HEADS):
                heads(1, g, sub * WIN_Q_ROWS, WIN_Q_ROWS, window_parts(g), True)

    @pl.when(t >= N_LAT_Q_TILES)
    def _context():
        for kind in range(2):
            for g in range(N_KV_HEADS):
                heads(kind, g, 0, Q_TILE,
                      lambda lhs, parity: [(_dot_nt(lhs, kv_ref[0, kind, g, parity, SEQ:TOK, :]),
                                            kv_ref[0, kind, g, 2 + parity, SEQ:TOK, :])],
                      kind == 1)


def _attn_kernel(par_ref, q_ref, kv_ref, o_ref):
    bounded = par_ref[PAR_BOUNDED] > 0.5

    @pl.when(bounded)
    def _bounded():
        _attn_body(par_ref, q_ref, kv_ref, o_ref, True)

    @pl.when(jnp.logical_not(bounded))
    def _exact_max():
        _attn_body(par_ref, q_ref, kv_ref, o_ref, False)


def _attention_params(qg_a, kg_a, qg_b, kg_b, sink):
    def bound(qg, kg):
        return 1.02 * LOG2E * HEAD_DIM ** 0.5 * jnp.max(jnp.abs(qg)) * jnp.max(jnp.abs(kg))
    sink2 = sink * LOG2E
    shift_a = bound(qg_a, kg_a)
    shift_b = jnp.maximum(bound(qg_b, kg_b), jnp.max(sink2))
    ok = jnp.logical_and(2.0 * shift_a < MAX_SHIFT_SPAN, shift_b + bound(qg_b, kg_b) < MAX_SHIFT_SPAN)
    return jnp.concatenate([sink2, jnp.stack([shift_a, shift_b, ok.astype(F32)]),
                            jnp.zeros((PAR_LEN - PAR_BOUNDED - 1,), F32)])


def _attention(q, kv, params, n_q_tiles):
    return pl.pallas_call(
        _attn_kernel,
        grid=(BATCH, n_q_tiles),
        in_specs=[
            pl.BlockSpec(memory_space=pltpu.SMEM),
            pl.BlockSpec((1, Q_TILE, 2 * Q_WIDTH), lambda b, t: (b, t, 0)),
            pl.BlockSpec((1, 2, N_KV_HEADS, 4, TOK, LANES), lambda b, t: (b, 0, 0, 0, 0, 0)),
        ],
        out_specs=pl.BlockSpec((1, Q_TILE, 2 * Q_WIDTH), lambda b, t: (b, t, 0)),
        out_shape=jax.ShapeDtypeStruct((BATCH, TOK, 2 * Q_WIDTH), BF16),
        compiler_params=pltpu.CompilerParams(vmem_limit_bytes=VMEM_LIMIT),
        name="attention",
    )(params, q, kv)


def _conv_in_kernel(x_ref, mod_ref, modc_ref, g_ref, w_ref, u_ref, bg_ref):
    h = _modulated_norm(x_ref, g_ref, _tile_mods(mod_ref, modc_ref), 0, 1)
    bg_ref[0] = _dot(h, w_ref[:, 0:D_MODEL]).astype(BF16)
    cg = _dot(h, w_ref[:, D_MODEL:2 * D_MODEL])
    v = _dot(h, w_ref[:, 2 * D_MODEL:3 * D_MODEL])
    u_ref[0] = (cg * v).astype(BF16)


def _conv_in(mod, layer, norm_g, w_in):
    return _Projection(
        "conv_in", _conv_in_kernel,
        in_specs=[*_mod_specs(layer), _const_spec((1, D_MODEL)), _const_spec((D_MODEL, 3 * D_MODEL))],
        args=[mod, mod, norm_g, w_in],
        out_specs=[_step_spec(), _step_spec()],
        out_shape=[
            jax.ShapeDtypeStruct((BATCH, TOK, D_MODEL), BF16),
            jax.ShapeDtypeStruct((BATCH, TOK, D_MODEL), BF16),
        ])


def _tail(lhs, w_ref, x_ref, mod_ref, modc_ref, g_ref, rw_ref, x1_ref, h2_ref, aff_ref):
    y = _dot(lhs, w_ref[...])
    h2_hi, h2_lo = [], []
    for k, m in enumerate(_tile_mods(mod_ref, modc_ref)):
        x1 = x_ref[0, _tile_rows(k), :] + m[2:3, :] * y[_tile_rows(k)]
        x1_ref[0, _tile_rows(k), :] = x1
        h2 = _rmsnorm_rows(x1, g_ref[...])
        h2 = h2 * (1.0 + m[4:5, :]) + m[3:4, :]
        hi = h2.astype(BF16)
        h2_ref[0, _tile_rows(k), :] = hi
        h2_hi.append(hi)
        h2_lo.append((h2 - hi.astype(F32)).astype(BF16))
    rw = rw_ref[...]
    part = _dot_nt(rw, jnp.concatenate(h2_hi, axis=0))
    logits = (part[0:N_EXPERTS] + part[N_EXPERTS:2 * N_EXPERTS]
              + _dot_nt(rw[0:N_EXPERTS], jnp.concatenate(h2_lo, axis=0)))
    e = jnp.exp(logits - jnp.max(logits, axis=0, keepdims=True))
    aff_ref[0] = e / jnp.sum(e, axis=0, keepdims=True)


def _attn_out_kernel(o_ref, w_ref, x_ref, mod_ref, modc_ref, g_ref, rw_ref, x1_ref, h2_ref, aff_ref):
    _tail(o_ref[0], w_ref, x_ref, mod_ref, modc_ref, g_ref, rw_ref, x1_ref, h2_ref, aff_ref)


def _conv_out_kernel(u_ref, up_ref, un_ref, bg_ref, ck_ref, cb_ref, w_ref, x_ref, mod_ref, modc_ref,
                     g_ref, rw_ref, x1_ref, h2_ref, aff_ref):
    u = u_ref[0].astype(F32)
    row = lax.broadcasted_iota(I32, (STEP_ROWS, 1), 0)
    pos = row + pl.program_id(1) * STEP_ROWS
    seq_first = jnp.logical_or(pos == 0, pos == SEQ)
    seq_last = jnp.logical_or(pos == SEQ - 1, pos == TOK - 1)
    u_prev = jnp.where(row == 0, up_ref[0, HALO - 1:HALO, :].astype(F32), pltpu.roll(u, 1, axis=0))
    u_next = jnp.where(row == STEP_ROWS - 1, un_ref[0, 0:1, :].astype(F32),
                       pltpu.roll(u, STEP_ROWS - 1, axis=0))
    u_prev = jnp.where(seq_first, 0.0, u_prev)
    u_next = jnp.where(seq_last, 0.0, u_next)
    y = ck_ref[0:1, :] * u_prev + ck_ref[1:2, :] * u + ck_ref[2:3, :] * u_next + cb_ref[...]
    lhs = (bg_ref[0].astype(F32) * y).astype(BF16)
    _tail(lhs, w_ref, x_ref, mod_ref, modc_ref, g_ref, rw_ref, x1_ref, h2_ref, aff_ref)


def _attn_out_split_kernel(o_ref, w_ref, x_ref, ctx_ref, *refs):
    _attn_out_kernel(o_ref, w_ref, _SplitStream(x_ref, ctx_ref), *refs)


def _mixer_out(xs, mod, layer, w_out, norm_g, rw_split, attn_o=None, conv=None, ctx=None):
    step = pl.BlockSpec((1, STEP_ROWS, D_MODEL), lambda b, t: (b, t, 0))
    common_specs = [
        pl.BlockSpec((D_MODEL, D_MODEL), lambda b, t: (0, 0)),
        step,
        *([] if ctx is None else [_ctx_spec()]),
        *_mod_specs(layer),
        pl.BlockSpec((1, D_MODEL), lambda b, t: (0, 0)),
        pl.BlockSpec((2 * N_EXPERTS, D_MODEL), lambda b, t: (0, 0)),
    ]
    stream = [xs] if ctx is None else [xs, ctx]
    if conv is None:
        body = _attn_out_kernel if ctx is None else _attn_out_split_kernel
        specs, args = [step], [attn_o]
    else:
        assert ctx is None
        u, bg, ck, cb = conv
        halo_per_step = STEP_ROWS // HALO
        last_halo = TOK // HALO - 1
        body = _conv_out_kernel
        specs = [
            step,
            pl.BlockSpec((1, HALO, D_MODEL), lambda b, t: (b, jnp.maximum(t * halo_per_step - 1, 0), 0)),
            pl.BlockSpec((1, HALO, D_MODEL),
                         lambda b, t: (b, jnp.minimum((t + 1) * halo_per_step, last_halo), 0)),
            step,
            pl.BlockSpec((3, D_MODEL), lambda b, t: (0, 0)),
            pl.BlockSpec((1, D_MODEL), lambda b, t: (0, 0)),
        ]
        args = [u, u, u, bg, ck, cb]
    return pl.pallas_call(
        body,
        grid=(BATCH, STEPS_PER_SAMPLE),
        in_specs=specs + common_specs,
        out_specs=[
            step,
            step,
            pl.BlockSpec((1, N_EXPERTS, STEP_ROWS), lambda b, t: (b, 0, t)),
        ],
        out_shape=[
            jax.ShapeDtypeStruct((BATCH, TOK, D_MODEL), F32),
            jax.ShapeDtypeStruct((BATCH, TOK, D_MODEL), BF16),
            jax.ShapeDtypeStruct((BATCH, N_EXPERTS, TOK), F32),
        ],
        compiler_params=pltpu.CompilerParams(vmem_limit_bytes=VMEM_LIMIT),
        name="mixer_out",
    )(*args, w_out, *stream, mod, mod, norm_g, rw_split)


def _select_slots(v, k):
    rows, n = v.shape
    u = pltpu.bitcast(v, I32)

    def body(i, thr):
        cand = thr | jnp.left_shift(jnp.int32(1), 30 - i)
        cnt = jnp.sum(jnp.where(u >= cand, 1.0, 0.0), axis=1, keepdims=True)
        return jnp.where(cnt >= k, cand, thr)

    thr = lax.fori_loop(0, 31, body, jnp.zeros((rows, 1), I32))
    gt = u > thr
    eq = u == thr
    need = k - jnp.sum(jnp.where(gt, 1.0, 0.0), axis=1, keepdims=True)
    r = lax.broadcasted_iota(I32, (PREFIX_CHUNK, PREFIX_CHUNK), 0)
    c = lax.broadcasted_iota(I32, (PREFIX_CHUNK, PREFIX_CHUNK), 1)
    before = jnp.where(r < c, 1.0, 0.0).astype(BF16)

    def prefix_count(flags):
        out, carry = [], jnp.zeros((rows, 1), F32)
        for j in range(n // PREFIX_CHUNK):
            chunk = flags[:, j * PREFIX_CHUNK:(j + 1) * PREFIX_CHUNK]
            out.append(_dot(chunk.astype(BF16), before) + carry)
            carry = carry + jnp.sum(chunk, axis=1, keepdims=True)
        return jnp.concatenate(out, axis=1)

    eq_rank = prefix_count(jnp.where(eq, 1.0, 0.0))
    sel = jnp.where(gt, 1.0, jnp.where(eq, jnp.where(eq_rank < need, 1.0, 0.0), 0.0))
    slot = prefix_count(sel)
    return jnp.where(sel > 0.5, slot, -1.0).astype(I32), sel


def _topk_kernel(a_ref, pos_ref, starts_ref):
    pos_l, sel_l = _select_slots(a_ref[:, 0:SEQ], CAP_LAT)
    pos_c, _ = _select_slots(a_ref[:, SEQ:TOK], CAP_CTX)
    pos_ref[:, 0:SEQ] = pos_l
    pos_ref[:, SEQ:TOK] = pos_c
    tok = lax.broadcasted_iota(I32, (SEQ, LANES), 0)
    tile = lax.broadcasted_iota(I32, (SEQ, LANES), 1)
    earlier = jnp.where(tok < tile * ROW_TILE, 1.0, 0.0).astype(BF16)
    starts_ref[...] = _dot(sel_l.astype(BF16), earlier).astype(I32)


def _topk(aff_t):
    rows = BATCH * N_EXPERTS
    return pl.pallas_call(
        _topk_kernel,
        out_shape=[jax.ShapeDtypeStruct((rows, TOK), I32), jax.ShapeDtypeStruct((rows, LANES), I32)],
        compiler_params=pltpu.CompilerParams(vmem_limit_bytes=VMEM_LIMIT),
        name="topk",
    )(aff_t.reshape(rows, TOK))


def _slot_window(starts_ref, b, e, t):
    base = (b * N_EXPERTS + e) * STARTS_STRIDE + t
    s = starts_ref[base]
    n = starts_ref[base + 1] - s
    a = jnp.minimum(s - (s & (SLOT_ALIGN - 1)), CAP_LAT - WIN_SLOTS)
    return pl.multiple_of(a, SLOT_ALIGN), (s - a + n) <= WIN_SLOTS


def _all_windows(starts_ref, b, t):
    wins = [_slot_window(starts_ref, b, e, t) for e in range(N_EXPERTS)]
    fits = functools.reduce(jnp.logical_and, [ok for _, ok in wins])
    return [a for a, _ in wins], fits


def _moe_gather_kernel(starts_ref, pos_ref, h_ref, xl_ref, xc_ref):
    b = pl.program_id(0)
    step = pl.program_id(1)
    last = step == STEPS_PER_SAMPLE - 1
    wins = [_all_windows(starts_ref, b, step * TILES_PER_STEP + k) for k in range(TILES_PER_STEP)]
    lat_fit = functools.reduce(jnp.logical_and, [fits for _, fits in wins[:-1]])
    all_fit = jnp.logical_and(lat_fit, wins[-1][1])

    def tile_pos(k, e):
        return pos_ref[0, e:e + 1, _tile_rows(k)]

    def windowed(k):
        starts = wins[k][0]
        slot = lax.broadcasted_iota(I32, (WIN_SLOTS, ROW_TILE), 0)
        picks = [jnp.where(slot == tile_pos(k, e) - starts[e], 1.0, 0.0).astype(BF16)
                 for e in range(N_EXPERTS)]
        y = _dot(jnp.concatenate(picks, axis=0), h_ref[0, _tile_rows(k), :]).astype(BF16)
        for e in range(N_EXPERTS):
            xl_ref[0, e, pl.ds(starts[e], WIN_SLOTS), :] += y[e * WIN_SLOTS:(e + 1) * WIN_SLOTS]

    def full(k):
        slot = lax.broadcasted_iota(I32, (CAP_LAT, ROW_TILE), 0)
        for e in range(N_EXPERTS):
            pick = jnp.where(slot == tile_pos(k, e), 1.0, 0.0).astype(BF16)
            xl_ref[0, e] += _dot(pick, h_ref[0, _tile_rows(k), :]).astype(BF16)

    def context(k):
        slot = lax.broadcasted_iota(I32, (CAP_CTX, ROW_TILE), 0)
        picks = [jnp.where(slot == tile_pos(k, e), 1.0, 0.0).astype(BF16) for e in range(N_EXPERTS)]
        y = _dot(jnp.concatenate(picks, axis=0), h_ref[0, _tile_rows(k), :]).astype(BF16)
        for e in range(N_EXPERTS):
            xc_ref[0, e] = y[e * CAP_CTX:(e + 1) * CAP_CTX]

    @pl.when(step == 0)
    def _zero():
        xl_ref[...] = jnp.zeros(xl_ref.shape, BF16)

    @pl.when(jnp.logical_and(jnp.logical_not(last), all_fit))
    def _windowed_step():
        for k in range(TILES_PER_STEP):
            windowed(k)

    @pl.when(jnp.logical_and(jnp.logical_not(last), jnp.logical_not(all_fit)))
    def _full_step():
        for k in range(TILES_PER_STEP):
            full(k)

    @pl.when(jnp.logical_and(last, lat_fit))
    def _windowed_last_step():
        for k in range(TILES_PER_STEP - 1):
            windowed(k)
        context(TILES_PER_STEP - 1)

    @pl.when(jnp.logical_and(last, jnp.logical_not(lat_fit)))
    def _full_last_step():
        for k in range(TILES_PER_STEP - 1):
            full(k)
        context(TILES_PER_STEP - 1)


def _moe_gather(starts, pos, h2):
    return pl.pallas_call(
        _moe_gather_kernel,
        grid_spec=pltpu.PrefetchScalarGridSpec(
            num_scalar_prefetch=1,
            grid=(BATCH, STEPS_PER_SAMPLE),
            in_specs=[
                pl.BlockSpec((1, N_EXPERTS, STEP_ROWS), lambda b, t, s: (b, 0, t)),
                pl.BlockSpec((1, STEP_ROWS, D_MODEL), lambda b, t, s: (b, t, 0)),
            ],
            out_specs=[
                pl.BlockSpec((1, N_EXPERTS, CAP_LAT, D_MODEL), lambda b, t, s: (b, 0, 0, 0)),
                pl.BlockSpec((1, N_EXPERTS, CAP_CTX, D_MODEL), lambda b, t, s: (b, 0, 0, 0)),
            ],
        ),
        out_shape=[
            jax.ShapeDtypeStruct((BATCH, N_EXPERTS, CAP_LAT, D_MODEL), BF16),
            jax.ShapeDtypeStruct((BATCH, N_EXPERTS, CAP_CTX, D_MODEL), BF16),
        ],
        compiler_params=pltpu.CompilerParams(vmem_limit_bytes=VMEM_LIMIT),
        name="moe_gather",
    )(starts, pos, h2)


def _moe_ffn_kernel(xl_ref, xc_ref, wg_ref, wu_ref, wd_ref, yl_ref, yc_ref, wg_bf, wu_bf, wd_bf):
    @pl.when(pl.program_id(1) == 0)
    def _cast_expert_weights():
        wg_bf[...] = wg_ref[0].astype(BF16)
        wu_bf[...] = wu_ref[0].astype(BF16)
        wd_bf[...] = wd_ref[0].astype(BF16)

    xe = jnp.concatenate([r[i, 0] for i in range(FFN_SAMPLES) for r in (xl_ref, xc_ref)], axis=0)
    hid = (_silu(_dot(xe, wg_bf[...])) * _dot(xe, wu_bf[...])).astype(BF16)
    ye = _dot(hid, wd_bf[...]).astype(BF16)
    cap = CAP_LAT + CAP_CTX
    for i in range(FFN_SAMPLES):
        yl_ref[i, 0] = ye[i * cap:i * cap + CAP_LAT]
        yc_ref[i, 0] = ye[i * cap + CAP_LAT:(i + 1) * cap]


def _moe_ffn(xe_l, xe_c, layer, wg, wu, wd):
    wspec = pl.BlockSpec((None, 1, D_MODEL, D_MODEL), lambda e, b: (layer, e, 0, 0))
    lat = pl.BlockSpec((FFN_SAMPLES, 1, CAP_LAT, D_MODEL), lambda e, b: (b, e, 0, 0))
    ctx = pl.BlockSpec((FFN_SAMPLES, 1, CAP_CTX, D_MODEL), lambda e, b: (b, e, 0, 0))
    return pl.pallas_call(
        _moe_ffn_kernel,
        grid=(N_EXPERTS, BATCH // FFN_SAMPLES),
        in_specs=[lat, ctx, wspec, wspec, wspec],
        out_specs=[lat, ctx],
        out_shape=[
            jax.ShapeDtypeStruct((BATCH, N_EXPERTS, CAP_LAT, D_MODEL), BF16),
            jax.ShapeDtypeStruct((BATCH, N_EXPERTS, CAP_CTX, D_MODEL), BF16),
        ],
        scratch_shapes=[pltpu.VMEM((D_MODEL, D_MODEL), BF16)] * 3,
        compiler_params=pltpu.CompilerParams(vmem_limit_bytes=VMEM_LIMIT),
        name="moe_ffn",
    )(xe_l, xe_c, wg, wu, wd)


def _moe_combine_kernel(starts_ref, pos_ref, aff_ref, yl_ref, yc_ref, x_ref, mod_ref, modc_ref, o_ref,
                        g_ref, yw_ref, *, tiles, with_context):
    b = pl.program_id(0)
    step = pl.program_id(1)
    wins = [_all_windows(starts_ref, b, step * tiles + k) for k in range(tiles)]
    lat_fit = functools.reduce(jnp.logical_and, [fits for _, fits in wins[:-1]])
    all_fit = jnp.logical_and(lat_fit, wins[-1][1])

    def gates(g_ref, row0, cap, k, e, first_slot):
        slot = lax.broadcasted_iota(I32, (cap, ROW_TILE), 0)
        g_ref[row0 + e * cap:row0 + (e + 1) * cap, :] = jnp.where(
            slot == pos_ref[0, e:e + 1, _tile_rows(k)] - first_slot,
            aff_ref[0, e:e + 1, _tile_rows(k)], 0.0).astype(BF16)

    def finish(k, moe, m_ref):
        o_ref[0, _tile_rows(k), :] = x_ref[0, _tile_rows(k), :] + m_ref[5:6, :] * moe

    def windowed(k):
        starts = wins[k][0]
        base = k * N_EXPERTS * WIN_SLOTS
        for e in range(N_EXPERTS):
            gates(g_ref, base, WIN_SLOTS, k, e, starts[e])
            row0 = pl.multiple_of(e * CAP_LAT + starts[e], SLOT_ALIGN)
            yw_ref[base + e * WIN_SLOTS:base + (e + 1) * WIN_SLOTS, :] = yl_ref[0, pl.ds(row0, WIN_SLOTS), :]
        finish(k, _dot_tn(g_ref[base:base + N_EXPERTS * WIN_SLOTS, :],
                          yw_ref[base:base + N_EXPERTS * WIN_SLOTS, :]), mod_ref)

    def full(k):
        for e in range(N_EXPERTS):
            gates(g_ref, 0, CAP_LAT, k, e, 0)
        finish(k, _dot_tn(g_ref[...], yl_ref[0]), mod_ref)

    def context(k):
        base = N_EXPERTS * (CAP_LAT - CAP_CTX)
        for e in range(N_EXPERTS):
            gates(g_ref, base, CAP_CTX, k, e, 0)
        finish(k, _dot_tn(g_ref[base:base + N_EXPERTS * CAP_CTX, :], yc_ref[0]), modc_ref)

    last = step == pl.num_programs(1) - 1 if with_context else False
    not_last = jnp.logical_not(last) if with_context else True

    @pl.when(jnp.logical_and(not_last, all_fit))
    def _windowed_step():
        for k in range(tiles):
            windowed(k)

    @pl.when(jnp.logical_and(not_last, jnp.logical_not(all_fit)))
    def _full_step():
        for k in range(tiles):
            full(k)

    if with_context:
        @pl.when(jnp.logical_and(last, lat_fit))
        def _windowed_last_step():
            for k in range(tiles - 1):
                windowed(k)
            context(tiles - 1)

        @pl.when(jnp.logical_and(last, jnp.logical_not(lat_fit)))
        def _full_last_step():
            for k in range(tiles - 1):
                full(k)
            context(tiles - 1)


N_COMBINE_INPUTS = 8
N_COMBINE_SCRATCH = 2


def _combine_project_kernel(*refs, projection_body, n_proj_inputs):
    n_in = N_COMBINE_INPUTS + n_proj_inputs
    combine_in, proj_in = refs[:N_COMBINE_INPUTS], refs[N_COMBINE_INPUTS:n_in]
    x_out, proj_out = refs[n_in], refs[n_in + 1:len(refs) - N_COMBINE_SCRATCH]
    scratch = refs[len(refs) - N_COMBINE_SCRATCH:]
    _moe_combine_kernel(*combine_in, x_out, *scratch, tiles=TILES_PER_STEP, with_context=True)
    projection_body(x_out, *proj_in, *proj_out)


def _moe_combine(starts, pos, aff_t, ye_l, ye_c, x1, mod, layer, with_context, projection=None):
    tiles = TILES_PER_STEP if with_context else 2
    rows = TOK if with_context else SEQ
    step = pl.BlockSpec((1, tiles * ROW_TILE, D_MODEL), lambda b, t, s: (b, t, 0))
    sel = pl.BlockSpec((1, N_EXPERTS, tiles * ROW_TILE), lambda b, t, s: (b, 0, t))
    x_shape = jax.ShapeDtypeStruct((BATCH, rows, D_MODEL), F32)
    if projection is None:
        body = functools.partial(_moe_combine_kernel, tiles=tiles, with_context=with_context)
        extra_specs, extra_args, out_specs, out_shape = [], [], step, x_shape
    else:
        assert with_context and tiles * ROW_TILE == STEP_ROWS
        body = functools.partial(_combine_project_kernel, projection_body=projection.body,
                                 n_proj_inputs=len(projection.args))
        extra_specs, extra_args = projection.in_specs, projection.args
        out_specs, out_shape = [step] + projection.out_specs, [x_shape] + projection.out_shape
    return pl.pallas_call(
        body,
        grid_spec=pltpu.PrefetchScalarGridSpec(
            num_scalar_prefetch=1,
            grid=(BATCH, rows // (tiles * ROW_TILE)),
            in_specs=[
                sel, sel,
                pl.BlockSpec((1, N_EXPERTS * CAP_LAT, D_MODEL), lambda b, t, s: (b, 0, 0)),
                pl.BlockSpec((1, N_EXPERTS * CAP_CTX, D_MODEL), lambda b, t, s: (b, 0, 0)),
                step,
                *_mod_specs(layer),
            ] + extra_specs,
            out_specs=out_specs,
            scratch_shapes=[
                pltpu.VMEM((N_EXPERTS * CAP_LAT, ROW_TILE), BF16),
                pltpu.VMEM((tiles * N_EXPERTS * WIN_SLOTS, D_MODEL), BF16),
            ],
        ),
        out_shape=out_shape,
        compiler_params=pltpu.CompilerParams(vmem_limit_bytes=VMEM_LIMIT_FUSED),
        name="moe_combine" if projection is None else "moe_combine_" + projection.name,
    )(starts, pos, aff_t, ye_l.reshape(BATCH, N_EXPERTS * CAP_LAT, D_MODEL),
      ye_c.reshape(BATCH, N_EXPERTS * CAP_CTX, D_MODEL), x1, mod, mod, *extra_args)


def _rope_tables():
    rows = SEQ // GRID_W
    row = jnp.repeat(jnp.arange(rows, dtype=F32), GRID_W)
    col = jnp.tile(jnp.arange(GRID_W, dtype=F32), rows)
    axis_dim = HEAD_DIM // 2
    inv_freq = ROPE_THETA ** (-jnp.arange(0, axis_dim, 2, dtype=F32) / axis_dim)
    ang_r = row[:, None] * inv_freq[None, :]
    ang_c = col[:, None] * inv_freq[None, :]
    cos_h = jnp.concatenate([jnp.cos(ang_r), jnp.cos(ang_r), jnp.cos(ang_c), jnp.cos(ang_c)], axis=-1)
    sin_h = jnp.concatenate([-jnp.sin(ang_r), jnp.sin(ang_r), -jnp.sin(ang_c), jnp.sin(ang_c)], axis=-1)
    cos_t = jnp.concatenate([jnp.tile(cos_h, (1, 2)), jnp.ones((CTX_LEN, LANES), F32)], axis=0)
    sin_t = jnp.concatenate([jnp.tile(sin_h, (1, 2)), jnp.zeros((CTX_LEN, LANES), F32)], axis=0)
    return cos_t, sin_t


def kernel(x, c, ctx, c_ctx, ada_w, ada_b, norm1_g, norm2_g, attn_w_in, attn_w_out, qnorm_a, knorm_a,
           qnorm_b, knorm_b, sink_b, conv_w_in, conv_k, conv_b, conv_w_out, router_w, moe_w_gate,
           moe_w_up, moe_w_down):
    assert x.shape == (BATCH, SEQ, D_MODEL) and ctx.shape == (BATCH, CTX_LEN, D_MODEL)
    cos_t, sin_t = _rope_tables()
    mod_rows = 24
    c_all = jnp.concatenate([c, c_ctx[None, :], jnp.zeros((mod_rows - BATCH - 1, D_MODEL), F32)], axis=0)
    mod = _adaln(c_all, ada_w, ada_b).reshape(DEPTH, mod_rows, N_ADA, D_MODEL)
    def projection(i):
        j = i // 2
        g1 = norm1_g[i].reshape(1, D_MODEL)
        if i % 2 == 0:
            tile2 = lambda g: jnp.tile(g.reshape(1, HEAD_DIM), (1, 2))
            qk_g = jnp.concatenate([tile2(qnorm_a[j]), tile2(knorm_a[j]), tile2(qnorm_b[j]),
                                    tile2(knorm_b[j])], axis=0)
            return _attn_in(mod, i, g1, attn_w_in[j].astype(BF16), qk_g, cos_t, sin_t)
        return _conv_in(mod, i, g1, conv_w_in[j].astype(BF16))

    xs, first_ctx = x, ctx
    proj = projection(0)(x, ctx)
    for i in range(DEPTH):
        j = i // 2
        last = i == DEPTH - 1
        g2 = norm2_g[i].reshape(1, D_MODEL)
        rw_t = router_w[i].T
        rw_hi = rw_t.astype(BF16)
        rw_split = jnp.concatenate([rw_hi, (rw_t - rw_hi.astype(F32)).astype(BF16)], axis=0)
        if i % 2 == 0:
            q, kv = proj
            o = _attention(q, kv, _attention_params(qnorm_a[j], knorm_a[j], qnorm_b[j], knorm_b[j],
                                                    sink_b[j]), N_Q_TILES)
            x1, h2, aff_t = _mixer_out(xs, mod, i, attn_w_out[j].astype(BF16), g2, rw_split, attn_o=o,
                                       ctx=first_ctx if i == 0 else None)
        else:
            u, bg = proj
            x1, h2, aff_t = _mixer_out(xs, mod, i, conv_w_out[j].astype(BF16), g2, rw_split,
                                       conv=(u, bg, conv_k[j], conv_b[j].reshape(1, D_MODEL)))
        pos, starts = _topk(aff_t)
        pos = pos.reshape(BATCH, N_EXPERTS, TOK)
        starts = starts[:, :STARTS_STRIDE].reshape(-1)
        xe_l, xe_c = _moe_gather(starts, pos, h2)
        ye_l, ye_c = _moe_ffn(xe_l, xe_c, i, moe_w_gate, moe_w_up, moe_w_down)
        if last:
            return _moe_combine(starts, pos, aff_t, ye_l, ye_c, x1, mod, i, with_context=False)
        xs, *proj = _moe_combine(starts, pos, aff_t, ye_l, ye_c, x1, mod, i, with_context=True,
                                 projection=projection(i + 1))
```

```python
import functools

import jax
import jax.numpy as jnp
import numpy as np
from jax import lax
from jax.experimental import pallas as pl
from jax.experimental.pallas import tpu as pltpu

F32 = jnp.float32
BF16 = jnp.bfloat16
I32 = jnp.int32

D_MODEL = 1024
BATCH = 16
SEQ = 2048
CTX_LEN = 256
TOK = SEQ + CTX_LEN
DEPTH = 4
GRID_W = 64
HEAD_DIM = 64
N_Q_HEADS = 8
N_KV_HEADS = 2
Q_WIDTH = N_Q_HEADS * HEAD_DIM
KV_WIDTH = N_KV_HEADS * HEAD_DIM
ATT_IN_WIDTH = 2 * (Q_WIDTH + 2 * KV_WIDTH)
WINDOW = 128
ROPE_THETA = 10000.0
N_EXPERTS = 16
CAP_LAT = 2 * SEQ // N_EXPERTS
CAP_CTX = 2 * CTX_LEN // N_EXPERTS
N_ADA = 6
EPS = 1e-6
NEG_INF = -1e30

LANES = 128
ROW_TILE = 256
TILES_PER_STEP = 3
STEP_ROWS = TILES_PER_STEP * ROW_TILE
STEPS_PER_SAMPLE = (SEQ + CTX_LEN) // STEP_ROWS
N_ROW_TILES = TOK // ROW_TILE
N_LAT_ROW_TILES = SEQ // ROW_TILE
Q_TILE = 256
N_Q_TILES = TOK // Q_TILE
N_LAT_Q_TILES = SEQ // Q_TILE
WIN_Q_ROWS = 128
WIN_SPAN = WIN_Q_ROWS + 2 * WINDOW
HALO = 16
PREFIX_CHUNK = 256
SLOT_ALIGN = 16
WIN_SLOTS = 64
STARTS_STRIDE = 16
FFN_SAMPLES = 4
VMEM_LIMIT = 56 * 1024 * 1024
VMEM_LIMIT_FUSED = 62 * 1024 * 1024

LOG2E = 1.4426950408889634
PAR_SHIFT = 8
PAR_BOUNDED = 10
PAR_LEN = 16
MAX_SHIFT_SPAN = 100.0

HIGHEST = lax.Precision.HIGHEST


def _dot(a, b, precision=None):
    return jnp.dot(a, b, preferred_element_type=F32, precision=precision)


def _dot_nt(a, b, precision=None):
    return lax.dot_general(a, b, (((1,), (1,)), ((), ())), preferred_element_type=F32,
                           precision=precision)


def _dot_tn(a, b):
    return lax.dot_general(a, b, (((0,), (0,)), ((), ())), preferred_element_type=F32)


def _rmsnorm_rows(x, g):
    return x * lax.rsqrt(jnp.mean(x * x, axis=-1, keepdims=True) + EPS) * g


def _silu(x):
    return x * (1.0 / (1.0 + jnp.exp(-x)))


def _adaln_kernel(c_ref, w_ref, b_ref, o_ref):
    s = _silu(c_ref[...])
    w = w_ref[0]
    s_hi = s.astype(BF16)
    s_lo = (s - s_hi.astype(F32)).astype(BF16)
    w_hi = w.astype(BF16)
    w_lo = (w - w_hi.astype(F32)).astype(BF16)
    rows = s.shape[0]
    part = _dot(jnp.concatenate([s_hi, s_lo], axis=0), w_hi)
    o_ref[0] = part[0:rows] + part[rows:2 * rows] + _dot(s_hi, w_lo) + b_ref[0]


def _adaln(c_all, ada_w, ada_b):
    rows = c_all.shape[0]
    return pl.pallas_call(
        _adaln_kernel,
        grid=(DEPTH, N_ADA),
        in_specs=[
            pl.BlockSpec((rows, D_MODEL), lambda i, j: (0, 0)),
            pl.BlockSpec((1, D_MODEL, D_MODEL), lambda i, j: (i, 0, j)),
            pl.BlockSpec((1, 1, D_MODEL), lambda i, j: (i, 0, j)),
        ],
        out_specs=pl.BlockSpec((1, rows, D_MODEL), lambda i, j: (i, 0, j)),
        out_shape=jax.ShapeDtypeStruct((DEPTH, rows, N_ADA * D_MODEL), F32),
        name="adaln",
    )(c_all, ada_w, ada_b.reshape(DEPTH, 1, N_ADA * D_MODEL))


def _tile_mods(mod_ref, modc_ref):
    last_step = pl.program_id(1) == STEPS_PER_SAMPLE - 1
    mods = [mod_ref[...]] * (TILES_PER_STEP - 1)
    return mods + [jnp.where(last_step, modc_ref[...], mod_ref[...])]


def _tile_rows(k):
    return slice(k * ROW_TILE, (k + 1) * ROW_TILE)


def _modulated_norm(x_ref, g_ref, mods, shift_row, scale_row):
    hs = []
    for k, m in enumerate(mods):
        h = _rmsnorm_rows(x_ref[0, _tile_rows(k), :], g_ref[...])
        hs.append((h * (1.0 + m[scale_row:scale_row + 1, :]) + m[shift_row:shift_row + 1, :]).astype(BF16))
    return jnp.concatenate(hs, axis=0)


def _attn_in_kernel(x_ref, mod_ref, modc_ref, g_ref, w_ref, qkg_ref, cos_ref, sin_ref, q_ref, kv_ref):
    h = _modulated_norm(x_ref, g_ref, _tile_mods(mod_ref, modc_ref), 0, 1)
    p = _dot(h, w_ref[...])

    lane = lax.broadcasted_iota(I32, (STEP_ROWS, LANES), 1)
    r = lax.broadcasted_iota(I32, (LANES, LANES), 0)
    c = lax.broadcasted_iota(I32, (LANES, LANES), 1)
    seg_mean = jnp.where((r >> 6) == (c >> 6), 1.0 / HEAD_DIM, 0.0).astype(BF16)
    cos = cos_ref[...]
    sin = sin_ref[...]
    first_half = (lane & (HEAD_DIM // 2 - 1)) < (HEAD_DIM // 4)
    low = lane < HEAD_DIM

    def qk_norm_rope(xc, g):
        ms = _dot((xc * xc).astype(BF16), seg_mean)
        xn = xc * lax.rsqrt(ms + EPS) * g
        partner = jnp.where(first_half, pltpu.roll(xn, LANES - HEAD_DIM // 4, axis=1),
                            pltpu.roll(xn, HEAD_DIM // 4, axis=1))
        return xn * cos + partner * sin

    def put_kv(kind, base, vals, pad):
        rolled = pltpu.roll(vals, HEAD_DIM, axis=1)
        kv_ref[0, kind, 0, base + 0] = jnp.where(low, vals, pad).astype(BF16)
        kv_ref[0, kind, 0, base + 1] = jnp.where(low, pad, rolled).astype(BF16)
        kv_ref[0, kind, 1, base + 0] = jnp.where(low, rolled, pad).astype(BF16)
        kv_ref[0, kind, 1, base + 1] = jnp.where(low, pad, vals).astype(BF16)

    for kind in range(2):
        col0 = kind * (Q_WIDTH + 2 * KV_WIDTH)
        gq = qkg_ref[2 * kind:2 * kind + 1, :]
        gk = qkg_ref[2 * kind + 1:2 * kind + 2, :]
        for ci in range(Q_WIDTH // LANES):
            qc = qk_norm_rope(p[:, col0 + ci * LANES:col0 + (ci + 1) * LANES], gq)
            q_ref[0, :, kind * Q_WIDTH + ci * LANES:kind * Q_WIDTH + (ci + 1) * LANES] = (
                qc * (HEAD_DIM ** -0.5 * LOG2E)).astype(BF16)
        kc = qk_norm_rope(p[:, col0 + Q_WIDTH:col0 + Q_WIDTH + KV_WIDTH], gk)
        put_kv(kind, 0, kc, 0.0)
        put_kv(kind, 2, p[:, col0 + Q_WIDTH + KV_WIDTH:col0 + Q_WIDTH + 2 * KV_WIDTH], 1.0)


def _mod_specs(layer):
    return [pl.BlockSpec((None, None, N_ADA, D_MODEL), lambda b, t, *_: (layer, b, 0, 0)),
            pl.BlockSpec((None, None, N_ADA, D_MODEL), lambda b, t, *_: (layer, BATCH, 0, 0))]


def _step_spec(width=D_MODEL):
    return pl.BlockSpec((1, STEP_ROWS, width), lambda b, t, *_: (b, t, 0))


def _const_spec(shape):
    return pl.BlockSpec(shape, lambda b, t, *_: (0,) * len(shape), pipeline_mode=pl.Buffered(1))


class _SplitStream:
    def __init__(self, x_ref, ctx_ref):
        self.x_ref, self.ctx_ref = x_ref, ctx_ref

    def __getitem__(self, idx):
        _, rows, cols = idx
        assert (rows.stop - rows.start) == ROW_TILE and rows.start % ROW_TILE == 0
        if rows.start != (TILES_PER_STEP - 1) * ROW_TILE:
            return self.x_ref[0, rows, cols]
        last_step = pl.program_id(1) == STEPS_PER_SAMPLE - 1
        return jnp.where(last_step, self.ctx_ref[0, :, cols], self.x_ref[0, rows, cols])


def _split_stream_body(body):
    return lambda x_ref, ctx_ref, *refs: body(_SplitStream(x_ref, ctx_ref), *refs)


def _ctx_spec():
    return pl.BlockSpec((1, CTX_LEN, D_MODEL), lambda b, t, *_: (b, 0, 0))


class _Projection:
    def __init__(self, name, body, in_specs, args, out_specs, out_shape):
        self.name, self.body, self.in_specs, self.args = name, body, in_specs, args
        self.out_specs, self.out_shape = out_specs, out_shape

    def __call__(self, x, ctx):
        return pl.pallas_call(
            _split_stream_body(self.body),
            grid=(BATCH, STEPS_PER_SAMPLE),
            in_specs=[_step_spec(), _ctx_spec()] + self.in_specs,
            out_specs=self.out_specs,
            out_shape=self.out_shape,
            compiler_params=pltpu.CompilerParams(vmem_limit_bytes=VMEM_LIMIT),
            name=self.name,
        )(x, ctx, *self.args)


def _attn_in(mod, layer, norm_g, w_in, qk_g, cos_t, sin_t):
    return _Projection(
        "attn_in", _attn_in_kernel,
        in_specs=[
            *_mod_specs(layer),
            _const_spec((1, D_MODEL)),
            _const_spec((D_MODEL, ATT_IN_WIDTH)),
            _const_spec((4, LANES)),
            pl.BlockSpec((STEP_ROWS, LANES), lambda b, t, *_: (t, 0)),
            pl.BlockSpec((STEP_ROWS, LANES), lambda b, t, *_: (t, 0)),
        ],
        args=[mod, mod, norm_g, w_in, qk_g, cos_t, sin_t],
        out_specs=[
            _step_spec(2 * Q_WIDTH),
            pl.BlockSpec((1, 2, N_KV_HEADS, 4, STEP_ROWS, LANES), lambda b, t, *_: (b, 0, 0, 0, t, 0)),
        ],
        out_shape=[
            jax.ShapeDtypeStruct((BATCH, TOK, 2 * Q_WIDTH), BF16),
            jax.ShapeDtypeStruct((BATCH, 2, N_KV_HEADS, 4, TOK, LANES), BF16),
        ])


def _softmax_pv(parts, shift, sink=None):
    if shift is None:
        for s, _ in parts:
            mx = jnp.max(s, axis=-1, keepdims=True)
            shift = mx if shift is None else jnp.maximum(shift, mx)
        if sink is not None:
            shift = jnp.maximum(shift, sink)
    acc = None
    for s, v in parts:
        pv = _dot(jnp.exp2(s - shift).astype(BF16), v)
        acc = pv if acc is None else acc + pv
    return acc, (jnp.exp2(sink - shift) if sink is not None else None)


def _attn_body(par_ref, q_ref, kv_ref, o_ref, bounded):
    t = pl.program_id(1)

    def heads(kind, g, r0, n, parts_of, with_sink):
        c0 = kind * Q_WIDTH + g * 2 * LANES
        lhs = jnp.concatenate([q_ref[0, r0:r0 + n, c0:c0 + LANES],
                               q_ref[0, r0:r0 + n, c0 + LANES:c0 + 2 * LANES]], axis=0)
        first_pair = lax.broadcasted_iota(I32, (2 * n, 1), 0) < n
        outs = []
        for parity in range(2):
            sink = None
            if with_sink:
                sink = jnp.where(first_pair, par_ref[4 * g + parity], par_ref[4 * g + 2 + parity])
            acc, sink_term = _softmax_pv(parts_of(lhs, parity), par_ref[PAR_SHIFT + kind] if bounded
                                         else None, sink)
            denom = acc[:, (1 - parity) * HEAD_DIM:(1 - parity) * HEAD_DIM + 1]
            if sink_term is not None:
                denom = denom + sink_term
            outs.append(acc * (1.0 / denom))
        low = lax.broadcasted_iota(I32, (2 * n, LANES), 1) < HEAD_DIM
        o = jnp.where(low, outs[0], outs[1])
        o_ref[0, r0:r0 + n, c0:c0 + LANES] = o[:n].astype(BF16)
        o_ref[0, r0:r0 + n, c0 + LANES:c0 + 2 * LANES] = o[n:].astype(BF16)

    @pl.when(t < N_LAT_Q_TILES)
    def _latent():
        for g in range(N_KV_HEADS):
            heads(0, g, 0, Q_TILE, lambda lhs, parity: [(_dot_nt(lhs, kv_ref[0, 0, g, parity]),
                                                         kv_ref[0, 0, g, 2 + parity])], False)

        for sub in range(Q_TILE // WIN_Q_ROWS):
            q0 = t * Q_TILE + sub * WIN_Q_ROWS
            ks = pl.multiple_of(jnp.clip(q0 - WINDOW, 0, SEQ - WIN_SPAN), WIN_Q_ROWS)
            qi = lax.broadcasted_iota(I32, (2 * WIN_Q_ROWS, WIN_SPAN), 0) & (WIN_Q_ROWS - 1)
            kj = lax.broadcasted_iota(I32, (2 * WIN_Q_ROWS, WIN_SPAN), 1)
            band = jnp.abs(qi - kj + (q0 - ks)) <= WINDOW

            def window_parts(g):
                def parts_of(lhs, parity):
                    s_w = _dot_nt(lhs, kv_ref[0, 1, g, parity, pl.ds(ks, WIN_SPAN), :])
                    s_c = _dot_nt(lhs, kv_ref[0, 1, g, parity, SEQ:TOK, :])
                    return [(jnp.where(band, s_w, NEG_INF),
                             kv_ref[0, 1, g, 2 + parity, pl.ds(ks, WIN_SPAN), :]),
                            (s_c, kv_ref[0, 1, g, 2 + parity, SEQ:TOK, :])]
                return parts_of

            for g in range(N_KV_HEADS):
                heads(1, g, sub * WIN_Q_ROWS, WIN_Q_ROWS, window_parts(g), True)

    @pl.when(t >= N_LAT_Q_TILES)
    def _context():
        for kind in range(2):
            for g in range(N_KV_HEADS):
                heads(kind, g, 0, Q_TILE,
                      lambda lhs, parity: [(_dot_nt(lhs, kv_ref[0, kind, g, parity, SEQ:TOK, :]),
                                            kv_ref[0, kind, g, 2 + parity, SEQ:TOK, :])],
                      kind == 1)


def _attn_kernel(par_ref, q_ref, kv_ref, o_ref):
    bounded = par_ref[PAR_BOUNDED] > 0.5

    @pl.when(bounded)
    def _bounded():
        _attn_body(par_ref, q_ref, kv_ref, o_ref, True)

    @pl.when(jnp.logical_not(bounded))
    def _exact_max():
        _attn_body(par_ref, q_ref, kv_ref, o_ref, False)


def _attention_params(qg_a, kg_a, qg_b, kg_b, sink):
    def bound(qg, kg):
        return 1.02 * LOG2E * HEAD_DIM ** 0.5 * jnp.max(jnp.abs(qg)) * jnp.max(jnp.abs(kg))
    sink2 = sink * LOG2E
    shift_a = bound(qg_a, kg_a)
    shift_b = jnp.maximum(bound(qg_b, kg_b), jnp.max(sink2))
    ok = jnp.logical_and(2.0 * shift_a < MAX_SHIFT_SPAN, shift_b + bound(qg_b, kg_b) < MAX_SHIFT_SPAN)
    return jnp.concatenate([sink2, jnp.stack([shift_a, shift_b, ok.astype(F32)]),
                            jnp.zeros((PAR_LEN - PAR_BOUNDED - 1,), F32)])


def _attention(q, kv, params, n_q_tiles):
    return pl.pallas_call(
        _attn_kernel,
        grid=(BATCH, n_q_tiles),
        in_specs=[
            pl.BlockSpec(memory_space=pltpu.SMEM),
            pl.BlockSpec((1, Q_TILE, 2 * Q_WIDTH), lambda b, t: (b, t, 0)),
            pl.BlockSpec((1, 2, N_KV_HEADS, 4, TOK, LANES), lambda b, t: (b, 0, 0, 0, 0, 0)),
        ],
        out_specs=pl.BlockSpec((1, Q_TILE, 2 * Q_WIDTH), lambda b, t: (b, t, 0)),
        out_shape=jax.ShapeDtypeStruct((BATCH, TOK, 2 * Q_WIDTH), BF16),
        compiler_params=pltpu.CompilerParams(vmem_limit_bytes=VMEM_LIMIT),
        name="attention",
    )(params, q, kv)


def _conv_in_kernel(x_ref, mod_ref, modc_ref, g_ref, w_ref, u_ref, bg_ref):
    h = _modulated_norm(x_ref, g_ref, _tile_mods(mod_ref, modc_ref), 0, 1)
    bg_ref[0] = _dot(h, w_ref[:, 0:D_MODEL]).astype(BF16)
    cg = _dot(h, w_ref[:, D_MODEL:2 * D_MODEL])
    v = _dot(h, w_ref[:, 2 * D_MODEL:3 * D_MODEL])
    u_ref[0] = (cg * v).astype(BF16)


def _conv_in(mod, layer, norm_g, w_in):
    return _Projection(
        "conv_in", _conv_in_kernel,
        in_specs=[*_mod_specs(layer), _const_spec((1, D_MODEL)), _const_spec((D_MODEL, 3 * D_MODEL))],
        args=[mod, mod, norm_g, w_in],
        out_specs=[_step_spec(), _step_spec()],
        out_shape=[
            jax.ShapeDtypeStruct((BATCH, TOK, D_MODEL), BF16),
            jax.ShapeDtypeStruct((BATCH, TOK, D_MODEL), BF16),
        ])


def _tail(lhs, w_ref, x_ref, mod_ref, modc_ref, g_ref, rw_ref, x1_ref, h2_ref, aff_ref):
    h2_hi, h2_lo = [], []
    for k, m in enumerate(_tile_mods(mod_ref, modc_ref)):
        x1 = x_ref[0, _tile_rows(k), :] + m[2:3, :] * _dot(lhs[_tile_rows(k)], w_ref[...])
        x1_ref[0, _tile_rows(k), :] = x1
        h2 = _rmsnorm_rows(x1, g_ref[...])
        h2 = h2 * (1.0 + m[4:5, :]) + m[3:4, :]
        hi = h2.astype(BF16)
        h2_ref[0, _tile_rows(k), :] = hi
        h2_hi.append(hi)
        h2_lo.append((h2 - hi.astype(F32)).astype(BF16))
    rw = rw_ref[...]
    part = _dot_nt(rw, jnp.concatenate(h2_hi, axis=0))
    logits = (part[0:N_EXPERTS] + part[N_EXPERTS:2 * N_EXPERTS]
              + _dot_nt(rw[0:N_EXPERTS], jnp.concatenate(h2_lo, axis=0)))
    e = jnp.exp(logits - jnp.max(logits, axis=0, keepdims=True))
    aff_ref[0] = e / jnp.sum(e, axis=0, keepdims=True)


def _attn_out_kernel(o_ref, w_ref, x_ref, mod_ref, modc_ref, g_ref, rw_ref, x1_ref, h2_ref, aff_ref):
    _tail(o_ref[0], w_ref, x_ref, mod_ref, modc_ref, g_ref, rw_ref, x1_ref, h2_ref, aff_ref)


def _conv_out_kernel(u_ref, up_ref, un_ref, bg_ref, ck_ref, cb_ref, w_ref, x_ref, mod_ref, modc_ref,
                     g_ref, rw_ref, x1_ref, h2_ref, aff_ref):
    u = u_ref[0].astype(F32)
    row = lax.broadcasted_iota(I32, (STEP_ROWS, 1), 0)
    pos = row + pl.program_id(1) * STEP_ROWS
    seq_first = jnp.logical_or(pos == 0, pos == SEQ)
    seq_last = jnp.logical_or(pos == SEQ - 1, pos == TOK - 1)
    u_prev = jnp.where(row == 0, up_ref[0, HALO - 1:HALO, :].astype(F32), pltpu.roll(u, 1, axis=0))
    u_next = jnp.where(row == STEP_ROWS - 1, un_ref[0, 0:1, :].astype(F32),
                       pltpu.roll(u, STEP_ROWS - 1, axis=0))
    u_prev = jnp.where(seq_first, 0.0, u_prev)
    u_next = jnp.where(seq_last, 0.0, u_next)
    y = ck_ref[0:1, :] * u_prev + ck_ref[1:2, :] * u + ck_ref[2:3, :] * u_next + cb_ref[...]
    lhs = (bg_ref[0].astype(F32) * y).astype(BF16)
    _tail(lhs, w_ref, x_ref, mod_ref, modc_ref, g_ref, rw_ref, x1_ref, h2_ref, aff_ref)


def _attn_out_split_kernel(o_ref, w_ref, x_ref, ctx_ref, *refs):
    _attn_out_kernel(o_ref, w_ref, _SplitStream(x_ref, ctx_ref), *refs)


def _mixer_out(xs, mod, layer, w_out, norm_g, rw_split, attn_o=None, conv=None, ctx=None):
    step = pl.BlockSpec((1, STEP_ROWS, D_MODEL), lambda b, t: (b, t, 0))
    common_specs = [
        pl.BlockSpec((D_MODEL, D_MODEL), lambda b, t: (0, 0)),
        step,
        *([] if ctx is None else [_ctx_spec()]),
        *_mod_specs(layer),
        pl.BlockSpec((1, D_MODEL), lambda b, t: (0, 0)),
        pl.BlockSpec((2 * N_EXPERTS, D_MODEL), lambda b, t: (0, 0)),
    ]
    stream = [xs] if ctx is None else [xs, ctx]
    if conv is None:
        body = _attn_out_kernel if ctx is None else _attn_out_split_kernel
        specs, args = [step], [attn_o]
    else:
        assert ctx is None
        u, bg, ck, cb = conv
        halo_per_step = STEP_ROWS // HALO
        last_halo = TOK // HALO - 1
        body = _conv_out_kernel
        specs = [
            step,
            pl.BlockSpec((1, HALO, D_MODEL), lambda b, t: (b, jnp.maximum(t * halo_per_step - 1, 0), 0)),
            pl.BlockSpec((1, HALO, D_MODEL),
                         lambda b, t: (b, jnp.minimum((t + 1) * halo_per_step, last_halo), 0)),
            step,
            pl.BlockSpec((3, D_MODEL), lambda b, t: (0, 0)),
            pl.BlockSpec((1, D_MODEL), lambda b, t: (0, 0)),
        ]
        args = [u, u, u, bg, ck, cb]
    return pl.pallas_call(
        body,
        grid=(BATCH, STEPS_PER_SAMPLE),
        in_specs=specs + common_specs,
        out_specs=[
            step,
            step,
            pl.BlockSpec((1, N_EXPERTS, STEP_ROWS), lambda b, t: (b, 0, t)),
        ],
        out_shape=[
            jax.ShapeDtypeStruct((BATCH, TOK, D_MODEL), F32),
            jax.ShapeDtypeStruct((BATCH, TOK, D_MODEL), BF16),
            jax.ShapeDtypeStruct((BATCH, N_EXPERTS, TOK), F32),
        ],
        compiler_params=pltpu.CompilerParams(vmem_limit_bytes=VMEM_LIMIT),
        name="mixer_out",
    )(*args, w_out, *stream, mod, mod, norm_g, rw_split)


def _select_slots(v, k):
    rows, n = v.shape
    u = pltpu.bitcast(v, I32)

    def body(i, thr):
        cand = thr | jnp.left_shift(jnp.int32(1), 30 - i)
        cnt = jnp.sum(jnp.where(u >= cand, 1.0, 0.0), axis=1, keepdims=True)
        return jnp.where(cnt >= k, cand, thr)

    thr = lax.fori_loop(0, 31, body, jnp.zeros((rows, 1), I32))
    gt = u > thr
    eq = u == thr
    need = k - jnp.sum(jnp.where(gt, 1.0, 0.0), axis=1, keepdims=True)
    r = lax.broadcasted_iota(I32, (PREFIX_CHUNK, PREFIX_CHUNK), 0)
    c = lax.broadcasted_iota(I32, (PREFIX_CHUNK, PREFIX_CHUNK), 1)
    before = jnp.where(r < c, 1.0, 0.0).astype(BF16)

    def prefix_count(flags):
        out, carry = [], jnp.zeros((rows, 1), F32)
        for j in range(n // PREFIX_CHUNK):
            chunk = flags[:, j * PREFIX_CHUNK:(j + 1) * PREFIX_CHUNK]
            out.append(_dot(chunk.astype(BF16), before) + carry)
            carry = carry + jnp.sum(chunk, axis=1, keepdims=True)
        return jnp.concatenate(out, axis=1)

    eq_rank = prefix_count(jnp.where(eq, 1.0, 0.0))
    sel = jnp.where(gt, 1.0, jnp.where(eq, jnp.where(eq_rank < need, 1.0, 0.0), 0.0))
    slot = prefix_count(sel)
    return jnp.where(sel > 0.5, slot, -1.0).astype(I32), sel


def _topk_kernel(a_ref, pos_ref, starts_ref):
    pos_l, sel_l = _select_slots(a_ref[:, 0:SEQ], CAP_LAT)
    pos_c, _ = _select_slots(a_ref[:, SEQ:TOK], CAP_CTX)
    pos_ref[:, 0:SEQ] = pos_l
    pos_ref[:, SEQ:TOK] = pos_c
    tok = lax.broadcasted_iota(I32, (SEQ, LANES), 0)
    tile = lax.broadcasted_iota(I32, (SEQ, LANES), 1)
    earlier = jnp.where(tok < tile * ROW_TILE, 1.0, 0.0).astype(BF16)
    starts_ref[...] = _dot(sel_l.astype(BF16), earlier).astype(I32)


def _topk(aff_t):
    rows = BATCH * N_EXPERTS
    return pl.pallas_call(
        _topk_kernel,
        out_shape=[jax.ShapeDtypeStruct((rows, TOK), I32), jax.ShapeDtypeStruct((rows, LANES), I32)],
        compiler_params=pltpu.CompilerParams(vmem_limit_bytes=VMEM_LIMIT),
        name="topk",
    )(aff_t.reshape(rows, TOK))


def _slot_window(starts_ref, b, e, t):
    base = (b * N_EXPERTS + e) * STARTS_STRIDE + t
    s = starts_ref[base]
    n = starts_ref[base + 1] - s
    a = jnp.minimum(s - (s & (SLOT_ALIGN - 1)), CAP_LAT - WIN_SLOTS)
    return pl.multiple_of(a, SLOT_ALIGN), (s - a + n) <= WIN_SLOTS


def _all_windows(starts_ref, b, t):
    wins = [_slot_window(starts_ref, b, e, t) for e in range(N_EXPERTS)]
    fits = functools.reduce(jnp.logical_and, [ok for _, ok in wins])
    return [a for a, _ in wins], fits


def _moe_gather_kernel(starts_ref, pos_ref, h_ref, xl_ref, xc_ref):
    b = pl.program_id(0)
    step = pl.program_id(1)
    last = step == STEPS_PER_SAMPLE - 1
    wins = [_all_windows(starts_ref, b, step * TILES_PER_STEP + k) for k in range(TILES_PER_STEP)]
    lat_fit = functools.reduce(jnp.logical_and, [fits for _, fits in wins[:-1]])
    all_fit = jnp.logical_and(lat_fit, wins[-1][1])

    def tile_pos(k, e):
        return pos_ref[0, e:e + 1, _tile_rows(k)]

    def windowed(k):
        starts = wins[k][0]
        slot = lax.broadcasted_iota(I32, (WIN_SLOTS, ROW_TILE), 0)
        picks = [jnp.where(slot == tile_pos(k, e) - starts[e], 1.0, 0.0).astype(BF16)
                 for e in range(N_EXPERTS)]
        y = _dot(jnp.concatenate(picks, axis=0), h_ref[0, _tile_rows(k), :]).astype(BF16)
        for e in range(N_EXPERTS):
            xl_ref[0, e, pl.ds(starts[e], WIN_SLOTS), :] += y[e * WIN_SLOTS:(e + 1) * WIN_SLOTS]

    def full(k):
        slot = lax.broadcasted_iota(I32, (CAP_LAT, ROW_TILE), 0)
        for e in range(N_EXPERTS):
            pick = jnp.where(slot == tile_pos(k, e), 1.0, 0.0).astype(BF16)
            xl_ref[0, e] += _dot(pick, h_ref[0, _tile_rows(k), :]).astype(BF16)

    def context(k):
        slot = lax.broadcasted_iota(I32, (CAP_CTX, ROW_TILE), 0)
        picks = [jnp.where(slot == tile_pos(k, e), 1.0, 0.0).astype(BF16) for e in range(N_EXPERTS)]
        y = _dot(jnp.concatenate(picks, axis=0), h_ref[0, _tile_rows(k), :]).astype(BF16)
        for e in range(N_EXPERTS):
            xc_ref[0, e] = y[e * CAP_CTX:(e + 1) * CAP_CTX]

    @pl.when(step == 0)
    def _zero():
        xl_ref[...] = jnp.zeros(xl_ref.shape, BF16)

    @pl.when(jnp.logical_and(jnp.logical_not(last), all_fit))
    def _windowed_step():
        for k in range(TILES_PER_STEP):
            windowed(k)

    @pl.when(jnp.logical_and(jnp.logical_not(last), jnp.logical_not(all_fit)))
    def _full_step():
        for k in range(TILES_PER_STEP):
            full(k)

    @pl.when(jnp.logical_and(last, lat_fit))
    def _windowed_last_step():
        for k in range(TILES_PER_STEP - 1):
            windowed(k)
        context(TILES_PER_STEP - 1)

    @pl.when(jnp.logical_and(last, jnp.logical_not(lat_fit)))
    def _full_last_step():
        for k in range(TILES_PER_STEP - 1):
            full(k)
        context(TILES_PER_STEP - 1)


def _moe_gather(starts, pos, h2):
    return pl.pallas_call(
        _moe_gather_kernel,
        grid_spec=pltpu.PrefetchScalarGridSpec(
            num_scalar_prefetch=1,
            grid=(BATCH, STEPS_PER_SAMPLE),
            in_specs=[
                pl.BlockSpec((1, N_EXPERTS, STEP_ROWS), lambda b, t, s: (b, 0, t)),
                pl.BlockSpec((1, STEP_ROWS, D_MODEL), lambda b, t, s: (b, t, 0)),
            ],
            out_specs=[
                pl.BlockSpec((1, N_EXPERTS, CAP_LAT, D_MODEL), lambda b, t, s: (b, 0, 0, 0)),
                pl.BlockSpec((1, N_EXPERTS, CAP_CTX, D_MODEL), lambda b, t, s: (b, 0, 0, 0)),
            ],
        ),
        out_shape=[
            jax.ShapeDtypeStruct((BATCH, N_EXPERTS, CAP_LAT, D_MODEL), BF16),
            jax.ShapeDtypeStruct((BATCH, N_EXPERTS, CAP_CTX, D_MODEL), BF16),
        ],
        compiler_params=pltpu.CompilerParams(vmem_limit_bytes=VMEM_LIMIT),
        name="moe_gather",
    )(starts, pos, h2)


def _moe_ffn_kernel(xl_ref, xc_ref, wg_ref, wu_ref, wd_ref, yl_ref, yc_ref, wg_bf, wu_bf, wd_bf):
    @pl.when(pl.program_id(1) == 0)
    def _cast_expert_weights():
        wg_bf[...] = wg_ref[0].astype(BF16)
        wu_bf[...] = wu_ref[0].astype(BF16)
        wd_bf[...] = wd_ref[0].astype(BF16)

    xe = jnp.concatenate([r[i, 0] for i in range(FFN_SAMPLES) for r in (xl_ref, xc_ref)], axis=0)
    hid = (_silu(_dot(xe, wg_bf[...])) * _dot(xe, wu_bf[...])).astype(BF16)
    ye = _dot(hid, wd_bf[...]).astype(BF16)
    cap = CAP_LAT + CAP_CTX
    for i in range(FFN_SAMPLES):
        yl_ref[i, 0] = ye[i * cap:i * cap + CAP_LAT]
        yc_ref[i, 0] = ye[i * cap + CAP_LAT:(i + 1) * cap]


def _moe_ffn(xe_l, xe_c, layer, wg, wu, wd):
    wspec = pl.BlockSpec((None, 1, D_MODEL, D_MODEL), lambda e, b: (layer, e, 0, 0))
    lat = pl.BlockSpec((FFN_SAMPLES, 1, CAP_LAT, D_MODEL), lambda e, b: (b, e, 0, 0))
    ctx = pl.BlockSpec((FFN_SAMPLES, 1, CAP_CTX, D_MODEL), lambda e, b: (b, e, 0, 0))
    return pl.pallas_call(
        _moe_ffn_kernel,
        grid=(N_EXPERTS, BATCH // FFN_SAMPLES),
        in_specs=[lat, ctx, wspec, wspec, wspec],
        out_specs=[lat, ctx],
        out_shape=[
            jax.ShapeDtypeStruct((BATCH, N_EXPERTS, CAP_LAT, D_MODEL), BF16),
            jax.ShapeDtypeStruct((BATCH, N_EXPERTS, CAP_CTX, D_MODEL), BF16),
        ],
        scratch_shapes=[pltpu.VMEM((D_MODEL, D_MODEL), BF16)] * 3,
        compiler_params=pltpu.CompilerParams(vmem_limit_bytes=VMEM_LIMIT),
        name="moe_ffn",
    )(xe_l, xe_c, wg, wu, wd)


def _moe_combine_kernel(starts_ref, pos_ref, aff_ref, yl_ref, yc_ref, x_ref, mod_ref, modc_ref, o_ref,
                        g_ref, yw_ref, *, tiles, with_context):
    b = pl.program_id(0)
    step = pl.program_id(1)
    wins = [_all_windows(starts_ref, b, step * tiles + k) for k in range(tiles)]
    lat_fit = functools.reduce(jnp.logical_and, [fits for _, fits in wins[:-1]])
    all_fit = jnp.logical_and(lat_fit, wins[-1][1])

    def gates(g_ref, row0, cap, k, e, first_slot):
        slot = lax.broadcasted_iota(I32, (cap, ROW_TILE), 0)
        g_ref[row0 + e * cap:row0 + (e + 1) * cap, :] = jnp.where(
            slot == pos_ref[0, e:e + 1, _tile_rows(k)] - first_slot,
            aff_ref[0, e:e + 1, _tile_rows(k)], 0.0).astype(BF16)

    def finish(k, moe, m_ref):
        o_ref[0, _tile_rows(k), :] = x_ref[0, _tile_rows(k), :] + m_ref[5:6, :] * moe

    def windowed(k):
        starts = wins[k][0]
        base = k * N_EXPERTS * WIN_SLOTS
        for e in range(N_EXPERTS):
            gates(g_ref, base, WIN_SLOTS, k, e, starts[e])
            row0 = pl.multiple_of(e * CAP_LAT + starts[e], SLOT_ALIGN)
            yw_ref[base + e * WIN_SLOTS:base + (e + 1) * WIN_SLOTS, :] = yl_ref[0, pl.ds(row0, WIN_SLOTS), :]
        finish(k, _dot_tn(g_ref[base:base + N_EXPERTS * WIN_SLOTS, :],
                          yw_ref[base:base + N_EXPERTS * WIN_SLOTS, :]), mod_ref)

    def full(k):
        @pl.when(starts_ref[0] >= 0)
        def _one_tile():
            for e in range(N_EXPERTS):
                gates(g_ref, 0, CAP_LAT, k, e, 0)
            finish(k, _dot_tn(g_ref[...], yl_ref[0]), mod_ref)

    def context(k):
        base = N_EXPERTS * (CAP_LAT - CAP_CTX)
        for e in range(N_EXPERTS):
            gates(g_ref, base, CAP_CTX, k, e, 0)
        finish(k, _dot_tn(g_ref[base:base + N_EXPERTS * CAP_CTX, :], yc_ref[0]), modc_ref)

    last = step == pl.num_programs(1) - 1 if with_context else False
    not_last = jnp.logical_not(last) if with_context else True

    @pl.when(jnp.logical_and(not_last, all_fit))
    def _windowed_step():
        for k in range(tiles):
            windowed(k)

    @pl.when(jnp.logical_and(not_last, jnp.logical_not(all_fit)))
    def _full_step():
        for k in range(tiles):
            full(k)

    if with_context:
        @pl.when(jnp.logical_and(last, lat_fit))
        def _windowed_last_step():
            for k in range(tiles - 1):
                windowed(k)
            context(tiles - 1)

        @pl.when(jnp.logical_and(last, jnp.logical_not(lat_fit)))
        def _full_last_step():
            for k in range(tiles - 1):
                full(k)
            context(tiles - 1)


N_COMBINE_INPUTS = 8
N_COMBINE_SCRATCH = 2


def _combine_project_kernel(*refs, projection_body, n_proj_inputs):
    n_in = N_COMBINE_INPUTS + n_proj_inputs
    combine_in, proj_in = refs[:N_COMBINE_INPUTS], refs[N_COMBINE_INPUTS:n_in]
    x_out, proj_out = refs[n_in], refs[n_in + 1:len(refs) - N_COMBINE_SCRATCH]
    scratch = refs[len(refs) - N_COMBINE_SCRATCH:]
    _moe_combine_kernel(*combine_in, x_out, *scratch, tiles=TILES_PER_STEP, with_context=True)
    projection_body(x_out, *proj_in, *proj_out)


def _moe_combine(starts, pos, aff_t, ye_l, ye_c, x1, mod, layer, with_context, projection=None):
    tiles = TILES_PER_STEP if with_context else 2
    rows = TOK if with_context else SEQ
    step = pl.BlockSpec((1, tiles * ROW_TILE, D_MODEL), lambda b, t, s: (b, t, 0))
    sel = pl.BlockSpec((1, N_EXPERTS, tiles * ROW_TILE), lambda b, t, s: (b, 0, t))
    x_shape = jax.ShapeDtypeStruct((BATCH, rows, D_MODEL), F32)
    if projection is None:
        body = functools.partial(_moe_combine_kernel, tiles=tiles, with_context=with_context)
        extra_specs, extra_args, out_specs, out_shape = [], [], step, x_shape
    else:
        assert with_context and tiles * ROW_TILE == STEP_ROWS
        body = functools.partial(_combine_project_kernel, projection_body=projection.body,
                                 n_proj_inputs=len(projection.args))
        extra_specs, extra_args = projection.in_specs, projection.args
        out_specs, out_shape = [step] + projection.out_specs, [x_shape] + projection.out_shape
    return pl.pallas_call(
        body,
        grid_spec=pltpu.PrefetchScalarGridSpec(
            num_scalar_prefetch=1,
            grid=(BATCH, rows // (tiles * ROW_TILE)),
            in_specs=[
                sel, sel,
                pl.BlockSpec((1, N_EXPERTS * CAP_LAT, D_MODEL), lambda b, t, s: (b, 0, 0)),
                pl.BlockSpec((1, N_EXPERTS * CAP_CTX, D_MODEL), lambda b, t, s: (b, 0, 0)),
                step,
                *_mod_specs(layer),
            ] + extra_specs,
            out_specs=out_specs,
            scratch_shapes=[
                pltpu.VMEM((N_EXPERTS * CAP_LAT, ROW_TILE), BF16),
                pltpu.VMEM((tiles * N_EXPERTS * WIN_SLOTS, D_MODEL), BF16),
            ],
        ),
        out_shape=out_shape,
        compiler_params=pltpu.CompilerParams(vmem_limit_bytes=VMEM_LIMIT_FUSED),
        name="moe_combine" if projection is None else "moe_combine_" + projection.name,
    )(starts, pos, aff_t, ye_l.reshape(BATCH, N_EXPERTS * CAP_LAT, D_MODEL),
      ye_c.reshape(BATCH, N_EXPERTS * CAP_CTX, D_MODEL), x1, mod, mod, *extra_args)


def _rope_tables():
    rows = SEQ // GRID_W
    row = jnp.repeat(jnp.arange(rows, dtype=F32), GRID_W)
    col = jnp.tile(jnp.arange(GRID_W, dtype=F32), rows)
    axis_dim = HEAD_DIM // 2
    inv_freq = ROPE_THETA ** (-jnp.arange(0, axis_dim, 2, dtype=F32) / axis_dim)
    ang_r = row[:, None] * inv_freq[None, :]
    ang_c = col[:, None] * inv_freq[None, :]
    cos_h = jnp.concatenate([jnp.cos(ang_r), jnp.cos(ang_r), jnp.cos(ang_c), jnp.cos(ang_c)], axis=-1)
    sin_h = jnp.concatenate([-jnp.sin(ang_r), jnp.sin(ang_r), -jnp.sin(ang_c), jnp.sin(ang_c)], axis=-1)
    cos_t = jnp.concatenate([jnp.tile(cos_h, (1, 2)), jnp.ones((CTX_LEN, LANES), F32)], axis=0)
    sin_t = jnp.concatenate([jnp.tile(sin_h, (1, 2)), jnp.zeros((CTX_LEN, LANES), F32)], axis=0)
    return cos_t, sin_t


def kernel(x, c, ctx, c_ctx, ada_w, ada_b, norm1_g, norm2_g, attn_w_in, attn_w_out, qnorm_a, knorm_a,
           qnorm_b, knorm_b, sink_b, conv_w_in, conv_k, conv_b, conv_w_out, router_w, moe_w_gate,
           moe_w_up, moe_w_down):
    assert x.shape == (BATCH, SEQ, D_MODEL) and ctx.shape == (BATCH, CTX_LEN, D_MODEL)
    cos_t, sin_t = _rope_tables()
    mod_rows = 32
    c_all = jnp.concatenate([c, c_ctx[None, :], jnp.zeros((mod_rows - BATCH - 1, D_MODEL), F32)], axis=0)
    mod = _adaln(c_all, ada_w, ada_b).reshape(DEPTH, mod_rows, N_ADA, D_MODEL)
    def projection(i):
        j = i // 2
        g1 = norm1_g[i].reshape(1, D_MODEL)
        if i % 2 == 0:
            tile2 = lambda g: jnp.tile(g.reshape(1, HEAD_DIM), (1, 2))
            qk_g = jnp.concatenate([tile2(qnorm_a[j]), tile2(knorm_a[j]), tile2(qnorm_b[j]),
                                    tile2(knorm_b[j])], axis=0)
            return _attn_in(mod, i, g1, attn_w_in[j].astype(BF16), qk_g, cos_t, sin_t)
        return _conv_in(mod, i, g1, conv_w_in[j].astype(BF16))

    xs, first_ctx = x, ctx
    proj = projection(0)(x, ctx)
    for i in range(DEPTH):
        j = i // 2
        last = i == DEPTH - 1
        g2 = norm2_g[i].reshape(1, D_MODEL)
        rw_t = router_w[i].T
        rw_hi = rw_t.astype(BF16)
        rw_split = jnp.concatenate([rw_hi, (rw_t - rw_hi.astype(F32)).astype(BF16)], axis=0)
        if i % 2 == 0:
            q, kv = proj
            o = _attention(q, kv, _attention_params(qnorm_a[j], knorm_a[j], qnorm_b[j], knorm_b[j],
                                                    sink_b[j]), N_Q_TILES)
            x1, h2, aff_t = _mixer_out(xs, mod, i, attn_w_out[j].astype(BF16), g2, rw_split, attn_o=o,
                                       ctx=first_ctx if i == 0 else None)
        else:
            u, bg = proj
            x1, h2, aff_t = _mixer_out(xs, mod, i, conv_w_out[j].astype(BF16), g2, rw_split,
                                       conv=(u, bg, conv_k[j], conv_b[j].reshape(1, D_MODEL)))
        pos, starts = _topk(aff_t)
        pos = pos.reshape(BATCH, N_EXPERTS, TOK)
        starts = starts[:, :STARTS_STRIDE].reshape(-1)
        xe_l, xe_c = _moe_gather(starts, pos, h2)
        ye_l, ye_c = _moe_ffn(xe_l, xe_c, i, moe_w_gate, moe_w_up, moe_w_down)
        if last:
            return _moe_combine(starts, pos, aff_t, ye_l, ye_c, x1, mod, i, with_context=False)
        xs, *proj = _moe_combine(starts, pos, aff_t, ye_l, ye_c, x1, mod, i, with_context=True,
                                 projection=projection(i + 1))
```

```python
import functools

import jax
import jax.numpy as jnp
from jax import lax
from jax.experimental import pallas as pl
from jax.experimental.pallas import tpu as pltpu

F32 = jnp.float32
BF16 = jnp.bfloat16
I32 = jnp.int32

D_MODEL = 1024
BATCH = 16
SEQ = 2048
CTX_LEN = 256
TOK = SEQ + CTX_LEN
DEPTH = 4
GRID_W = 64
HEAD_DIM = 64
N_Q_HEADS = 8
N_KV_HEADS = 2
Q_WIDTH = N_Q_HEADS * HEAD_DIM
KV_WIDTH = N_KV_HEADS * HEAD_DIM
ATT_IN_WIDTH = 2 * (Q_WIDTH + 2 * KV_WIDTH)
WINDOW = 128
ROPE_THETA = 10000.0
N_EXPERTS = 16
CAP_LAT = 2 * SEQ // N_EXPERTS
CAP_CTX = 2 * CTX_LEN // N_EXPERTS
N_ADA = 6
EPS = 1e-6
NEG_INF = -1e30

LANES = 128
ROW_TILE = 256
TILES_PER_STEP = 3
STEP_ROWS = TILES_PER_STEP * ROW_TILE
STEPS_PER_SAMPLE = (SEQ + CTX_LEN) // STEP_ROWS
Q_TILE = 256
N_Q_TILES = TOK // Q_TILE
N_LAT_Q_TILES = SEQ // Q_TILE
WIN_Q_ROWS = 128
WIN_SPAN = WIN_Q_ROWS + 2 * WINDOW
HALO = 16
PREFIX_CHUNK = 256
SLOT_ALIGN = 16
WIN_SLOTS = 64
STARTS_STRIDE = 16
FFN_SAMPLES = 4
VMEM_LIMIT = 56 * 1024 * 1024
VMEM_LIMIT_FUSED = 62 * 1024 * 1024

LOG2E = 1.4426950408889634
PAR_BOUNDED = 8
PAR_LEN = 16
MAX_ABS_LOGIT = 50.0

def _dot(a, b):
    return jnp.dot(a, b, preferred_element_type=F32)


def _dot_nt(a, b):
    return lax.dot_general(a, b, (((1,), (1,)), ((), ())), preferred_element_type=F32)


def _dot_tn(a, b):
    return lax.dot_general(a, b, (((0,), (0,)), ((), ())), preferred_element_type=F32)


def _rmsnorm_rows(x, g):
    return x * lax.rsqrt(jnp.mean(x * x, axis=-1, keepdims=True) + EPS) * g


def _silu(x):
    return x * (1.0 / (1.0 + jnp.exp(-x)))


def _adaln_kernel(c_ref, w_ref, b_ref, o_ref):
    s = _silu(c_ref[...])
    w = w_ref[0]
    s_hi = s.astype(BF16)
    s_lo = (s - s_hi.astype(F32)).astype(BF16)
    w_hi = w.astype(BF16)
    w_lo = (w - w_hi.astype(F32)).astype(BF16)
    rows = s.shape[0]
    part = _dot(jnp.concatenate([s_hi, s_lo], axis=0), w_hi)
    o_ref[0] = part[0:rows] + part[rows:2 * rows] + _dot(s_hi, w_lo) + b_ref[0]


def _adaln(c_all, ada_w, ada_b):
    rows = c_all.shape[0]
    return pl.pallas_call(
        _adaln_kernel,
        grid=(DEPTH, N_ADA),
        in_specs=[
            pl.BlockSpec((rows, D_MODEL), lambda i, j: (0, 0)),
            pl.BlockSpec((1, D_MODEL, D_MODEL), lambda i, j: (i, 0, j)),
            pl.BlockSpec((1, 1, D_MODEL), lambda i, j: (i, 0, j)),
        ],
        out_specs=pl.BlockSpec((1, rows, D_MODEL), lambda i, j: (i, 0, j)),
        out_shape=jax.ShapeDtypeStruct((DEPTH, rows, N_ADA * D_MODEL), F32),
        name="adaln",
    )(c_all, ada_w, ada_b.reshape(DEPTH, 1, N_ADA * D_MODEL))


def _tile_mods(mod_ref, modc_ref):
    last_step = pl.program_id(1) == STEPS_PER_SAMPLE - 1
    mods = [mod_ref[...]] * (TILES_PER_STEP - 1)
    return mods + [jnp.where(last_step, modc_ref[...], mod_ref[...])]


def _tile_rows(k):
    return slice(k * ROW_TILE, (k + 1) * ROW_TILE)


def _modulated_norm(x_ref, g_ref, mods, shift_row, scale_row):
    hs = []
    for k, m in enumerate(mods):
        h = _rmsnorm_rows(x_ref[0, _tile_rows(k), :], g_ref[...])
        hs.append((h * (1.0 + m[scale_row:scale_row + 1, :]) + m[shift_row:shift_row + 1, :]).astype(BF16))
    return jnp.concatenate(hs, axis=0)


def _attn_in_kernel(x_ref, mod_ref, modc_ref, g_ref, w_ref, qkg_ref, cos_ref, sin_ref, q_ref, kv_ref):
    h = _modulated_norm(x_ref, g_ref, _tile_mods(mod_ref, modc_ref), 0, 1)
    p = _dot(h, w_ref[...])

    lane = lax.broadcasted_iota(I32, (STEP_ROWS, LANES), 1)
    r = lax.broadcasted_iota(I32, (LANES, LANES), 0)
    c = lax.broadcasted_iota(I32, (LANES, LANES), 1)
    seg_mean = jnp.where((r >> 6) == (c >> 6), 1.0 / HEAD_DIM, 0.0).astype(BF16)
    cos = cos_ref[...]
    sin = sin_ref[...]
    quarter = HEAD_DIM // 4
    swap_halves = jnp.where(r == c + jnp.where((c & (2 * quarter - 1)) < quarter, quarter, -quarter),
                            1.0, 0.0).astype(BF16)
    low = lane < HEAD_DIM

    def qk_norm_rope(xc, g):
        ms = _dot((xc * xc).astype(BF16), seg_mean)
        xn = xc * lax.rsqrt(ms + EPS) * g
        partner = _dot(xn.astype(BF16), swap_halves)
        return xn * cos + partner * sin

    def put_kv(kind, base, vals, pad):
        rolled = pltpu.roll(vals, HEAD_DIM, axis=1)
        kv_ref[0, kind, 0, base + 0] = jnp.where(low, vals, pad).astype(BF16)
        kv_ref[0, kind, 0, base + 1] = jnp.where(low, pad, rolled).astype(BF16)
        kv_ref[0, kind, 1, base + 0] = jnp.where(low, rolled, pad).astype(BF16)
        kv_ref[0, kind, 1, base + 1] = jnp.where(low, pad, vals).astype(BF16)

    for kind in range(2):
        col0 = kind * (Q_WIDTH + 2 * KV_WIDTH)
        gq = qkg_ref[2 * kind:2 * kind + 1, :]
        gk = qkg_ref[2 * kind + 1:2 * kind + 2, :]
        for ci in range(Q_WIDTH // LANES):
            qc = qk_norm_rope(p[:, col0 + ci * LANES:col0 + (ci + 1) * LANES], gq)
            q_ref[0, :, kind * Q_WIDTH + ci * LANES:kind * Q_WIDTH + (ci + 1) * LANES] = (
                qc * (HEAD_DIM ** -0.5 * LOG2E)).astype(BF16)
        kc = qk_norm_rope(p[:, col0 + Q_WIDTH:col0 + Q_WIDTH + KV_WIDTH], gk)
        put_kv(kind, 0, kc, 0.0)
        put_kv(kind, 2, p[:, col0 + Q_WIDTH + KV_WIDTH:col0 + Q_WIDTH + 2 * KV_WIDTH], 1.0)


def _mod_specs(layer):
    return [pl.BlockSpec((None, None, N_ADA, D_MODEL), lambda b, t, *_: (layer, b, 0, 0)),
            pl.BlockSpec((None, None, N_ADA, D_MODEL), lambda b, t, *_: (layer, BATCH, 0, 0))]


def _step_spec(width=D_MODEL):
    return pl.BlockSpec((1, STEP_ROWS, width), lambda b, t, *_: (b, t, 0))


def _const_spec(shape):
    return pl.BlockSpec(shape, lambda b, t, *_: (0,) * len(shape), pipeline_mode=pl.Buffered(1))


class _SplitStream:
    def __init__(self, x_ref, ctx_ref):
        self.x_ref, self.ctx_ref = x_ref, ctx_ref

    def __getitem__(self, idx):
        _, rows, cols = idx
        assert (rows.stop - rows.start) == ROW_TILE and rows.start % ROW_TILE == 0
        if rows.start != (TILES_PER_STEP - 1) * ROW_TILE:
            return self.x_ref[0, rows, cols]
        last_step = pl.program_id(1) == STEPS_PER_SAMPLE - 1
        return jnp.where(last_step, self.ctx_ref[0, :, cols], self.x_ref[0, rows, cols])


def _split_stream_body(body):
    return lambda x_ref, ctx_ref, *refs: body(_SplitStream(x_ref, ctx_ref), *refs)


def _ctx_spec():
    return pl.BlockSpec((1, CTX_LEN, D_MODEL), lambda b, t, *_: (b, 0, 0))


class _Projection:
    def __init__(self, name, body, in_specs, args, out_specs, out_shape):
        self.name, self.body, self.in_specs, self.args = name, body, in_specs, args
        self.out_specs, self.out_shape = out_specs, out_shape

    def __call__(self, x, ctx):
        return pl.pallas_call(
            _split_stream_body(self.body),
            grid=(BATCH, STEPS_PER_SAMPLE),
            in_specs=[_step_spec(), _ctx_spec()] + self.in_specs,
            out_specs=self.out_specs,
            out_shape=self.out_shape,
            compiler_params=pltpu.CompilerParams(vmem_limit_bytes=VMEM_LIMIT),
            name=self.name,
        )(x, ctx, *self.args)


def _attn_in(mod, layer, norm_g, w_in, qk_g, cos_t, sin_t):
    return _Projection(
        "attn_in", _attn_in_kernel,
        in_specs=[
            *_mod_specs(layer),
            _const_spec((1, D_MODEL)),
            _const_spec((D_MODEL, ATT_IN_WIDTH)),
            _const_spec((4, LANES)),
            pl.BlockSpec((STEP_ROWS, LANES), lambda b, t, *_: (t, 0)),
            pl.BlockSpec((STEP_ROWS, LANES), lambda b, t, *_: (t, 0)),
        ],
        args=[mod, mod, norm_g, w_in, qk_g, cos_t, sin_t],
        out_specs=[
            _step_spec(2 * Q_WIDTH),
            pl.BlockSpec((1, 2, N_KV_HEADS, 4, STEP_ROWS, LANES), lambda b, t, *_: (b, 0, 0, 0, t, 0)),
        ],
        out_shape=[
            jax.ShapeDtypeStruct((BATCH, TOK, 2 * Q_WIDTH), BF16),
            jax.ShapeDtypeStruct((BATCH, 2, N_KV_HEADS, 4, TOK, LANES), BF16),
        ])


def _softmax_pv(parts, bounded, sink=None):
    shift = None
    if not bounded:
        for s, _ in parts:
            mx = jnp.max(s, axis=-1, keepdims=True)
            shift = mx if shift is None else jnp.maximum(shift, mx)
        if sink is not None:
            shift = jnp.maximum(shift, sink)
    shifted = (lambda z: z) if bounded else (lambda z: z - shift)
    acc = None
    for s, v in parts:
        pv = _dot(jnp.exp2(shifted(s)).astype(BF16), v)
        acc = pv if acc is None else acc + pv
    return acc, (jnp.exp2(shifted(sink)) if sink is not None else None)


def _attn_body(par_ref, q_ref, kv_ref, o_ref, bounded):
    t = pl.program_id(1)

    def heads(kind, g, r0, n, parts_of, with_sink):
        c0 = kind * Q_WIDTH + g * 2 * LANES
        lhs = jnp.concatenate([q_ref[0, r0:r0 + n, c0:c0 + LANES],
                               q_ref[0, r0:r0 + n, c0 + LANES:c0 + 2 * LANES]], axis=0)
        first_pair = lax.broadcasted_iota(I32, (2 * n, 1), 0) < n
        outs = []
        for parity in range(2):
            sink = None
            if with_sink:
                sink = jnp.where(first_pair, par_ref[4 * g + parity], par_ref[4 * g + 2 + parity])
            acc, sink_term = _softmax_pv(parts_of(lhs, parity), bounded, sink)
            denom = acc[:, (1 - parity) * HEAD_DIM:(1 - parity) * HEAD_DIM + 1]
            if sink_term is not None:
                denom = denom + sink_term
            outs.append(acc * (1.0 / denom))
        low = lax.broadcasted_iota(I32, (2 * n, LANES), 1) < HEAD_DIM
        o = jnp.where(low, outs[0], outs[1])
        o_ref[0, r0:r0 + n, c0:c0 + LANES] = o[:n].astype(BF16)
        o_ref[0, r0:r0 + n, c0 + LANES:c0 + 2 * LANES] = o[n:].astype(BF16)

    @pl.when(t < N_LAT_Q_TILES)
    def _latent():
        for g in range(N_KV_HEADS):
            heads(0, g, 0, Q_TILE, lambda lhs, parity: [(_dot_nt(lhs, kv_ref[0, 0, g, parity]),
                                                         kv_ref[0, 0, g, 2 + parity])], False)

        for sub in range(Q_TILE // WIN_Q_ROWS):
            q0 = t * Q_TILE + sub * WIN_Q_ROWS
            ks = pl.multiple_of(jnp.clip(q0 - WINDOW, 0, SEQ - WIN_SPAN), WINDOW)
            qi = lax.broadcasted_iota(I32, (2 * WIN_Q_ROWS, WIN_SPAN), 0) & (WIN_Q_ROWS - 1)
            kj = lax.broadcasted_iota(I32, (2 * WIN_Q_ROWS, WIN_SPAN), 1)
            band = jnp.abs(qi - kj + (q0 - ks)) <= WINDOW

            def window_parts(g):
                def parts_of(lhs, parity):
                    s_w = _dot_nt(lhs, kv_ref[0, 1, g, parity, pl.ds(ks, WIN_SPAN), :])
                    s_c = _dot_nt(lhs, kv_ref[0, 1, g, parity, SEQ:TOK, :])
                    return [(jnp.where(band, s_w, NEG_INF),
                             kv_ref[0, 1, g, 2 + parity, pl.ds(ks, WIN_SPAN), :]),
                            (s_c, kv_ref[0, 1, g, 2 + parity, SEQ:TOK, :])]
                return parts_of

            for g in range(N_KV_HEADS):
                heads(1, g, sub * WIN_Q_ROWS, WIN_Q_ROWS, window_parts(g), True)

    @pl.when(t >= N_LAT_Q_TILES)
    def _context():
        for kind in range(2):
            for g in range(N_KV_HEADS):
                heads(kind, g, 0, Q_TILE,
                      lambda lhs, parity: [(_dot_nt(lhs, kv_ref[0, kind, g, parity, SEQ:TOK, :]),
                                            kv_ref[0, kind, g, 2 + parity, SEQ:TOK, :])],
                      kind == 1)


def _attn_kernel(par_ref, q_ref, kv_ref, o_ref):
    bounded = par_ref[PAR_BOUNDED] > 0.5

    @pl.when(bounded)
    def _bounded():
        _attn_body(par_ref, q_ref, kv_ref, o_ref, True)

    @pl.when(jnp.logical_not(bounded))
    def _exact_max():
        _attn_body(par_ref, q_ref, kv_ref, o_ref, False)


def _attention_params(qg_a, kg_a, qg_b, kg_b, sink):
    def bound(qg, kg):
        return 1.02 * LOG2E * HEAD_DIM ** 0.5 * jnp.max(jnp.abs(qg)) * jnp.max(jnp.abs(kg))
    sink2 = sink * LOG2E
    largest = jnp.maximum(jnp.maximum(bound(qg_a, kg_a), bound(qg_b, kg_b)), jnp.max(jnp.abs(sink2)))
    return jnp.concatenate([sink2, (largest < MAX_ABS_LOGIT).astype(F32)[None],
                            jnp.zeros((PAR_LEN - PAR_BOUNDED - 1,), F32)])


def _attention(q, kv, params, n_q_tiles):
    return pl.pallas_call(
        _attn_kernel,
        grid=(BATCH, n_q_tiles),
        in_specs=[
            pl.BlockSpec(memory_space=pltpu.SMEM),
            pl.BlockSpec((1, Q_TILE, 2 * Q_WIDTH), lambda b, t: (b, t, 0)),
            pl.BlockSpec((1, 2, N_KV_HEADS, 4, TOK, LANES), lambda b, t: (b, 0, 0, 0, 0, 0)),
        ],
        out_specs=pl.BlockSpec((1, Q_TILE, 2 * Q_WIDTH), lambda b, t: (b, t, 0)),
        out_shape=jax.ShapeDtypeStruct((BATCH, TOK, 2 * Q_WIDTH), BF16),
        compiler_params=pltpu.CompilerParams(vmem_limit_bytes=VMEM_LIMIT),
        name="attention",
    )(params, q, kv)


def _conv_in_kernel(x_ref, mod_ref, modc_ref, g_ref, w_ref, u_ref, bg_ref):
    h = _modulated_norm(x_ref, g_ref, _tile_mods(mod_ref, modc_ref), 0, 1)
    bg_ref[0] = _dot(h, w_ref[:, 0:D_MODEL]).astype(BF16)
    cg = _dot(h, w_ref[:, D_MODEL:2 * D_MODEL])
    v = _dot(h, w_ref[:, 2 * D_MODEL:3 * D_MODEL])
    u_ref[0] = (cg * v).astype(BF16)


def _conv_in(mod, layer, norm_g, w_in):
    return _Projection(
        "conv_in", _conv_in_kernel,
        in_specs=[*_mod_specs(layer), _const_spec((1, D_MODEL)), _const_spec((D_MODEL, 3 * D_MODEL))],
        args=[mod, mod, norm_g, w_in],
        out_specs=[_step_spec(), _step_spec()],
        out_shape=[
            jax.ShapeDtypeStruct((BATCH, TOK, D_MODEL), BF16),
            jax.ShapeDtypeStruct((BATCH, TOK, D_MODEL), BF16),
        ])


def _tail(lhs, w_ref, x_ref, mod_ref, modc_ref, g_ref, rw_ref, x1_ref, h2_ref, aff_ref):
    h2_hi, h2_lo = [], []
    for k, m in enumerate(_tile_mods(mod_ref, modc_ref)):
        x1 = x_ref[0, _tile_rows(k), :] + m[2:3, :] * _dot(lhs[_tile_rows(k)], w_ref[...])
        x1_ref[0, _tile_rows(k), :] = x1
        h2 = _rmsnorm_rows(x1, g_ref[...])
        h2 = h2 * (1.0 + m[4:5, :]) + m[3:4, :]
        hi = h2.astype(BF16)
        h2_ref[0, _tile_rows(k), :] = hi
        h2_hi.append(hi)
        h2_lo.append((h2 - hi.astype(F32)).astype(BF16))
    rw = rw_ref[...]
    part = _dot_nt(rw, jnp.concatenate(h2_hi, axis=0))
    logits = (part[0:N_EXPERTS] + part[N_EXPERTS:2 * N_EXPERTS]
              + _dot_nt(rw[0:N_EXPERTS], jnp.concatenate(h2_lo, axis=0)))
    e = jnp.exp(logits - jnp.max(logits, axis=0, keepdims=True))
    aff_ref[0] = e / jnp.sum(e, axis=0, keepdims=True)


def _attn_out_kernel(o_ref, w_ref, x_ref, mod_ref, modc_ref, g_ref, rw_ref, x1_ref, h2_ref, aff_ref):
    _tail(o_ref[0], w_ref, x_ref, mod_ref, modc_ref, g_ref, rw_ref, x1_ref, h2_ref, aff_ref)


def _conv_out_kernel(u_ref, up_ref, un_ref, bg_ref, ck_ref, cb_ref, w_ref, x_ref, mod_ref, modc_ref,
                     g_ref, rw_ref, x1_ref, h2_ref, aff_ref):
    u = u_ref[0].astype(F32)
    row = lax.broadcasted_iota(I32, (STEP_ROWS, 1), 0)
    pos = row + pl.program_id(1) * STEP_ROWS
    seq_first = jnp.logical_or(pos == 0, pos == SEQ)
    seq_last = jnp.logical_or(pos == SEQ - 1, pos == TOK - 1)
    u_prev = jnp.where(row == 0, up_ref[0, HALO - 1:HALO, :].astype(F32), pltpu.roll(u, 1, axis=0))
    u_next = jnp.where(row == STEP_ROWS - 1, un_ref[0, 0:1, :].astype(F32),
                       pltpu.roll(u, STEP_ROWS - 1, axis=0))
    u_prev = jnp.where(seq_first, 0.0, u_prev)
    u_next = jnp.where(seq_last, 0.0, u_next)
    y = ck_ref[0:1, :] * u_prev + ck_ref[1:2, :] * u + ck_ref[2:3, :] * u_next + cb_ref[...]
    lhs = (bg_ref[0].astype(F32) * y).astype(BF16)
    _tail(lhs, w_ref, x_ref, mod_ref, modc_ref, g_ref, rw_ref, x1_ref, h2_ref, aff_ref)


def _attn_out_split_kernel(o_ref, w_ref, x_ref, ctx_ref, *refs):
    _attn_out_kernel(o_ref, w_ref, _SplitStream(x_ref, ctx_ref), *refs)


def _mixer_out(xs, mod, layer, w_out, norm_g, rw_split, attn_o=None, conv=None, ctx=None):
    step = pl.BlockSpec((1, STEP_ROWS, D_MODEL), lambda b, t: (b, t, 0))
    common_specs = [
        pl.BlockSpec((D_MODEL, D_MODEL), lambda b, t: (0, 0)),
        step,
        *([] if ctx is None else [_ctx_spec()]),
        *_mod_specs(layer),
        pl.BlockSpec((1, D_MODEL), lambda b, t: (0, 0)),
        pl.BlockSpec((2 * N_EXPERTS, D_MODEL), lambda b, t: (0, 0)),
    ]
    stream = [xs] if ctx is None else [xs, ctx]
    if conv is None:
        body = _attn_out_kernel if ctx is None else _attn_out_split_kernel
        specs, args = [step], [attn_o]
    else:
        assert ctx is None
        u, bg, ck, cb = conv
        halo_per_step = STEP_ROWS // HALO
        last_halo = TOK // HALO - 1
        body = _conv_out_kernel
        specs = [
            step,
            pl.BlockSpec((1, HALO, D_MODEL), lambda b, t: (b, jnp.maximum(t * halo_per_step - 1, 0), 0)),
            pl.BlockSpec((1, HALO, D_MODEL),
                         lambda b, t: (b, jnp.minimum((t + 1) * halo_per_step, last_halo), 0)),
            step,
            pl.BlockSpec((3, D_MODEL), lambda b, t: (0, 0)),
            pl.BlockSpec((1, D_MODEL), lambda b, t: (0, 0)),
        ]
        args = [u, u, u, bg, ck, cb]
    return pl.pallas_call(
        body,
        grid=(BATCH, STEPS_PER_SAMPLE),
        in_specs=specs + common_specs,
        out_specs=[
            step,
            step,
            pl.BlockSpec((1, N_EXPERTS, STEP_ROWS), lambda b, t: (b, 0, t)),
        ],
        out_shape=[
            jax.ShapeDtypeStruct((BATCH, TOK, D_MODEL), F32),
            jax.ShapeDtypeStruct((BATCH, TOK, D_MODEL), BF16),
            jax.ShapeDtypeStruct((BATCH, N_EXPERTS, TOK), F32),
        ],
        compiler_params=pltpu.CompilerParams(vmem_limit_bytes=VMEM_LIMIT),
        name="mixer_out",
    )(*args, w_out, *stream, mod, mod, norm_g, rw_split)


def _select_slots(v, k):
    rows, n = v.shape
    u = pltpu.bitcast(v, I32)

    def body(i, thr):
        cand = thr | jnp.left_shift(jnp.int32(1), 30 - i)
        cnt = jnp.sum(jnp.where(u >= cand, 1.0, 0.0), axis=1, keepdims=True)
        return jnp.where(cnt >= k, cand, thr)

    thr = lax.fori_loop(0, 31, body, jnp.zeros((rows, 1), I32))
    gt = u > thr
    eq = u == thr
    need = k - jnp.sum(jnp.where(gt, 1.0, 0.0), axis=1, keepdims=True)
    r = lax.broadcasted_iota(I32, (PREFIX_CHUNK, PREFIX_CHUNK), 0)
    c = lax.broadcasted_iota(I32, (PREFIX_CHUNK, PREFIX_CHUNK), 1)
    before = jnp.where(r < c, 1.0, 0.0).astype(BF16)

    def prefix_count(flags):
        out, carry = [], jnp.zeros((rows, 1), F32)
        for j in range(n // PREFIX_CHUNK):
            chunk = flags[:, j * PREFIX_CHUNK:(j + 1) * PREFIX_CHUNK]
            out.append(_dot(chunk.astype(BF16), before) + carry)
            carry = carry + jnp.sum(chunk, axis=1, keepdims=True)
        return jnp.concatenate(out, axis=1)

    eq_rank = prefix_count(jnp.where(eq, 1.0, 0.0))
    sel = jnp.where(gt, 1.0, jnp.where(eq, jnp.where(eq_rank < need, 1.0, 0.0), 0.0))
    slot = prefix_count(sel)
    return jnp.where(sel > 0.5, slot, -1.0).astype(I32), sel


def _topk_kernel(a_ref, pos_ref, starts_ref):
    pos_l, sel_l = _select_slots(a_ref[:, 0:SEQ], CAP_LAT)
    pos_c, _ = _select_slots(a_ref[:, SEQ:TOK], CAP_CTX)
    pos_ref[:, 0:SEQ] = pos_l
    pos_ref[:, SEQ:TOK] = pos_c
    tok = lax.broadcasted_iota(I32, (SEQ, LANES), 0)
    tile = lax.broadcasted_iota(I32, (SEQ, LANES), 1)
    earlier = jnp.where(tok < tile * ROW_TILE, 1.0, 0.0).astype(BF16)
    starts_ref[...] = _dot(sel_l.astype(BF16), earlier).astype(I32)


def _topk(aff_t):
    rows = BATCH * N_EXPERTS
    return pl.pallas_call(
        _topk_kernel,
        out_shape=[jax.ShapeDtypeStruct((rows, TOK), I32), jax.ShapeDtypeStruct((rows, LANES), I32)],
        compiler_params=pltpu.CompilerParams(vmem_limit_bytes=VMEM_LIMIT),
        name="topk",
    )(aff_t.reshape(rows, TOK))


def _slot_window(starts_ref, b, e, t):
    base = (b * N_EXPERTS + e) * STARTS_STRIDE + t
    s = starts_ref[base]
    n = starts_ref[base + 1] - s
    a = jnp.minimum(s - (s & (SLOT_ALIGN - 1)), CAP_LAT - WIN_SLOTS)
    return pl.multiple_of(a, SLOT_ALIGN), (s - a + n) <= WIN_SLOTS


def _all_windows(starts_ref, b, t):
    wins = [_slot_window(starts_ref, b, e, t) for e in range(N_EXPERTS)]
    fits = functools.reduce(jnp.logical_and, [ok for _, ok in wins])
    return [a for a, _ in wins], fits


def _moe_gather_kernel(starts_ref, pos_ref, h_ref, xl_ref, xc_ref):
    b = pl.program_id(0)
    step = pl.program_id(1)
    last = step == STEPS_PER_SAMPLE - 1
    wins = [_all_windows(starts_ref, b, step * TILES_PER_STEP + k) for k in range(TILES_PER_STEP)]
    lat_fit = functools.reduce(jnp.logical_and, [fits for _, fits in wins[:-1]])
    all_fit = jnp.logical_and(lat_fit, wins[-1][1])

    def tile_pos(k, e):
        return pos_ref[0, e:e + 1, _tile_rows(k)]

    def windowed(k):
        starts = wins[k][0]
        slot = lax.broadcasted_iota(I32, (WIN_SLOTS, ROW_TILE), 0)
        picks = [jnp.where(slot == tile_pos(k, e) - starts[e], 1.0, 0.0).astype(BF16)
                 for e in range(N_EXPERTS)]
        y = _dot(jnp.concatenate(picks, axis=0), h_ref[0, _tile_rows(k), :]).astype(BF16)
        for e in range(N_EXPERTS):
            xl_ref[0, e, pl.ds(starts[e], WIN_SLOTS), :] += y[e * WIN_SLOTS:(e + 1) * WIN_SLOTS]

    def full(k):
        slot = lax.broadcasted_iota(I32, (CAP_LAT, ROW_TILE), 0)
        for e in range(N_EXPERTS):
            pick = jnp.where(slot == tile_pos(k, e), 1.0, 0.0).astype(BF16)
            xl_ref[0, e] += _dot(pick, h_ref[0, _tile_rows(k), :]).astype(BF16)

    def context(k):
        slot = lax.broadcasted_iota(I32, (CAP_CTX, ROW_TILE), 0)
        picks = [jnp.where(slot == tile_pos(k, e), 1.0, 0.0).astype(BF16) for e in range(N_EXPERTS)]
        y = _dot(jnp.concatenate(picks, axis=0), h_ref[0, _tile_rows(k), :]).astype(BF16)
        for e in range(N_EXPERTS):
            xc_ref[0, e] = y[e * CAP_CTX:(e + 1) * CAP_CTX]

    @pl.when(step == 0)
    def _zero():
        xl_ref[...] = jnp.zeros(xl_ref.shape, BF16)

    @pl.when(jnp.logical_and(jnp.logical_not(last), all_fit))
    def _windowed_step():
        for k in range(TILES_PER_STEP):
            windowed(k)

    @pl.when(jnp.logical_and(jnp.logical_not(last), jnp.logical_not(all_fit)))
    def _full_step():
        for k in range(TILES_PER_STEP):
            full(k)

    @pl.when(jnp.logical_and(last, lat_fit))
    def _windowed_last_step():
        for k in range(TILES_PER_STEP - 1):
            windowed(k)
        context(TILES_PER_STEP - 1)

    @pl.when(jnp.logical_and(last, jnp.logical_not(lat_fit)))
    def _full_last_step():
        for k in range(TILES_PER_STEP - 1):
            full(k)
        context(TILES_PER_STEP - 1)


def _moe_gather(starts, pos, h2):
    return pl.pallas_call(
        _moe_gather_kernel,
        grid_spec=pltpu.PrefetchScalarGridSpec(
            num_scalar_prefetch=1,
            grid=(BATCH, STEPS_PER_SAMPLE),
            in_specs=[
                pl.BlockSpec((1, N_EXPERTS, STEP_ROWS), lambda b, t, s: (b, 0, t)),
                pl.BlockSpec((1, STEP_ROWS, D_MODEL), lambda b, t, s: (b, t, 0)),
            ],
            out_specs=[
                pl.BlockSpec((1, N_EXPERTS, CAP_LAT, D_MODEL), lambda b, t, s: (b, 0, 0, 0)),
                pl.BlockSpec((1, N_EXPERTS, CAP_CTX, D_MODEL), lambda b, t, s: (b, 0, 0, 0)),
            ],
        ),
        out_shape=[
            jax.ShapeDtypeStruct((BATCH, N_EXPERTS, CAP_LAT, D_MODEL), BF16),
            jax.ShapeDtypeStruct((BATCH, N_EXPERTS, CAP_CTX, D_MODEL), BF16),
        ],
        compiler_params=pltpu.CompilerParams(vmem_limit_bytes=VMEM_LIMIT),
        name="moe_gather",
    )(starts, pos, h2)


def _moe_ffn_kernel(xl_ref, xc_ref, wg_ref, wu_ref, wd_ref, yl_ref, yc_ref, wg_bf, wu_bf, wd_bf):
    @pl.when(pl.program_id(1) == 0)
    def _cast_expert_weights():
        wg_bf[...] = wg_ref[0].astype(BF16)
        wu_bf[...] = wu_ref[0].astype(BF16)
        wd_bf[...] = wd_ref[0].astype(BF16)

    xe = jnp.concatenate([r[i, 0] for i in range(FFN_SAMPLES) for r in (xl_ref, xc_ref)], axis=0)
    hid = (_silu(_dot(xe, wg_bf[...])) * _dot(xe, wu_bf[...])).astype(BF16)
    ye = _dot(hid, wd_bf[...]).astype(BF16)
    cap = CAP_LAT + CAP_CTX
    for i in range(FFN_SAMPLES):
        yl_ref[i, 0] = ye[i * cap:i * cap + CAP_LAT]
        yc_ref[i, 0] = ye[i * cap + CAP_LAT:(i + 1) * cap]


def _moe_ffn(xe_l, xe_c, layer, wg, wu, wd):
    wspec = pl.BlockSpec((None, 1, D_MODEL, D_MODEL), lambda e, b: (layer, e, 0, 0))
    lat = pl.BlockSpec((FFN_SAMPLES, 1, CAP_LAT, D_MODEL), lambda e, b: (b, e, 0, 0))
    ctx = pl.BlockSpec((FFN_SAMPLES, 1, CAP_CTX, D_MODEL), lambda e, b: (b, e, 0, 0))
    return pl.pallas_call(
        _moe_ffn_kernel,
        grid=(N_EXPERTS, BATCH // FFN_SAMPLES),
        in_specs=[lat, ctx, wspec, wspec, wspec],
        out_specs=[lat, ctx],
        out_shape=[
            jax.ShapeDtypeStruct((BATCH, N_EXPERTS, CAP_LAT, D_MODEL), BF16),
            jax.ShapeDtypeStruct((BATCH, N_EXPERTS, CAP_CTX, D_MODEL), BF16),
        ],
        scratch_shapes=[pltpu.VMEM((D_MODEL, D_MODEL), BF16)] * 3,
        compiler_params=pltpu.CompilerParams(vmem_limit_bytes=VMEM_LIMIT),
        name="moe_ffn",
    )(xe_l, xe_c, wg, wu, wd)


def _moe_combine_kernel(starts_ref, pos_ref, aff_ref, yl_ref, yc_ref, x_ref, mod_ref, modc_ref, o_ref,
                        g_ref, yw_ref, *, tiles, with_context):
    b = pl.program_id(0)
    step = pl.program_id(1)
    wins = [_all_windows(starts_ref, b, step * tiles + k) for k in range(tiles)]
    lat_fit = functools.reduce(jnp.logical_and, [fits for _, fits in wins[:-1]])
    all_fit = jnp.logical_and(lat_fit, wins[-1][1])

    def gates(g_ref, row0, cap, k, e, first_slot):
        slot = lax.broadcasted_iota(I32, (cap, ROW_TILE), 0)
        g_ref[row0 + e * cap:row0 + (e + 1) * cap, :] = jnp.where(
            slot == pos_ref[0, e:e + 1, _tile_rows(k)] - first_slot,
            aff_ref[0, e:e + 1, _tile_rows(k)], 0.0).astype(BF16)

    def finish(k, moe, m_ref):
        o_ref[0, _tile_rows(k), :] = x_ref[0, _tile_rows(k), :] + m_ref[5:6, :] * moe

    def windowed(k):
        starts = wins[k][0]
        base = k * N_EXPERTS * WIN_SLOTS
        for e in range(N_EXPERTS):
            gates(g_ref, base, WIN_SLOTS, k, e, starts[e])
            row0 = pl.multiple_of(e * CAP_LAT + starts[e], SLOT_ALIGN)
            yw_ref[base + e * WIN_SLOTS:base + (e + 1) * WIN_SLOTS, :] = yl_ref[0, pl.ds(row0, WIN_SLOTS), :]
        finish(k, _dot_tn(g_ref[base:base + N_EXPERTS * WIN_SLOTS, :],
                          yw_ref[base:base + N_EXPERTS * WIN_SLOTS, :]), mod_ref)

    def full(k):
        @pl.when(starts_ref[0] >= 0)
        def _one_tile():
            for e in range(N_EXPERTS):
                gates(g_ref, 0, CAP_LAT, k, e, 0)
            finish(k, _dot_tn(g_ref[...], yl_ref[0]), mod_ref)

    def context(k):
        base = N_EXPERTS * (CAP_LAT - CAP_CTX)
        for e in range(N_EXPERTS):
            gates(g_ref, base, CAP_CTX, k, e, 0)
        finish(k, _dot_tn(g_ref[base:base + N_EXPERTS * CAP_CTX, :], yc_ref[0]), modc_ref)

    last = step == pl.num_programs(1) - 1 if with_context else False
    not_last = jnp.logical_not(last) if with_context else True

    @pl.when(jnp.logical_and(not_last, all_fit))
    def _windowed_step():
        for k in range(tiles):
            windowed(k)

    @pl.when(jnp.logical_and(not_last, jnp.logical_not(all_fit)))
    def _full_step():
        for k in range(tiles):
            full(k)

    if with_context:
        @pl.when(jnp.logical_and(last, lat_fit))
        def _windowed_last_step():
            for k in range(tiles - 1):
                windowed(k)
            context(tiles - 1)

        @pl.when(jnp.logical_and(last, jnp.logical_not(lat_fit)))
        def _full_last_step():
            for k in range(tiles - 1):
                full(k)
            context(tiles - 1)


N_COMBINE_INPUTS = 8
N_COMBINE_SCRATCH = 2


def _combine_project_kernel(*refs, projection_body, n_proj_inputs):
    n_in = N_COMBINE_INPUTS + n_proj_inputs
    combine_in, proj_in = refs[:N_COMBINE_INPUTS], refs[N_COMBINE_INPUTS:n_in]
    x_out, proj_out = refs[n_in], refs[n_in + 1:len(refs) - N_COMBINE_SCRATCH]
    scratch = refs[len(refs) - N_COMBINE_SCRATCH:]
    _moe_combine_kernel(*combine_in, x_out, *scratch, tiles=TILES_PER_STEP, with_context=True)
    projection_body(x_out, *proj_in, *proj_out)


def _moe_combine(starts, pos, aff_t, ye_l, ye_c, x1, mod, layer, with_context, projection=None):
    tiles = TILES_PER_STEP if with_context else 4
    rows = TOK if with_context else SEQ
    step = pl.BlockSpec((1, tiles * ROW_TILE, D_MODEL), lambda b, t, s: (b, t, 0))
    sel = pl.BlockSpec((1, N_EXPERTS, tiles * ROW_TILE), lambda b, t, s: (b, 0, t))
    x_shape = jax.ShapeDtypeStruct((BATCH, rows, D_MODEL), F32)
    if projection is None:
        body = functools.partial(_moe_combine_kernel, tiles=tiles, with_context=with_context)
        extra_specs, extra_args, out_specs, out_shape = [], [], step, x_shape
    else:
        assert with_context and tiles * ROW_TILE == STEP_ROWS
        body = functools.partial(_combine_project_kernel, projection_body=projection.body,
                                 n_proj_inputs=len(projection.args))
        extra_specs, extra_args = projection.in_specs, projection.args
        out_specs, out_shape = [step] + projection.out_specs, [x_shape] + projection.out_shape
    return pl.pallas_call(
        body,
        grid_spec=pltpu.PrefetchScalarGridSpec(
            num_scalar_prefetch=1,
            grid=(BATCH, rows // (tiles * ROW_TILE)),
            in_specs=[
                sel, sel,
                pl.BlockSpec((1, N_EXPERTS * CAP_LAT, D_MODEL), lambda b, t, s: (b, 0, 0)),
                pl.BlockSpec((1, N_EXPERTS * CAP_CTX, D_MODEL), lambda b, t, s: (b, 0, 0)),
                step,
                *_mod_specs(layer),
            ] + extra_specs,
            out_specs=out_specs,
            scratch_shapes=[
                pltpu.VMEM((N_EXPERTS * CAP_LAT, ROW_TILE), BF16),
                pltpu.VMEM((tiles * N_EXPERTS * WIN_SLOTS, D_MODEL), BF16),
            ],
        ),
        out_shape=out_shape,
        compiler_params=pltpu.CompilerParams(vmem_limit_bytes=VMEM_LIMIT_FUSED),
        name="moe_combine" if projection is None else "moe_combine_" + projection.name,
    )(starts, pos, aff_t, ye_l.reshape(BATCH, N_EXPERTS * CAP_LAT, D_MODEL),
      ye_c.reshape(BATCH, N_EXPERTS * CAP_CTX, D_MODEL), x1, mod, mod, *extra_args)


def _rope_tables():
    rows = SEQ // GRID_W
    row = jnp.repeat(jnp.arange(rows, dtype=F32), GRID_W)
    col = jnp.tile(jnp.arange(GRID_W, dtype=F32), rows)
    axis_dim = HEAD_DIM // 2
    inv_freq = ROPE_THETA ** (-jnp.arange(0, axis_dim, 2, dtype=F32) / axis_dim)
    ang_r = row[:, None] * inv_freq[None, :]
    ang_c = col[:, None] * inv_freq[None, :]
    cos_h = jnp.concatenate([jnp.cos(ang_r), jnp.cos(ang_r), jnp.cos(ang_c), jnp.cos(ang_c)], axis=-1)
    sin_h = jnp.concatenate([-jnp.sin(ang_r), jnp.sin(ang_r), -jnp.sin(ang_c), jnp.sin(ang_c)], axis=-1)
    cos_t = jnp.concatenate([jnp.tile(cos_h, (1, 2)), jnp.ones((CTX_LEN, LANES), F32)], axis=0)
    sin_t = jnp.concatenate([jnp.tile(sin_h, (1, 2)), jnp.zeros((CTX_LEN, LANES), F32)], axis=0)
    return cos_t, sin_t


def kernel(x, c, ctx, c_ctx, ada_w, ada_b, norm1_g, norm2_g, attn_w_in, attn_w_out, qnorm_a, knorm_a,
           qnorm_b, knorm_b, sink_b, conv_w_in, conv_k, conv_b, conv_w_out, router_w, moe_w_gate,
           moe_w_up, moe_w_down):
    assert x.shape == (BATCH, SEQ, D_MODEL) and ctx.shape == (BATCH, CTX_LEN, D_MODEL)
    cos_t, sin_t = _rope_tables()
    mod_rows = 32
    c_all = jnp.concatenate([c, c_ctx[None, :], jnp.zeros((mod_rows - BATCH - 1, D_MODEL), F32)], axis=0)
    mod = _adaln(c_all, ada_w, ada_b).reshape(DEPTH, mod_rows, N_ADA, D_MODEL)
    def projection(i):
        j = i // 2
        g1 = norm1_g[i].reshape(1, D_MODEL)
        if i % 2 == 0:
            tile2 = lambda g: jnp.tile(g.reshape(1, HEAD_DIM), (1, 2))
            qk_g = jnp.concatenate([tile2(qnorm_a[j]), tile2(knorm_a[j]), tile2(qnorm_b[j]),
                                    tile2(knorm_b[j])], axis=0)
            return _attn_in(mod, i, g1, attn_w_in[j].astype(BF16), qk_g, cos_t, sin_t)
        return _conv_in(mod, i, g1, conv_w_in[j].astype(BF16))

    xs, first_ctx = x, ctx
    proj = projection(0)(x, ctx)
    for i in range(DEPTH):
        j = i // 2
        last = i == DEPTH - 1
        g2 = norm2_g[i].reshape(1, D_MODEL)
        rw_t = router_w[i].T
        rw_hi = rw_t.astype(BF16)
        rw_split = jnp.concatenate([rw_hi, (rw_t - rw_hi.astype(F32)).astype(BF16)], axis=0)
        if i % 2 == 0:
            q, kv = proj
            o = _attention(q, kv, _attention_params(qnorm_a[j], knorm_a[j], qnorm_b[j], knorm_b[j],
                                                    sink_b[j]), N_Q_TILES)
            x1, h2, aff_t = _mixer_out(xs, mod, i, attn_w_out[j].astype(BF16), g2, rw_split, attn_o=o,
                                       ctx=first_ctx if i == 0 else None)
        else:
            u, bg = proj
            x1, h2, aff_t = _mixer_out(xs, mod, i, conv_w_out[j].astype(BF16), g2, rw_split,
                                       conv=(u, bg, conv_k[j], conv_b[j].reshape(1, D_MODEL)))
        pos, starts = _topk(aff_t)
        pos = pos.reshape(BATCH, N_EXPERTS, TOK)
        starts = starts[:, :STARTS_STRIDE].reshape(-1)
        xe_l, xe_c = _moe_gather(starts, pos, h2)
        ye_l, ye_c = _moe_ffn(xe_l, xe_c, i, moe_w_gate, moe_w_up, moe_w_down)
        if last:
            return _moe_combine(starts, pos, aff_t, ye_l, ye_c, x1, mod, i, with_context=False)
        xs, *proj = _moe_combine(starts, pos, aff_t, ye_l, ye_c, x1, mod, i, with_context=True,
                                 projection=projection(i + 1))
```

```python
import functools

import jax
import jax.numpy as jnp
from jax import lax
from jax.experimental import pallas as pl
from jax.experimental.pallas import tpu as pltpu

F32 = jnp.float32
BF16 = jnp.bfloat16
I32 = jnp.int32

D_MODEL = 1024
BATCH = 16
SEQ = 2048
CTX_LEN = 256
TOK = SEQ + CTX_LEN
DEPTH = 4
GRID_W = 64
HEAD_DIM = 64
N_Q_HEADS = 8
N_KV_HEADS = 2
Q_WIDTH = N_Q_HEADS * HEAD_DIM
KV_WIDTH = N_KV_HEADS * HEAD_DIM
ATT_IN_WIDTH = 2 * (Q_WIDTH + 2 * KV_WIDTH)
WINDOW = 128
ROPE_THETA = 10000.0
N_EXPERTS = 16
CAP_LAT = 2 * SEQ // N_EXPERTS
CAP_CTX = 2 * CTX_LEN // N_EXPERTS
N_ADA = 6
EPS = 1e-6
NEG_INF = -1e30

LANES = 128
ROW_TILE = 256
TILES_PER_STEP = 3
STEP_ROWS = TILES_PER_STEP * ROW_TILE
STEPS_PER_SAMPLE = (SEQ + CTX_LEN) // STEP_ROWS
Q_TILE = 256
N_Q_TILES = TOK // Q_TILE
N_LAT_Q_TILES = SEQ // Q_TILE
WIN_Q_ROWS = 128
WIN_SPAN = WIN_Q_ROWS + 2 * WINDOW
HALO = 16
PREFIX_CHUNK = 256
SLOT_ALIGN = 16
WIN_SLOTS = 64
STARTS_STRIDE = 16
FFN_SAMPLES = 4
VMEM_LIMIT = 56 * 1024 * 1024
VMEM_LIMIT_FUSED = 62 * 1024 * 1024

LOG2E = 1.4426950408889634
PAR_BOUNDED = 8
PAR_LEN = 16
MAX_ABS_LOGIT = 50.0

def _dot(a, b):
    return jnp.dot(a, b, preferred_element_type=F32)


def _dot_nt(a, b):
    return lax.dot_general(a, b, (((1,), (1,)), ((), ())), preferred_element_type=F32)


def _dot_tn(a, b):
    return lax.dot_general(a, b, (((0,), (0,)), ((), ())), preferred_element_type=F32)


def _rmsnorm_rows(x, g):
    return x * lax.rsqrt(jnp.mean(x * x, axis=-1, keepdims=True) + EPS) * g


def _silu(x):
    return x * (1.0 / (1.0 + jnp.exp(-x)))


def _adaln_kernel(c_ref, w_ref, b_ref, o_ref):
    s = _silu(c_ref[...])
    w = w_ref[0]
    s_hi = s.astype(BF16)
    s_lo = (s - s_hi.astype(F32)).astype(BF16)
    w_hi = w.astype(BF16)
    w_lo = (w - w_hi.astype(F32)).astype(BF16)
    rows = s.shape[0]
    part = _dot(jnp.concatenate([s_hi, s_lo], axis=0), w_hi)
    o_ref[0] = part[0:rows] + part[rows:2 * rows] + _dot(s_hi, w_lo) + b_ref[0]


def _adaln(c_all, ada_w, ada_b):
    rows = c_all.shape[0]
    return pl.pallas_call(
        _adaln_kernel,
        grid=(DEPTH, N_ADA),
        in_specs=[
            pl.BlockSpec((rows, D_MODEL), lambda i, j: (0, 0)),
            pl.BlockSpec((1, D_MODEL, D_MODEL), lambda i, j: (i, 0, j)),
            pl.BlockSpec((1, 1, D_MODEL), lambda i, j: (i, 0, j)),
        ],
        out_specs=pl.BlockSpec((1, rows, D_MODEL), lambda i, j: (i, 0, j)),
        out_shape=jax.ShapeDtypeStruct((DEPTH, rows, N_ADA * D_MODEL), F32),
        name="adaln",
    )(c_all, ada_w, ada_b.reshape(DEPTH, 1, N_ADA * D_MODEL))


def _tile_mods(mod_ref, modc_ref):
    last_step = pl.program_id(1) == STEPS_PER_SAMPLE - 1
    mods = [mod_ref[...]] * (TILES_PER_STEP - 1)
    return mods + [jnp.where(last_step, modc_ref[...], mod_ref[...])]


def _tile_rows(k):
    return slice(k * ROW_TILE, (k + 1) * ROW_TILE)


def _modulated_norm(x_ref, g_ref, mods, shift_row, scale_row):
    hs = []
    for k, m in enumerate(mods):
        h = _rmsnorm_rows(x_ref[0, _tile_rows(k), :], g_ref[...])
        hs.append((h * (1.0 + m[scale_row:scale_row + 1, :]) + m[shift_row:shift_row + 1, :]).astype(BF16))
    return jnp.concatenate(hs, axis=0)


def _attn_in_kernel(x_ref, mod_ref, modc_ref, g_ref, w_ref, qkg_ref, cos_ref, sin_ref, q_ref, kv_ref):
    h = _modulated_norm(x_ref, g_ref, _tile_mods(mod_ref, modc_ref), 0, 1)
    p = _dot(h, w_ref[...])

    lane = lax.broadcasted_iota(I32, (STEP_ROWS, LANES), 1)
    r = lax.broadcasted_iota(I32, (LANES, LANES), 0)
    c = lax.broadcasted_iota(I32, (LANES, LANES), 1)
    seg_mean = jnp.where((r >> 6) == (c >> 6), 1.0 / HEAD_DIM, 0.0).astype(BF16)
    cos = cos_ref[...]
    sin = sin_ref[...]
    quarter = HEAD_DIM // 4
    swap_halves = jnp.where(r == c + jnp.where((c & (2 * quarter - 1)) < quarter, quarter, -quarter),
                            1.0, 0.0).astype(BF16)
    low = lane < HEAD_DIM

    def qk_norm_rope(xc, g):
        ms = _dot((xc * xc).astype(BF16), seg_mean)
        xn = xc * lax.rsqrt(ms + EPS) * g
        partner = _dot(xn.astype(BF16), swap_halves)
        return xn * cos + partner * sin

    def put_kv(kind, base, vals, pad):
        rolled = pltpu.roll(vals, HEAD_DIM, axis=1)
        kv_ref[0, kind, 0, base + 0] = jnp.where(low, vals, pad).astype(BF16)
        kv_ref[0, kind, 0, base + 1] = jnp.where(low, pad, rolled).astype(BF16)
        kv_ref[0, kind, 1, base + 0] = jnp.where(low, rolled, pad).astype(BF16)
        kv_ref[0, kind, 1, base + 1] = jnp.where(low, pad, vals).astype(BF16)

    for kind in range(2):
        col0 = kind * (Q_WIDTH + 2 * KV_WIDTH)
        gq = qkg_ref[2 * kind:2 * kind + 1, :]
        gk = qkg_ref[2 * kind + 1:2 * kind + 2, :]
        for ci in range(Q_WIDTH // LANES):
            qc = qk_norm_rope(p[:, col0 + ci * LANES:col0 + (ci + 1) * LANES], gq)
            q_ref[0, :, kind * Q_WIDTH + ci * LANES:kind * Q_WIDTH + (ci + 1) * LANES] = (
                qc * (HEAD_DIM ** -0.5 * LOG2E)).astype(BF16)
        kc = qk_norm_rope(p[:, col0 + Q_WIDTH:col0 + Q_WIDTH + KV_WIDTH], gk)
        put_kv(kind, 0, kc, 0.0)
        put_kv(kind, 2, p[:, col0 + Q_WIDTH + KV_WIDTH:col0 + Q_WIDTH + 2 * KV_WIDTH], 1.0)


def _mod_specs(layer):
    return [pl.BlockSpec((None, None, N_ADA, D_MODEL), lambda b, t, *_: (layer, b, 0, 0)),
            pl.BlockSpec((None, None, N_ADA, D_MODEL), lambda b, t, *_: (layer, BATCH, 0, 0))]


def _step_spec(width=D_MODEL):
    return pl.BlockSpec((1, STEP_ROWS, width), lambda b, t, *_: (b, t, 0))


def _const_spec(shape):
    return pl.BlockSpec(shape, lambda b, t, *_: (0,) * len(shape), pipeline_mode=pl.Buffered(1))


class _SplitStream:
    def __init__(self, x_ref, ctx_ref):
        self.x_ref, self.ctx_ref = x_ref, ctx_ref

    def __getitem__(self, idx):
        _, rows, cols = idx
        assert (rows.stop - rows.start) == ROW_TILE and rows.start % ROW_TILE == 0
        if rows.start != (TILES_PER_STEP - 1) * ROW_TILE:
            return self.x_ref[0, rows, cols]
        last_step = pl.program_id(1) == STEPS_PER_SAMPLE - 1
        return jnp.where(last_step, self.ctx_ref[0, :, cols], self.x_ref[0, rows, cols])


def _split_stream_body(body):
    return lambda x_ref, ctx_ref, *refs: body(_SplitStream(x_ref, ctx_ref), *refs)


def _ctx_spec():
    return pl.BlockSpec((1, CTX_LEN, D_MODEL), lambda b, t, *_: (b, 0, 0))


class _Projection:
    def __init__(self, name, body, in_specs, args, out_specs, out_shape):
        self.name, self.body, self.in_specs, self.args = name, body, in_specs, args
        self.out_specs, self.out_shape = out_specs, out_shape

    def __call__(self, x, ctx):
        return pl.pallas_call(
            _split_stream_body(self.body),
            grid=(BATCH, STEPS_PER_SAMPLE),
            in_specs=[_step_spec(), _ctx_spec()] + self.in_specs,
            out_specs=self.out_specs,
            out_shape=self.out_shape,
            compiler_params=pltpu.CompilerParams(vmem_limit_bytes=VMEM_LIMIT),
            name=self.name,
        )(x, ctx, *self.args)


def _attn_in(mod, layer, norm_g, w_in, qk_g, cos_t, sin_t):
    return _Projection(
        "attn_in", _attn_in_kernel,
        in_specs=[
            *_mod_specs(layer),
            _const_spec((1, D_MODEL)),
            _const_spec((D_MODEL, ATT_IN_WIDTH)),
            _const_spec((4, LANES)),
            pl.BlockSpec((STEP_ROWS, LANES), lambda b, t, *_: (t, 0)),
            pl.BlockSpec((STEP_ROWS, LANES), lambda b, t, *_: (t, 0)),
        ],
        args=[mod, mod, norm_g, w_in, qk_g, cos_t, sin_t],
        out_specs=[
            _step_spec(2 * Q_WIDTH),
            pl.BlockSpec((1, 2, N_KV_HEADS, 4, STEP_ROWS, LANES), lambda b, t, *_: (b, 0, 0, 0, t, 0)),
        ],
        out_shape=[
            jax.ShapeDtypeStruct((BATCH, TOK, 2 * Q_WIDTH), BF16),
            jax.ShapeDtypeStruct((BATCH, 2, N_KV_HEADS, 4, TOK, LANES), BF16),
        ])


def _softmax_pv(parts, bounded, sink=None):
    shift = None
    if not bounded:
        for s, _ in parts:
            mx = jnp.max(s, axis=-1, keepdims=True)
            shift = mx if shift is None else jnp.maximum(shift, mx)
        if sink is not None:
            shift = jnp.maximum(shift, sink)
    shifted = (lambda z: z) if bounded else (lambda z: z - shift)
    acc = None
    for s, v in parts:
        pv = _dot(jnp.exp2(shifted(s)).astype(BF16), v)
        acc = pv if acc is None else acc + pv
    return acc, (jnp.exp2(shifted(sink)) if sink is not None else None)


def _attn_body(par_ref, q_ref, kv_ref, o_ref, bounded):
    t = pl.program_id(1)

    def heads(kind, g, r0, n, parts_of, with_sink):
        c0 = kind * Q_WIDTH + g * 2 * LANES
        lhs = jnp.concatenate([q_ref[0, r0:r0 + n, c0:c0 + LANES],
                               q_ref[0, r0:r0 + n, c0 + LANES:c0 + 2 * LANES]], axis=0)
        first_pair = lax.broadcasted_iota(I32, (2 * n, 1), 0) < n
        outs = []
        for parity in range(2):
            sink = None
            if with_sink:
                sink = jnp.where(first_pair, par_ref[4 * g + parity], par_ref[4 * g + 2 + parity])
            acc, sink_term = _softmax_pv(parts_of(lhs, parity), bounded, sink)
            denom = acc[:, (1 - parity) * HEAD_DIM:(1 - parity) * HEAD_DIM + 1]
            if sink_term is not None:
                denom = denom + sink_term
            outs.append(acc * (1.0 / denom))
        low = lax.broadcasted_iota(I32, (2 * n, LANES), 1) < HEAD_DIM
        o = jnp.where(low, outs[0], outs[1])
        o_ref[0, r0:r0 + n, c0:c0 + LANES] = o[:n].astype(BF16)
        o_ref[0, r0:r0 + n, c0 + LANES:c0 + 2 * LANES] = o[n:].astype(BF16)

    @pl.when(t < N_LAT_Q_TILES)
    def _latent():
        for g in range(N_KV_HEADS):
            heads(0, g, 0, Q_TILE, lambda lhs, parity: [(_dot_nt(lhs, kv_ref[0, 0, g, parity]),
                                                         kv_ref[0, 0, g, 2 + parity])], False)

        for sub in range(Q_TILE // WIN_Q_ROWS):
            q0 = t * Q_TILE + sub * WIN_Q_ROWS
            ks = pl.multiple_of(jnp.clip(q0 - WINDOW, 0, SEQ - WIN_SPAN), WINDOW)
            qi = lax.broadcasted_iota(I32, (2 * WIN_Q_ROWS, WIN_SPAN), 0) & (WIN_Q_ROWS - 1)
            kj = lax.broadcasted_iota(I32, (2 * WIN_Q_ROWS, WIN_SPAN), 1)
            band = jnp.abs(qi - kj + (q0 - ks)) <= WINDOW

            def window_parts(g):
                def parts_of(lhs, parity):
                    s_w = _dot_nt(lhs, kv_ref[0, 1, g, parity, pl.ds(ks, WIN_SPAN), :])
                    s_c = _dot_nt(lhs, kv_ref[0, 1, g, parity, SEQ:TOK, :])
                    return [(jnp.where(band, s_w, NEG_INF),
                             kv_ref[0, 1, g, 2 + parity, pl.ds(ks, WIN_SPAN), :]),
                            (s_c, kv_ref[0, 1, g, 2 + parity, SEQ:TOK, :])]
                return parts_of

            for g in range(N_KV_HEADS):
                heads(1, g, sub * WIN_Q_ROWS, WIN_Q_ROWS, window_parts(g), True)

    @pl.when(t >= N_LAT_Q_TILES)
    def _context():
        for kind in range(2):
            for g in range(N_KV_HEADS):
                heads(kind, g, 0, Q_TILE,
                      lambda lhs, parity: [(_dot_nt(lhs, kv_ref[0, kind, g, parity, SEQ:TOK, :]),
                                            kv_ref[0, kind, g, 2 + parity, SEQ:TOK, :])],
                      kind == 1)


def _attn_kernel(par_ref, q_ref, kv_ref, o_ref):
    bounded = par_ref[PAR_BOUNDED] > 0.5

    @pl.when(bounded)
    def _bounded():
        _attn_body(par_ref, q_ref, kv_ref, o_ref, True)

    @pl.when(jnp.logical_not(bounded))
    def _exact_max():
        _attn_body(par_ref, q_ref, kv_ref, o_ref, False)


def _attention_params(qg_a, kg_a, qg_b, kg_b, sink):
    def bound(qg, kg):
        return 1.02 * LOG2E * HEAD_DIM ** 0.5 * jnp.max(jnp.abs(qg)) * jnp.max(jnp.abs(kg))
    sink2 = sink * LOG2E
    largest = jnp.maximum(jnp.maximum(bound(qg_a, kg_a), bound(qg_b, kg_b)), jnp.max(jnp.abs(sink2)))
    return jnp.concatenate([sink2, (largest < MAX_ABS_LOGIT).astype(F32)[None],
                            jnp.zeros((PAR_LEN - PAR_BOUNDED - 1,), F32)])


def _attention(q, kv, params, n_q_tiles):
    return pl.pallas_call(
        _attn_kernel,
        grid=(BATCH, n_q_tiles),
        in_specs=[
            pl.BlockSpec(memory_space=pltpu.SMEM),
            pl.BlockSpec((1, Q_TILE, 2 * Q_WIDTH), lambda b, t: (b, t, 0)),
            pl.BlockSpec((1, 2, N_KV_HEADS, 4, TOK, LANES), lambda b, t: (b, 0, 0, 0, 0, 0)),
        ],
        out_specs=pl.BlockSpec((1, Q_TILE, 2 * Q_WIDTH), lambda b, t: (b, t, 0)),
        out_shape=jax.ShapeDtypeStruct((BATCH, TOK, 2 * Q_WIDTH), BF16),
        compiler_params=pltpu.CompilerParams(vmem_limit_bytes=VMEM_LIMIT),
        name="attention",
    )(params, q, kv)


def _conv_in_kernel(x_ref, mod_ref, modc_ref, g_ref, w_ref, u_ref, bg_ref):
    h = _modulated_norm(x_ref, g_ref, _tile_mods(mod_ref, modc_ref), 0, 1)
    bg_ref[0] = _dot(h, w_ref[:, 0:D_MODEL]).astype(BF16)
    cg = _dot(h, w_ref[:, D_MODEL:2 * D_MODEL])
    v = _dot(h, w_ref[:, 2 * D_MODEL:3 * D_MODEL])
    u_ref[0] = (cg * v).astype(BF16)


def _conv_in(mod, layer, norm_g, w_in):
    return _Projection(
        "conv_in", _conv_in_kernel,
        in_specs=[*_mod_specs(layer), _const_spec((1, D_MODEL)), _const_spec((D_MODEL, 3 * D_MODEL))],
        args=[mod, mod, norm_g, w_in],
        out_specs=[_step_spec(), _step_spec()],
        out_shape=[
            jax.ShapeDtypeStruct((BATCH, TOK, D_MODEL), BF16),
            jax.ShapeDtypeStruct((BATCH, TOK, D_MODEL), BF16),
        ])


def _tail(lhs, w_ref, x_ref, mod_ref, modc_ref, g_ref, rw_ref, x1_ref, h2_ref, aff_ref):
    h2_hi, h2_lo = [], []
    for k, m in enumerate(_tile_mods(mod_ref, modc_ref)):
        x1 = x_ref[0, _tile_rows(k), :] + m[2:3, :] * _dot(lhs[_tile_rows(k)], w_ref[...])
        x1_ref[0, _tile_rows(k), :] = x1
        h2 = _rmsnorm_rows(x1, g_ref[...])
        h2 = h2 * (1.0 + m[4:5, :]) + m[3:4, :]
        hi = h2.astype(BF16)
        h2_ref[0, _tile_rows(k), :] = hi
        h2_hi.append(hi)
        h2_lo.append((h2 - hi.astype(F32)).astype(BF16))
    rw = rw_ref[...]
    part = _dot_nt(rw, jnp.concatenate(h2_hi, axis=0))
    logits = (part[0:N_EXPERTS] + part[N_EXPERTS:2 * N_EXPERTS]
              + _dot_nt(rw[0:N_EXPERTS], jnp.concatenate(h2_lo, axis=0)))
    e = jnp.exp(logits - jnp.max(logits, axis=0, keepdims=True))
    aff_ref[0] = e / jnp.sum(e, axis=0, keepdims=True)


def _attn_out_kernel(o_ref, w_ref, x_ref, mod_ref, modc_ref, g_ref, rw_ref, x1_ref, h2_ref, aff_ref):
    _tail(o_ref[0], w_ref, x_ref, mod_ref, modc_ref, g_ref, rw_ref, x1_ref, h2_ref, aff_ref)


def _conv_out_kernel(u_ref, up_ref, un_ref, bg_ref, ck_ref, cb_ref, w_ref, x_ref, mod_ref, modc_ref,
                     g_ref, rw_ref, x1_ref, h2_ref, aff_ref):
    u = u_ref[0].astype(F32)
    row = lax.broadcasted_iota(I32, (STEP_ROWS, 1), 0)
    pos = row + pl.program_id(1) * STEP_ROWS
    seq_first = jnp.logical_or(pos == 0, pos == SEQ)
    seq_last = jnp.logical_or(pos == SEQ - 1, pos == TOK - 1)
    u_prev = jnp.where(row == 0, up_ref[0, HALO - 1:HALO, :].astype(F32), pltpu.roll(u, 1, axis=0))
    u_next = jnp.where(row == STEP_ROWS - 1, un_ref[0, 0:1, :].astype(F32),
                       pltpu.roll(u, STEP_ROWS - 1, axis=0))
    u_prev = jnp.where(seq_first, 0.0, u_prev)
    u_next = jnp.where(seq_last, 0.0, u_next)
    y = ck_ref[0:1, :] * u_prev + ck_ref[1:2, :] * u + ck_ref[2:3, :] * u_next + cb_ref[...]
    lhs = (bg_ref[0].astype(F32) * y).astype(BF16)
    _tail(lhs, w_ref, x_ref, mod_ref, modc_ref, g_ref, rw_ref, x1_ref, h2_ref, aff_ref)


def _attn_out_split_kernel(o_ref, w_ref, x_ref, ctx_ref, *refs):
    _attn_out_kernel(o_ref, w_ref, _SplitStream(x_ref, ctx_ref), *refs)


def _mixer_out(xs, mod, layer, w_out, norm_g, rw_split, attn_o=None, conv=None, ctx=None):
    step = pl.BlockSpec((1, STEP_ROWS, D_MODEL), lambda b, t: (b, t, 0))
    common_specs = [
        pl.BlockSpec((D_MODEL, D_MODEL), lambda b, t: (0, 0)),
        step,
        *([] if ctx is None else [_ctx_spec()]),
        *_mod_specs(layer),
        pl.BlockSpec((1, D_MODEL), lambda b, t: (0, 0)),
        pl.BlockSpec((2 * N_EXPERTS, D_MODEL), lambda b, t: (0, 0)),
    ]
    stream = [xs] if ctx is None else [xs, ctx]
    if conv is None:
        body = _attn_out_kernel if ctx is None else _attn_out_split_kernel
        specs, args = [step], [attn_o]
    else:
        assert ctx is None
        u, bg, ck, cb = conv
        halo_per_step = STEP_ROWS // HALO
        last_halo = TOK // HALO - 1
        body = _conv_out_kernel
        specs = [
            step,
            pl.BlockSpec((1, HALO, D_MODEL), lambda b, t: (b, jnp.maximum(t * halo_per_step - 1, 0), 0)),
            pl.BlockSpec((1, HALO, D_MODEL),
                         lambda b, t: (b, jnp.minimum((t + 1) * halo_per_step, last_halo), 0)),
            step,
            pl.BlockSpec((3, D_MODEL), lambda b, t: (0, 0)),
            pl.BlockSpec((1, D_MODEL), lambda b, t: (0, 0)),
        ]
        args = [u, u, u, bg, ck, cb]
    return pl.pallas_call(
        body,
        grid=(BATCH, STEPS_PER_SAMPLE),
        in_specs=specs + common_specs,
        out_specs=[
            step,
            step,
            pl.BlockSpec((1, N_EXPERTS, STEP_ROWS), lambda b, t: (b, 0, t)),
        ],
        out_shape=[
            jax.ShapeDtypeStruct((BATCH, TOK, D_MODEL), F32),
            jax.ShapeDtypeStruct((BATCH, TOK, D_MODEL), BF16),
            jax.ShapeDtypeStruct((BATCH, N_EXPERTS, TOK), F32),
        ],
        compiler_params=pltpu.CompilerParams(vmem_limit_bytes=VMEM_LIMIT),
        name="mixer_out",
    )(*args, w_out, *stream, mod, mod, norm_g, rw_split)


def _select_slots(v, k):
    rows, n = v.shape
    u = pltpu.bitcast(v, I32)

    def body(i, thr):
        cand = thr | jnp.left_shift(jnp.int32(1), 30 - i)
        cnt = jnp.sum(jnp.where(u >= cand, 1.0, 0.0), axis=1, keepdims=True)
        return jnp.where(cnt >= k, cand, thr)

    thr = lax.fori_loop(0, 31, body, jnp.zeros((rows, 1), I32))
    gt = u > thr
    eq = u == thr
    need = k - jnp.sum(jnp.where(gt, 1.0, 0.0), axis=1, keepdims=True)
    r = lax.broadcasted_iota(I32, (PREFIX_CHUNK, PREFIX_CHUNK), 0)
    c = lax.broadcasted_iota(I32, (PREFIX_CHUNK, PREFIX_CHUNK), 1)
    before = jnp.where(r < c, 1.0, 0.0).astype(BF16)

    def prefix_count(flags):
        out, carry = [], jnp.zeros((rows, 1), F32)
        for j in range(n // PREFIX_CHUNK):
            chunk = flags[:, j * PREFIX_CHUNK:(j + 1) * PREFIX_CHUNK]
            out.append(_dot(chunk.astype(BF16), before) + carry)
            carry = carry + jnp.sum(chunk, axis=1, keepdims=True)
        return jnp.concatenate(out, axis=1)

    eq_rank = prefix_count(jnp.where(eq, 1.0, 0.0))
    sel = jnp.where(gt, 1.0, jnp.where(eq, jnp.where(eq_rank < need, 1.0, 0.0), 0.0))
    slot = prefix_count(sel)
    return jnp.where(sel > 0.5, slot, -1.0).astype(I32), sel


def _topk_kernel(a_ref, pos_ref, starts_ref):
    pos_l, sel_l = _select_slots(a_ref[:, 0:SEQ], CAP_LAT)
    pos_c, _ = _select_slots(a_ref[:, SEQ:TOK], CAP_CTX)
    pos_ref[:, 0:SEQ] = pos_l
    pos_ref[:, SEQ:TOK] = pos_c
    tok = lax.broadcasted_iota(I32, (SEQ, LANES), 0)
    tile = lax.broadcasted_iota(I32, (SEQ, LANES), 1)
    earlier = jnp.where(tok < tile * ROW_TILE, 1.0, 0.0).astype(BF16)
    starts_ref[...] = _dot(sel_l.astype(BF16), earlier).astype(I32)


def _topk(aff_t):
    rows = BATCH * N_EXPERTS
    return pl.pallas_call(
        _topk_kernel,
        out_shape=[jax.ShapeDtypeStruct((rows, TOK), I32), jax.ShapeDtypeStruct((rows, LANES), I32)],
        compiler_params=pltpu.CompilerParams(vmem_limit_bytes=VMEM_LIMIT),
        name="topk",
    )(aff_t.reshape(rows, TOK))


def _slot_window(starts_ref, b, e, t):
    base = (b * N_EXPERTS + e) * STARTS_STRIDE + t
    s = starts_ref[base]
    n = starts_ref[base + 1] - s
    a = jnp.minimum(s - (s & (SLOT_ALIGN - 1)), CAP_LAT - WIN_SLOTS)
    return pl.multiple_of(a, SLOT_ALIGN), (s - a + n) <= WIN_SLOTS


def _all_windows(starts_ref, b, t):
    wins = [_slot_window(starts_ref, b, e, t) for e in range(N_EXPERTS)]
    fits = functools.reduce(jnp.logical_and, [ok for _, ok in wins])
    return [a for a, _ in wins], fits


def _moe_gather_kernel(starts_ref, pos_ref, h_ref, xl_ref, xc_ref):
    b = pl.program_id(0)
    step = pl.program_id(1)
    last = step == STEPS_PER_SAMPLE - 1
    wins = [_all_windows(starts_ref, b, step * TILES_PER_STEP + k) for k in range(TILES_PER_STEP)]
    lat_fit = functools.reduce(jnp.logical_and, [fits for _, fits in wins[:-1]])
    all_fit = jnp.logical_and(lat_fit, wins[-1][1])

    def tile_pos(k, e):
        return pos_ref[0, e:e + 1, _tile_rows(k)]

    def windowed(k):
        starts = wins[k][0]
        slot = lax.broadcasted_iota(I32, (WIN_SLOTS, ROW_TILE), 0)
        picks = [jnp.where(slot == tile_pos(k, e) - starts[e], 1.0, 0.0).astype(BF16)
                 for e in range(N_EXPERTS)]
        y = _dot(jnp.concatenate(picks, axis=0), h_ref[0, _tile_rows(k), :]).astype(BF16)
        for e in range(N_EXPERTS):
            xl_ref[0, e, pl.ds(starts[e], WIN_SLOTS), :] += y[e * WIN_SLOTS:(e + 1) * WIN_SLOTS]

    def full(k):
        slot = lax.broadcasted_iota(I32, (CAP_LAT, ROW_TILE), 0)
        for e in range(N_EXPERTS):
            pick = jnp.where(slot == tile_pos(k, e), 1.0, 0.0).astype(BF16)
            xl_ref[0, e] += _dot(pick, h_ref[0, _tile_rows(k), :]).astype(BF16)

    def context(k):
        slot = lax.broadcasted_iota(I32, (CAP_CTX, ROW_TILE), 0)
        picks = [jnp.where(slot == tile_pos(k, e), 1.0, 0.0).astype(BF16) for e in range(N_EXPERTS)]
        y = _dot(jnp.concatenate(picks, axis=0), h_ref[0, _tile_rows(k), :]).astype(BF16)
        for e in range(N_EXPERTS):
            xc_ref[0, e] = y[e * CAP_CTX:(e + 1) * CAP_CTX]

    @pl.when(step == 0)
    def _zero():
        xl_ref[...] = jnp.zeros(xl_ref.shape, BF16)

    @pl.when(jnp.logical_and(jnp.logical_not(last), all_fit))
    def _windowed_step():
        for k in range(TILES_PER_STEP):
            windowed(k)

    @pl.when(jnp.logical_and(jnp.logical_not(last), jnp.logical_not(all_fit)))
    def _full_step():
        for k in range(TILES_PER_STEP):
            full(k)

    @pl.when(jnp.logical_and(last, lat_fit))
    def _windowed_last_step():
        for k in range(TILES_PER_STEP - 1):
            windowed(k)
        context(TILES_PER_STEP - 1)

    @pl.when(jnp.logical_and(last, jnp.logical_not(lat_fit)))
    def _full_last_step():
        for k in range(TILES_PER_STEP - 1):
            full(k)
        context(TILES_PER_STEP - 1)


def _moe_gather(starts, pos, h2):
    return pl.pallas_call(
        _moe_gather_kernel,
        grid_spec=pltpu.PrefetchScalarGridSpec(
            num_scalar_prefetch=1,
            grid=(BATCH, STEPS_PER_SAMPLE),
            in_specs=[
                pl.BlockSpec((1, N_EXPERTS, STEP_ROWS), lambda b, t, s: (b, 0, t)),
                pl.BlockSpec((1, STEP_ROWS, D_MODEL), lambda b, t, s: (b, t, 0)),
            ],
            out_specs=[
                pl.BlockSpec((1, N_EXPERTS, CAP_LAT, D_MODEL), lambda b, t, s: (b, 0, 0, 0)),
                pl.BlockSpec((1, N_EXPERTS, CAP_CTX, D_MODEL), lambda b, t, s: (b, 0, 0, 0)),
            ],
        ),
        out_shape=[
            jax.ShapeDtypeStruct((BATCH, N_EXPERTS, CAP_LAT, D_MODEL), BF16),
            jax.ShapeDtypeStruct((BATCH, N_EXPERTS, CAP_CTX, D_MODEL), BF16),
        ],
        compiler_params=pltpu.CompilerParams(vmem_limit_bytes=VMEM_LIMIT),
        name="moe_gather",
    )(starts, pos, h2)


def _moe_ffn_kernel(xl_ref, xc_ref, wg_ref, wu_ref, wd_ref, yl_ref, yc_ref):
    xe = jnp.concatenate([r[i, 0] for i in range(FFN_SAMPLES) for r in (xl_ref, xc_ref)], axis=0)
    hid = (_silu(_dot(xe, wg_ref[0].astype(BF16))) * _dot(xe, wu_ref[0].astype(BF16))).astype(BF16)
    ye = _dot(hid, wd_ref[0].astype(BF16)).astype(BF16)
    cap = CAP_LAT + CAP_CTX
    for i in range(FFN_SAMPLES):
        yl_ref[i, 0] = ye[i * cap:i * cap + CAP_LAT]
        yc_ref[i, 0] = ye[i * cap + CAP_LAT:(i + 1) * cap]


def _moe_ffn(xe_l, xe_c, layer, wg, wu, wd):
    wspec = pl.BlockSpec((None, 1, D_MODEL, D_MODEL), lambda e, b: (layer, e, 0, 0))
    lat = pl.BlockSpec((FFN_SAMPLES, 1, CAP_LAT, D_MODEL), lambda e, b: (b, e, 0, 0))
    ctx = pl.BlockSpec((FFN_SAMPLES, 1, CAP_CTX, D_MODEL), lambda e, b: (b, e, 0, 0))
    return pl.pallas_call(
        _moe_ffn_kernel,
        grid=(N_EXPERTS, BATCH // FFN_SAMPLES),
        in_specs=[lat, ctx, wspec, wspec, wspec],
        out_specs=[lat, ctx],
        out_shape=[
            jax.ShapeDtypeStruct((BATCH, N_EXPERTS, CAP_LAT, D_MODEL), BF16),
            jax.ShapeDtypeStruct((BATCH, N_EXPERTS, CAP_CTX, D_MODEL), BF16),
        ],
        compiler_params=pltpu.CompilerParams(vmem_limit_bytes=VMEM_LIMIT),
        name="moe_ffn",
    )(xe_l, xe_c, wg, wu, wd)


def _moe_combine_kernel(starts_ref, pos_ref, aff_ref, yl_ref, yc_ref, x_ref, mod_ref, modc_ref, o_ref,
                        g_ref, yw_ref, *, tiles, with_context):
    b = pl.program_id(0)
    step = pl.program_id(1)
    wins = [_all_windows(starts_ref, b, step * tiles + k) for k in range(tiles)]
    lat_fit = functools.reduce(jnp.logical_and, [fits for _, fits in wins[:-1]])
    all_fit = jnp.logical_and(lat_fit, wins[-1][1])

    def gates(g_ref, row0, cap, k, e, first_slot):
        slot = lax.broadcasted_iota(I32, (cap, ROW_TILE), 0)
        g_ref[row0 + e * cap:row0 + (e + 1) * cap, :] = jnp.where(
            slot == pos_ref[0, e:e + 1, _tile_rows(k)] - first_slot,
            aff_ref[0, e:e + 1, _tile_rows(k)], 0.0).astype(BF16)

    def finish(k, moe, m_ref):
        o_ref[0, _tile_rows(k), :] = x_ref[0, _tile_rows(k), :] + m_ref[5:6, :] * moe

    def windowed(k):
        starts = wins[k][0]
        base = k * N_EXPERTS * WIN_SLOTS
        for e in range(N_EXPERTS):
            gates(g_ref, base, WIN_SLOTS, k, e, starts[e])
            row0 = pl.multiple_of(e * CAP_LAT + starts[e], SLOT_ALIGN)
            yw_ref[base + e * WIN_SLOTS:base + (e + 1) * WIN_SLOTS, :] = yl_ref[0, pl.ds(row0, WIN_SLOTS), :]
        finish(k, _dot_tn(g_ref[base:base + N_EXPERTS * WIN_SLOTS, :],
                          yw_ref[base:base + N_EXPERTS * WIN_SLOTS, :]), mod_ref)

    def full(k):
        @pl.when(starts_ref[0] >= 0)
        def _one_tile():
            for e in range(N_EXPERTS):
                gates(g_ref, 0, CAP_LAT, k, e, 0)
            finish(k, _dot_tn(g_ref[...], yl_ref[0]), mod_ref)

    def context(k):
        base = N_EXPERTS * (CAP_LAT - CAP_CTX)
        for e in range(N_EXPERTS):
            gates(g_ref, base, CAP_CTX, k, e, 0)
        finish(k, _dot_tn(g_ref[base:base + N_EXPERTS * CAP_CTX, :], yc_ref[0]), modc_ref)

    last = step == pl.num_programs(1) - 1 if with_context else False
    not_last = jnp.logical_not(last) if with_context else True

    @pl.when(jnp.logical_and(not_last, all_fit))
    def _windowed_step():
        for k in range(tiles):
            windowed(k)

    @pl.when(jnp.logical_and(not_last, jnp.logical_not(all_fit)))
    def _full_step():
        for k in range(tiles):
            full(k)

    if with_context:
        @pl.when(jnp.logical_and(last, lat_fit))
        def _windowed_last_step():
            for k in range(tiles - 1):
                windowed(k)
            context(tiles - 1)

        @pl.when(jnp.logical_and(last, jnp.logical_not(lat_fit)))
        def _full_last_step():
            for k in range(tiles - 1):
                full(k)
            context(tiles - 1)


N_COMBINE_INPUTS = 8
N_COMBINE_SCRATCH = 2


def _combine_project_kernel(*refs, projection_body, n_proj_inputs):
    n_in = N_COMBINE_INPUTS + n_proj_inputs
    combine_in, proj_in = refs[:N_COMBINE_INPUTS], refs[N_COMBINE_INPUTS:n_in]
    x_out, proj_out = refs[n_in], refs[n_in + 1:len(refs) - N_COMBINE_SCRATCH]
    scratch = refs[len(refs) - N_COMBINE_SCRATCH:]
    _moe_combine_kernel(*combine_in, x_out, *scratch, tiles=TILES_PER_STEP, with_context=True)
    projection_body(x_out, *proj_in, *proj_out)


def _moe_combine(starts, pos, aff_t, ye_l, ye_c, x1, mod, layer, with_context, projection=None):
    tiles = TILES_PER_STEP if with_context else 4
    rows = TOK if with_context else SEQ
    step = pl.BlockSpec((1, tiles * ROW_TILE, D_MODEL), lambda b, t, s: (b, t, 0))
    sel = pl.BlockSpec((1, N_EXPERTS, tiles * ROW_TILE), lambda b, t, s: (b, 0, t))
    x_shape = jax.ShapeDtypeStruct((BATCH, rows, D_MODEL), F32)
    if projection is None:
        body = functools.partial(_moe_combine_kernel, tiles=tiles, with_context=with_context)
        extra_specs, extra_args, out_specs, out_shape = [], [], step, x_shape
    else:
        assert with_context and tiles * ROW_TILE == STEP_ROWS
        body = functools.partial(_combine_project_kernel, projection_body=projection.body,
                                 n_proj_inputs=len(projection.args))
        extra_specs, extra_args = projection.in_specs, projection.args
        out_specs, out_shape = [step] + projection.out_specs, [x_shape] + projection.out_shape
    return pl.pallas_call(
        body,
        grid_spec=pltpu.PrefetchScalarGridSpec(
            num_scalar_prefetch=1,
            grid=(BATCH, rows // (tiles * ROW_TILE)),
            in_specs=[
                sel, sel,
                pl.BlockSpec((1, N_EXPERTS * CAP_LAT, D_MODEL), lambda b, t, s: (b, 0, 0)),
                pl.BlockSpec((1, N_EXPERTS * CAP_CTX, D_MODEL), lambda b, t, s: (b, 0, 0)),
                step,
                *_mod_specs(layer),
            ] + extra_specs,
            out_specs=out_specs,
            scratch_shapes=[
                pltpu.VMEM((N_EXPERTS * CAP_LAT, ROW_TILE), BF16),
                pltpu.VMEM((tiles * N_EXPERTS * WIN_SLOTS, D_MODEL), BF16),
            ],
        ),
        out_shape=out_shape,
        compiler_params=pltpu.CompilerParams(vmem_limit_bytes=VMEM_LIMIT_FUSED),
        name="moe_combine" if projection is None else "moe_combine_" + projection.name,
    )(starts, pos, aff_t, ye_l.reshape(BATCH, N_EXPERTS * CAP_LAT, D_MODEL),
      ye_c.reshape(BATCH, N_EXPERTS * CAP_CTX, D_MODEL), x1, mod, mod, *extra_args)


def _rope_tables():
    rows = SEQ // GRID_W
    row = jnp.repeat(jnp.arange(rows, dtype=F32), GRID_W)
    col = jnp.tile(jnp.arange(GRID_W, dtype=F32), rows)
    axis_dim = HEAD_DIM // 2
    inv_freq = ROPE_THETA ** (-jnp.arange(0, axis_dim, 2, dtype=F32) / axis_dim)
    ang_r = row[:, None] * inv_freq[None, :]
    ang_c = col[:, None] * inv_freq[None, :]
    cos_h = jnp.concatenate([jnp.cos(ang_r), jnp.cos(ang_r), jnp.cos(ang_c), jnp.cos(ang_c)], axis=-1)
    sin_h = jnp.concatenate([-jnp.sin(ang_r), jnp.sin(ang_r), -jnp.sin(ang_c), jnp.sin(ang_c)], axis=-1)
    cos_t = jnp.concatenate([jnp.tile(cos_h, (1, 2)), jnp.ones((CTX_LEN, LANES), F32)], axis=0)
    sin_t = jnp.concatenate([jnp.tile(sin_h, (1, 2)), jnp.zeros((CTX_LEN, LANES), F32)], axis=0)
    return cos_t, sin_t


def kernel(x, c, ctx, c_ctx, ada_w, ada_b, norm1_g, norm2_g, attn_w_in, attn_w_out, qnorm_a, knorm_a,
           qnorm_b, knorm_b, sink_b, conv_w_in, conv_k, conv_b, conv_w_out, router_w, moe_w_gate,
           moe_w_up, moe_w_down):
    assert x.shape == (BATCH, SEQ, D_MODEL) and ctx.shape == (BATCH, CTX_LEN, D_MODEL)
    cos_t, sin_t = _rope_tables()
    mod_rows = 32
    c_all = jnp.concatenate([c, c_ctx[None, :], jnp.zeros((mod_rows - BATCH - 1, D_MODEL), F32)], axis=0)
    mod = _adaln(c_all, ada_w, ada_b).reshape(DEPTH, mod_rows, N_ADA, D_MODEL)
    def projection(i):
        j = i // 2
        g1 = norm1_g[i].reshape(1, D_MODEL)
        if i % 2 == 0:
            tile2 = lambda g: jnp.tile(g.reshape(1, HEAD_DIM), (1, 2))
            qk_g = jnp.concatenate([tile2(qnorm_a[j]), tile2(knorm_a[j]), tile2(qnorm_b[j]),
                                    tile2(knorm_b[j])], axis=0)
            return _attn_in(mod, i, g1, attn_w_in[j].astype(BF16), qk_g, cos_t, sin_t)
        return _conv_in(mod, i, g1, conv_w_in[j].astype(BF16))

    xs, first_ctx = x, ctx
    proj = projection(0)(x, ctx)
    for i in range(DEPTH):
        j = i // 2
        last = i == DEPTH - 1
        g2 = norm2_g[i].reshape(1, D_MODEL)
        rw_t = router_w[i].T
        rw_hi = rw_t.astype(BF16)
        rw_split = jnp.concatenate([rw_hi, (rw_t - rw_hi.astype(F32)).astype(BF16)], axis=0)
        if i % 2 == 0:
            q, kv = proj
            o = _attention(q, kv, _attention_params(qnorm_a[j], knorm_a[j], qnorm_b[j], knorm_b[j],
                                                    sink_b[j]), N_Q_TILES)
            x1, h2, aff_t = _mixer_out(xs, mod, i, attn_w_out[j].astype(BF16), g2, rw_split, attn_o=o,
                                       ctx=first_ctx if i == 0 else None)
        else:
            u, bg = proj
            x1, h2, aff_t = _mixer_out(xs, mod, i, conv_w_out[j].astype(BF16), g2, rw_split,
                                       conv=(u, bg, conv_k[j], conv_b[j].reshape(1, D_MODEL)))
        pos, starts = _topk(aff_t)
        pos = pos.reshape(BATCH, N_EXPERTS, TOK)
        starts = starts[:, :STARTS_STRIDE].reshape(-1)
        xe_l, xe_c = _moe_gather(starts, pos, h2)
        ye_l, ye_c = _moe_ffn(xe_l, xe_c, i, moe_w_gate, moe_w_up, moe_w_down)
        if last:
            return _moe_combine(starts, pos, aff_t, ye_l, ye_c, x1, mod, i, with_context=False)
        xs, *proj = _moe_combine(starts, pos, aff_t, ye_l, ye_c, x1, mod, i, with_context=True,
                                 projection=projection(i + 1))
```

```python
import functools

import jax
import jax.numpy as jnp
from jax import lax
from jax.experimental import pallas as pl
from jax.experimental.pallas import tpu as pltpu

F32 = jnp.float32
BF16 = jnp.bfloat16
I32 = jnp.int32

D_MODEL = 1024
BATCH = 16
SEQ = 2048
CTX_LEN = 256
TOK = SEQ + CTX_LEN
DEPTH = 4
GRID_W = 64
HEAD_DIM = 64
N_Q_HEADS = 8
N_KV_HEADS = 2
Q_WIDTH = N_Q_HEADS * HEAD_DIM
KV_WIDTH = N_KV_HEADS * HEAD_DIM
ATT_IN_WIDTH = 2 * (Q_WIDTH + 2 * KV_WIDTH)
WINDOW = 128
ROPE_THETA = 10000.0
N_EXPERTS = 16
CAP_LAT = 2 * SEQ // N_EXPERTS
CAP_CTX = 2 * CTX_LEN // N_EXPERTS
N_ADA = 6
EPS = 1e-6
NEG_INF = -1e30

LANES = 128
ROW_TILE = 256
TILES_PER_STEP = 3
STEP_ROWS = TILES_PER_STEP * ROW_TILE
STEPS_PER_SAMPLE = (SEQ + CTX_LEN) // STEP_ROWS
Q_TILE = 256
N_Q_TILES = TOK // Q_TILE
N_LAT_Q_TILES = SEQ // Q_TILE
WIN_Q_ROWS = 128
WIN_SPAN = WIN_Q_ROWS + 2 * WINDOW
HALO = 16
PREFIX_CHUNK = 256
SLOT_ALIGN = 16
WIN_SLOTS = 64
STARTS_STRIDE = 16
FFN_SAMPLES = 4
VMEM_LIMIT = 56 * 1024 * 1024
VMEM_LIMIT_FUSED = 62 * 1024 * 1024

LOG2E = 1.4426950408889634
PAR_BOUNDED = 8
PAR_LEN = 16
MAX_ABS_LOGIT = 50.0

def _dot(a, b):
    return jnp.dot(a, b, preferred_element_type=F32)


def _dot_nt(a, b):
    return lax.dot_general(a, b, (((1,), (1,)), ((), ())), preferred_element_type=F32)


def _dot_tn(a, b):
    return lax.dot_general(a, b, (((0,), (0,)), ((), ())), preferred_element_type=F32)


def _rmsnorm_rows(x, g):
    return x * lax.rsqrt(jnp.mean(x * x, axis=-1, keepdims=True) + EPS) * g


def _silu(x):
    return x * (1.0 / (1.0 + jnp.exp(-x)))


def _adaln_kernel(c_ref, w_ref, b_ref, o_ref):
    s = _silu(c_ref[...])
    w = w_ref[0]
    s_hi = s.astype(BF16)
    s_lo = (s - s_hi.astype(F32)).astype(BF16)
    w_hi = w.astype(BF16)
    w_lo = (w - w_hi.astype(F32)).astype(BF16)
    rows = s.shape[0]
    part = _dot(jnp.concatenate([s_hi, s_lo], axis=0), w_hi)
    o_ref[0] = part[0:rows] + part[rows:2 * rows] + _dot(s_hi, w_lo) + b_ref[0]


def _adaln(c_all, ada_w, ada_b):
    rows = c_all.shape[0]
    return pl.pallas_call(
        _adaln_kernel,
        grid=(DEPTH, N_ADA),
        in_specs=[
            pl.BlockSpec((rows, D_MODEL), lambda i, j: (0, 0)),
            pl.BlockSpec((1, D_MODEL, D_MODEL), lambda i, j: (i, 0, j)),
            pl.BlockSpec((1, 1, D_MODEL), lambda i, j: (i, 0, j)),
        ],
        out_specs=pl.BlockSpec((1, rows, D_MODEL), lambda i, j: (i, 0, j)),
        out_shape=jax.ShapeDtypeStruct((DEPTH, rows, N_ADA * D_MODEL), F32),
        name="adaln",
    )(c_all, ada_w, ada_b.reshape(DEPTH, 1, N_ADA * D_MODEL))


def _tile_mods(mod_ref, modc_ref):
    last_step = pl.program_id(1) == STEPS_PER_SAMPLE - 1
    mods = [mod_ref[...]] * (TILES_PER_STEP - 1)
    return mods + [jnp.where(last_step, modc_ref[...], mod_ref[...])]


def _tile_rows(k):
    return slice(k * ROW_TILE, (k + 1) * ROW_TILE)


def _modulated_norm(x_ref, g_ref, mods, shift_row, scale_row):
    hs = []
    for k, m in enumerate(mods):
        h = _rmsnorm_rows(x_ref[0, _tile_rows(k), :], g_ref[...])
        hs.append((h * (1.0 + m[scale_row:scale_row + 1, :]) + m[shift_row:shift_row + 1, :]).astype(BF16))
    return jnp.concatenate(hs, axis=0)


def _attn_in_kernel(x_ref, mod_ref, modc_ref, g_ref, w_ref, qkg_ref, cos_ref, sin_ref, q_ref, kv_ref):
    h = _modulated_norm(x_ref, g_ref, _tile_mods(mod_ref, modc_ref), 0, 1)
    p = _dot(h, w_ref[...])

    lane = lax.broadcasted_iota(I32, (STEP_ROWS, LANES), 1)
    r = lax.broadcasted_iota(I32, (LANES, LANES), 0)
    c = lax.broadcasted_iota(I32, (LANES, LANES), 1)
    seg_mean = jnp.where((r >> 6) == (c >> 6), 1.0 / HEAD_DIM, 0.0).astype(BF16)
    cos = cos_ref[...]
    sin = sin_ref[...]
    quarter = HEAD_DIM // 4
    swap_halves = jnp.where(r == c + jnp.where((c & (2 * quarter - 1)) < quarter, quarter, -quarter),
                            1.0, 0.0).astype(BF16)
    low = lane < HEAD_DIM

    def qk_norm_rope(xc, g):
        ms = _dot((xc * xc).astype(BF16), seg_mean)
        xn = xc * lax.rsqrt(ms + EPS) * g
        partner = _dot(xn.astype(BF16), swap_halves)
        return xn * cos + partner * sin

    def put_kv(kind, base, vals, pad):
        rolled = pltpu.roll(vals, HEAD_DIM, axis=1)
        kv_ref[0, kind, 0, base + 0] = jnp.where(low, vals, pad).astype(BF16)
        kv_ref[0, kind, 0, base + 1] = jnp.where(low, pad, rolled).astype(BF16)
        kv_ref[0, kind, 1, base + 0] = jnp.where(low, rolled, pad).astype(BF16)
        kv_ref[0, kind, 1, base + 1] = jnp.where(low, pad, vals).astype(BF16)

    for kind in range(2):
        col0 = kind * (Q_WIDTH + 2 * KV_WIDTH)
        gq = qkg_ref[2 * kind:2 * kind + 1, :]
        gk = qkg_ref[2 * kind + 1:2 * kind + 2, :]
        for ci in range(Q_WIDTH // LANES):
            qc = qk_norm_rope(p[:, col0 + ci * LANES:col0 + (ci + 1) * LANES], gq)
            q_ref[0, :, kind * Q_WIDTH + ci * LANES:kind * Q_WIDTH + (ci + 1) * LANES] = (
                qc * (HEAD_DIM ** -0.5 * LOG2E)).astype(BF16)
        kc = qk_norm_rope(p[:, col0 + Q_WIDTH:col0 + Q_WIDTH + KV_WIDTH], gk)
        put_kv(kind, 0, kc, 0.0)
        put_kv(kind, 2, p[:, col0 + Q_WIDTH + KV_WIDTH:col0 + Q_WIDTH + 2 * KV_WIDTH], 1.0)


def _mod_specs(layer):
    return [pl.BlockSpec((None, None, N_ADA, D_MODEL), lambda b, t, *_: (layer, b, 0, 0)),
            pl.BlockSpec((None, None, N_ADA, D_MODEL), lambda b, t, *_: (layer, BATCH, 0, 0))]


def _step_spec(width=D_MODEL):
    return pl.BlockSpec((1, STEP_ROWS, width), lambda b, t, *_: (b, t, 0))


def _const_spec(shape):
    return pl.BlockSpec(shape, lambda b, t, *_: (0,) * len(shape), pipeline_mode=pl.Buffered(1))


class _SplitStream:
    def __init__(self, x_ref, ctx_ref):
        self.x_ref, self.ctx_ref = x_ref, ctx_ref

    def __getitem__(self, idx):
        _, rows, cols = idx
        assert (rows.stop - rows.start) == ROW_TILE and rows.start % ROW_TILE == 0
        if rows.start != (TILES_PER_STEP - 1) * ROW_TILE:
            return self.x_ref[0, rows, cols]
        last_step = pl.program_id(1) == STEPS_PER_SAMPLE - 1
        return jnp.where(last_step, self.ctx_ref[0, :, cols], self.x_ref[0, rows, cols])


def _split_stream_body(body):
    return lambda x_ref, ctx_ref, *refs: body(_SplitStream(x_ref, ctx_ref), *refs)


def _ctx_spec():
    return pl.BlockSpec((1, CTX_LEN, D_MODEL), lambda b, t, *_: (b, 0, 0))


class _Projection:
    def __init__(self, name, body, in_specs, args, out_specs, out_shape):
        self.name, self.body, self.in_specs, self.args = name, body, in_specs, args
        self.out_specs, self.out_shape = out_specs, out_shape

    def __call__(self, x, ctx):
        return pl.pallas_call(
            _split_stream_body(self.body),
            grid=(BATCH, STEPS_PER_SAMPLE),
            in_specs=[_step_spec(), _ctx_spec()] + self.in_specs,
            out_specs=self.out_specs,
            out_shape=self.out_shape,
            compiler_params=pltpu.CompilerParams(vmem_limit_bytes=VMEM_LIMIT),
            name=self.name,
        )(x, ctx, *self.args)


def _attn_in(mod, layer, norm_g, w_in, qk_g, cos_t, sin_t):
    return _Projection(
        "attn_in", _attn_in_kernel,
        in_specs=[
            *_mod_specs(layer),
            _const_spec((1, D_MODEL)),
            _const_spec((D_MODEL, ATT_IN_WIDTH)),
            _const_spec((4, LANES)),
            pl.BlockSpec((STEP_ROWS, LANES), lambda b, t, *_: (t, 0)),
            pl.BlockSpec((STEP_ROWS, LANES), lambda b, t, *_: (t, 0)),
        ],
        args=[mod, mod, norm_g, w_in, qk_g, cos_t, sin_t],
        out_specs=[
            _step_spec(2 * Q_WIDTH),
            pl.BlockSpec((1, 2, N_KV_HEADS, 4, STEP_ROWS, LANES), lambda b, t, *_: (b, 0, 0, 0, t, 0)),
        ],
        out_shape=[
            jax.ShapeDtypeStruct((BATCH, TOK, 2 * Q_WIDTH), BF16),
            jax.ShapeDtypeStruct((BATCH, 2, N_KV_HEADS, 4, TOK, LANES), BF16),
        ])


def _softmax_pv(parts, bounded, sink=None):
    shift = None
    if not bounded:
        for s, _ in parts:
            mx = jnp.max(s, axis=-1, keepdims=True)
            shift = mx if shift is None else jnp.maximum(shift, mx)
        if sink is not None:
            shift = jnp.maximum(shift, sink)
    shifted = (lambda z: z) if bounded else (lambda z: z - shift)
    acc = None
    for s, v in parts:
        pv = _dot(jnp.exp2(shifted(s)).astype(BF16), v)
        acc = pv if acc is None else acc + pv
    return acc, (jnp.exp2(shifted(sink)) if sink is not None else None)


def _attn_body(par_ref, q_ref, kv_ref, o_ref, bounded):
    t = pl.program_id(1)

    def heads(kind, g, r0, n, parts_of, with_sink):
        c0 = kind * Q_WIDTH + g * 2 * LANES
        lhs = jnp.concatenate([q_ref[0, r0:r0 + n, c0:c0 + LANES],
                               q_ref[0, r0:r0 + n, c0 + LANES:c0 + 2 * LANES]], axis=0)
        first_pair = lax.broadcasted_iota(I32, (2 * n, 1), 0) < n
        outs = []
        for parity in range(2):
            sink = None
            if with_sink:
                sink = jnp.where(first_pair, par_ref[4 * g + parity], par_ref[4 * g + 2 + parity])
            acc, sink_term = _softmax_pv(parts_of(lhs, parity), bounded, sink)
            denom = acc[:, (1 - parity) * HEAD_DIM:(1 - parity) * HEAD_DIM + 1]
            if sink_term is not None:
                denom = denom + sink_term
            outs.append(acc * (1.0 / denom))
        low = lax.broadcasted_iota(I32, (2 * n, LANES), 1) < HEAD_DIM
        o = jnp.where(low, outs[0], outs[1])
        o_ref[0, r0:r0 + n, c0:c0 + LANES] = o[:n].astype(BF16)
        o_ref[0, r0:r0 + n, c0 + LANES:c0 + 2 * LANES] = o[n:].astype(BF16)

    @pl.when(t < N_LAT_Q_TILES)
    def _latent():
        for g in range(N_KV_HEADS):
            heads(0, g, 0, Q_TILE, lambda lhs, parity: [(_dot_nt(lhs, kv_ref[0, 0, g, parity]),
                                                         kv_ref[0, 0, g, 2 + parity])], False)

        for sub in range(Q_TILE // WIN_Q_ROWS):
            q0 = t * Q_TILE + sub * WIN_Q_ROWS
            ks = pl.multiple_of(jnp.clip(q0 - WINDOW, 0, SEQ - WIN_SPAN), WINDOW)
            qi = lax.broadcasted_iota(I32, (2 * WIN_Q_ROWS, WIN_SPAN), 0) & (WIN_Q_ROWS - 1)
            kj = lax.broadcasted_iota(I32, (2 * WIN_Q_ROWS, WIN_SPAN), 1)
            band = jnp.abs(qi - kj + (q0 - ks)) <= WINDOW

            def window_parts(g):
                def parts_of(lhs, parity):
                    s_w = _dot_nt(lhs, kv_ref[0, 1, g, parity, pl.ds(ks, WIN_SPAN), :])
                    s_c = _dot_nt(lhs, kv_ref[0, 1, g, parity, SEQ:TOK, :])
                    return [(jnp.where(band, s_w, NEG_INF),
                             kv_ref[0, 1, g, 2 + parity, pl.ds(ks, WIN_SPAN), :]),
                            (s_c, kv_ref[0, 1, g, 2 + parity, SEQ:TOK, :])]
                return parts_of

            for g in range(N_KV_HEADS):
                heads(1, g, sub * WIN_Q_ROWS, WIN_Q_ROWS, window_parts(g), True)

    @pl.when(t >= N_LAT_Q_TILES)
    def _context():
        for kind in range(2):
            for g in range(N_KV_HEADS):
                heads(kind, g, 0, Q_TILE,
                      lambda lhs, parity: [(_dot_nt(lhs, kv_ref[0, kind, g, parity, SEQ:TOK, :]),
                                            kv_ref[0, kind, g, 2 + parity, SEQ:TOK, :])],
                      kind == 1)


def _attn_kernel(par_ref, q_ref, kv_ref, o_ref):
    bounded = par_ref[PAR_BOUNDED] > 0.5

    @pl.when(bounded)
    def _bounded():
        _attn_body(par_ref, q_ref, kv_ref, o_ref, True)

    @pl.when(jnp.logical_not(bounded))
    def _exact_max():
        _attn_body(par_ref, q_ref, kv_ref, o_ref, False)


def _attention_params(qg_a, kg_a, qg_b, kg_b, sink):
    def bound(qg, kg):
        return 1.02 * LOG2E * HEAD_DIM ** 0.5 * jnp.max(jnp.abs(qg)) * jnp.max(jnp.abs(kg))
    sink2 = sink * LOG2E
    largest = jnp.maximum(jnp.maximum(bound(qg_a, kg_a), bound(qg_b, kg_b)), jnp.max(jnp.abs(sink2)))
    return jnp.concatenate([sink2, (largest < MAX_ABS_LOGIT).astype(F32)[None],
                            jnp.zeros((PAR_LEN - PAR_BOUNDED - 1,), F32)])


def _attention(q, kv, params, n_q_tiles):
    return pl.pallas_call(
        _attn_kernel,
        grid=(BATCH, n_q_tiles),
        in_specs=[
            pl.BlockSpec(memory_space=pltpu.SMEM),
            pl.BlockSpec((1, Q_TILE, 2 * Q_WIDTH), lambda b, t: (b, t, 0)),
            pl.BlockSpec((1, 2, N_KV_HEADS, 4, TOK, LANES), lambda b, t: (b, 0, 0, 0, 0, 0)),
        ],
        out_specs=pl.BlockSpec((1, Q_TILE, 2 * Q_WIDTH), lambda b, t: (b, t, 0)),
        out_shape=jax.ShapeDtypeStruct((BATCH, TOK, 2 * Q_WIDTH), BF16),
        compiler_params=pltpu.CompilerParams(vmem_limit_bytes=VMEM_LIMIT),
        name="attention",
    )(params, q, kv)


def _conv_in_kernel(x_ref, mod_ref, modc_ref, g_ref, w_ref, u_ref, bg_ref):
    h = _modulated_norm(x_ref, g_ref, _tile_mods(mod_ref, modc_ref), 0, 1)
    bg_ref[0] = _dot(h, w_ref[:, 0:D_MODEL]).astype(BF16)
    cg = _dot(h, w_ref[:, D_MODEL:2 * D_MODEL])
    v = _dot(h, w_ref[:, 2 * D_MODEL:3 * D_MODEL])
    u_ref[0] = (cg * v).astype(BF16)


def _conv_in(mod, layer, norm_g, w_in):
    return _Projection(
        "conv_in", _conv_in_kernel,
        in_specs=[*_mod_specs(layer), _const_spec((1, D_MODEL)), _const_spec((D_MODEL, 3 * D_MODEL))],
        args=[mod, mod, norm_g, w_in],
        out_specs=[_step_spec(), _step_spec()],
        out_shape=[
            jax.ShapeDtypeStruct((BATCH, TOK, D_MODEL), BF16),
            jax.ShapeDtypeStruct((BATCH, TOK, D_MODEL), BF16),
        ])


def _tail(lhs, w_ref, x_ref, mod_ref, modc_ref, g_ref, rw_ref, x1_ref, h2_ref, aff_ref):
    h2_hi, h2_lo = [], []
    for k, m in enumerate(_tile_mods(mod_ref, modc_ref)):
        x1 = x_ref[0, _tile_rows(k), :] + m[2:3, :] * _dot(lhs[_tile_rows(k)], w_ref[...])
        x1_ref[0, _tile_rows(k), :] = x1
        h2 = _rmsnorm_rows(x1, g_ref[...])
        h2 = h2 * (1.0 + m[4:5, :]) + m[3:4, :]
        hi = h2.astype(BF16)
        h2_ref[0, _tile_rows(k), :] = hi
        h2_hi.append(hi)
        h2_lo.append((h2 - hi.astype(F32)).astype(BF16))
    rw = rw_ref[...]
    part = _dot_nt(rw, jnp.concatenate(h2_hi, axis=0))
    logits = (part[0:N_EXPERTS] + part[N_EXPERTS:2 * N_EXPERTS]
              + _dot_nt(rw[0:N_EXPERTS], jnp.concatenate(h2_lo, axis=0)))
    e = jnp.exp(logits - jnp.max(logits, axis=0, keepdims=True))
    aff_ref[0] = e / jnp.sum(e, axis=0, keepdims=True)


def _attn_out_kernel(o_ref, w_ref, x_ref, mod_ref, modc_ref, g_ref, rw_ref, x1_ref, h2_ref, aff_ref):
    _tail(o_ref[0], w_ref, x_ref, mod_ref, modc_ref, g_ref, rw_ref, x1_ref, h2_ref, aff_ref)


def _conv_out_kernel(u_ref, up_ref, un_ref, bg_ref, ck_ref, cb_ref, w_ref, x_ref, mod_ref, modc_ref,
                     g_ref, rw_ref, x1_ref, h2_ref, aff_ref):
    u = u_ref[0].astype(F32)
    row = lax.broadcasted_iota(I32, (STEP_ROWS, 1), 0)
    pos = row + pl.program_id(1) * STEP_ROWS
    seq_first = jnp.logical_or(pos == 0, pos == SEQ)
    seq_last = jnp.logical_or(pos == SEQ - 1, pos == TOK - 1)
    u_prev = jnp.where(row == 0, up_ref[0, HALO - 1:HALO, :].astype(F32), pltpu.roll(u, 1, axis=0))
    u_next = jnp.where(row == STEP_ROWS - 1, un_ref[0, 0:1, :].astype(F32),
                       pltpu.roll(u, STEP_ROWS - 1, axis=0))
    u_prev = jnp.where(seq_first, 0.0, u_prev)
    u_next = jnp.where(seq_last, 0.0, u_next)
    y = ck_ref[0:1, :] * u_prev + ck_ref[1:2, :] * u + ck_ref[2:3, :] * u_next + cb_ref[...]
    lhs = (bg_ref[0].astype(F32) * y).astype(BF16)
    _tail(lhs, w_ref, x_ref, mod_ref, modc_ref, g_ref, rw_ref, x1_ref, h2_ref, aff_ref)


def _attn_out_split_kernel(o_ref, w_ref, x_ref, ctx_ref, *refs):
    _attn_out_kernel(o_ref, w_ref, _SplitStream(x_ref, ctx_ref), *refs)


def _mixer_out(xs, mod, layer, w_out, norm_g, rw_split, attn_o=None, conv=None, ctx=None):
    step = pl.BlockSpec((1, STEP_ROWS, D_MODEL), lambda b, t: (b, t, 0))
    common_specs = [
        pl.BlockSpec((D_MODEL, D_MODEL), lambda b, t: (0, 0)),
        step,
        *([] if ctx is None else [_ctx_spec()]),
        *_mod_specs(layer),
        pl.BlockSpec((1, D_MODEL), lambda b, t: (0, 0)),
        pl.BlockSpec((2 * N_EXPERTS, D_MODEL), lambda b, t: (0, 0)),
    ]
    stream = [xs] if ctx is None else [xs, ctx]
    if conv is None:
        body = _attn_out_kernel if ctx is None else _attn_out_split_kernel
        specs, args = [step], [attn_o]
    else:
        assert ctx is None
        u, bg, ck, cb = conv
        halo_per_step = STEP_ROWS // HALO
        last_halo = TOK // HALO - 1
        body = _conv_out_kernel
        specs = [
            step,
            pl.BlockSpec((1, HALO, D_MODEL), lambda b, t: (b, jnp.maximum(t * halo_per_step - 1, 0), 0)),
            pl.BlockSpec((1, HALO, D_MODEL),
                         lambda b, t: (b, jnp.minimum((t + 1) * halo_per_step, last_halo), 0)),
            step,
            pl.BlockSpec((3, D_MODEL), lambda b, t: (0, 0)),
            pl.BlockSpec((1, D_MODEL), lambda b, t: (0, 0)),
        ]
        args = [u, u, u, bg, ck, cb]
    return pl.pallas_call(
        body,
        grid=(BATCH, STEPS_PER_SAMPLE),
        in_specs=specs + common_specs,
        out_specs=[
            step,
            step,
            pl.BlockSpec((1, N_EXPERTS, STEP_ROWS), lambda b, t: (b, 0, t)),
        ],
        out_shape=[
            jax.ShapeDtypeStruct((BATCH, TOK, D_MODEL), F32),
            jax.ShapeDtypeStruct((BATCH, TOK, D_MODEL), BF16),
            jax.ShapeDtypeStruct((BATCH, N_EXPERTS, TOK), F32),
        ],
        compiler_params=pltpu.CompilerParams(vmem_limit_bytes=VMEM_LIMIT),
        name="mixer_out",
    )(*args, w_out, *stream, mod, mod, norm_g, rw_split)


def _select_slots(v, k):
    rows, n = v.shape
    u = pltpu.bitcast(v, I32)

    def body(i, thr):
        cand = thr | jnp.left_shift(jnp.int32(1), 30 - i)
        cnt = jnp.sum(jnp.where(u >= cand, 1.0, 0.0), axis=1, keepdims=True)
        return jnp.where(cnt >= k, cand, thr)

    thr = lax.fori_loop(0, 31, body, jnp.zeros((rows, 1), I32))
    gt = u > thr
    eq = u == thr
    need = k - jnp.sum(jnp.where(gt, 1.0, 0.0), axis=1, keepdims=True)
    r = lax.broadcasted_iota(I32, (PREFIX_CHUNK, PREFIX_CHUNK), 0)
    c = lax.broadcasted_iota(I32, (PREFIX_CHUNK, PREFIX_CHUNK), 1)
    before = jnp.where(r < c, 1.0, 0.0).astype(BF16)

    def prefix_count(flags):
        out, carry = [], jnp.zeros((rows, 1), F32)
        for j in range(n // PREFIX_CHUNK):
            chunk = flags[:, j * PREFIX_CHUNK:(j + 1) * PREFIX_CHUNK]
            out.append(_dot(chunk.astype(BF16), before) + carry)
            carry = carry + jnp.sum(chunk, axis=1, keepdims=True)
        return jnp.concatenate(out, axis=1)

    eq_rank = prefix_count(jnp.where(eq, 1.0, 0.0))
    sel = jnp.where(gt, 1.0, jnp.where(eq, jnp.where(eq_rank < need, 1.0, 0.0), 0.0))
    slot = prefix_count(sel)
    return jnp.where(sel > 0.5, slot, -1.0).astype(I32), sel


def _topk_kernel(a_ref, pos_ref, starts_ref):
    pos_l, sel_l = _select_slots(a_ref[:, 0:SEQ], CAP_LAT)
    pos_c, _ = _select_slots(a_ref[:, SEQ:TOK], CAP_CTX)
    pos_ref[:, 0:SEQ] = pos_l
    pos_ref[:, SEQ:TOK] = pos_c
    tok = lax.broadcasted_iota(I32, (SEQ, LANES), 0)
    tile = lax.broadcasted_iota(I32, (SEQ, LANES), 1)
    earlier = jnp.where(tok < tile * ROW_TILE, 1.0, 0.0).astype(BF16)
    starts_ref[...] = _dot(sel_l.astype(BF16), earlier).astype(I32)


def _topk(aff_t):
    rows = BATCH * N_EXPERTS
    return pl.pallas_call(
        _topk_kernel,
        out_shape=[jax.ShapeDtypeStruct((rows, TOK), I32), jax.ShapeDtypeStruct((rows, LANES), I32)],
        compiler_params=pltpu.CompilerParams(vmem_limit_bytes=VMEM_LIMIT),
        name="topk",
    )(aff_t.reshape(rows, TOK))


def _slot_window(starts_ref, b, e, t):
    base = (b * N_EXPERTS + e) * STARTS_STRIDE + t
    s = starts_ref[base]
    n = starts_ref[base + 1] - s
    a = jnp.minimum(s - (s & (SLOT_ALIGN - 1)), CAP_LAT - WIN_SLOTS)
    return pl.multiple_of(a, SLOT_ALIGN), (s - a + n) <= WIN_SLOTS


def _all_windows(starts_ref, b, t):
    wins = [_slot_window(starts_ref, b, e, t) for e in range(N_EXPERTS)]
    fits = functools.reduce(jnp.logical_and, [ok for _, ok in wins])
    return [a for a, _ in wins], fits


def _moe_gather_kernel(starts_ref, pos_ref, h_ref, xl_ref, xc_ref):
    b = pl.program_id(0)
    n_lat = SEQ // ROW_TILE
    wins = [_all_windows(starts_ref, b, k) for k in range(n_lat)]
    lat_fit = functools.reduce(jnp.logical_and, [fits for _, fits in wins])

    def tile_pos(k, e):
        return pos_ref[0, e:e + 1, _tile_rows(k)]

    def windowed(k):
        starts = wins[k][0]
        slot = lax.broadcasted_iota(I32, (WIN_SLOTS, ROW_TILE), 0)
        picks = [jnp.where(slot == tile_pos(k, e) - starts[e], 1.0, 0.0).astype(BF16)
                 for e in range(N_EXPERTS)]
        y = _dot(jnp.concatenate(picks, axis=0), h_ref[0, _tile_rows(k), :]).astype(BF16)
        for e in range(N_EXPERTS):
            xl_ref[0, e, pl.ds(starts[e], WIN_SLOTS), :] += y[e * WIN_SLOTS:(e + 1) * WIN_SLOTS]

    def full(k):
        slot = lax.broadcasted_iota(I32, (CAP_LAT, ROW_TILE), 0)
        for e in range(N_EXPERTS):
            pick = jnp.where(slot == tile_pos(k, e), 1.0, 0.0).astype(BF16)
            xl_ref[0, e] += _dot(pick, h_ref[0, _tile_rows(k), :]).astype(BF16)

    def context(k):
        slot = lax.broadcasted_iota(I32, (CAP_CTX, ROW_TILE), 0)
        picks = [jnp.where(slot == tile_pos(k, e), 1.0, 0.0).astype(BF16) for e in range(N_EXPERTS)]
        y = _dot(jnp.concatenate(picks, axis=0), h_ref[0, _tile_rows(k), :]).astype(BF16)
        for e in range(N_EXPERTS):
            xc_ref[0, e] = y[e * CAP_CTX:(e + 1) * CAP_CTX]

    xl_ref[...] = jnp.zeros(xl_ref.shape, BF16)

    @pl.when(lat_fit)
    def _windowed_tiles():
        for k in range(n_lat):
            windowed(k)

    @pl.when(jnp.logical_not(lat_fit))
    def _full_tiles():
        for k in range(n_lat):
            full(k)

    context(n_lat)


def _moe_gather(starts, pos, h2):
    return pl.pallas_call(
        _moe_gather_kernel,
        grid_spec=pltpu.PrefetchScalarGridSpec(
            num_scalar_prefetch=1,
            grid=(BATCH,),
            in_specs=[
                pl.BlockSpec((1, N_EXPERTS, TOK), lambda b, s: (b, 0, 0)),
                pl.BlockSpec((1, TOK, D_MODEL), lambda b, s: (b, 0, 0)),
            ],
            out_specs=[
                pl.BlockSpec((1, N_EXPERTS, CAP_LAT, D_MODEL), lambda b, s: (b, 0, 0, 0)),
                pl.BlockSpec((1, N_EXPERTS, CAP_CTX, D_MODEL), lambda b, s: (b, 0, 0, 0)),
            ],
        ),
        out_shape=[
            jax.ShapeDtypeStruct((BATCH, N_EXPERTS, CAP_LAT, D_MODEL), BF16),
            jax.ShapeDtypeStruct((BATCH, N_EXPERTS, CAP_CTX, D_MODEL), BF16),
        ],
        compiler_params=pltpu.CompilerParams(vmem_limit_bytes=VMEM_LIMIT),
        name="moe_gather",
    )(starts, pos, h2)


def _moe_ffn_kernel(xl_ref, xc_ref, wg_ref, wu_ref, wd_ref, yl_ref, yc_ref):
    xe = jnp.concatenate([r[i, 0] for i in range(FFN_SAMPLES) for r in (xl_ref, xc_ref)], axis=0)
    hid = (_silu(_dot(xe, wg_ref[0].astype(BF16))) * _dot(xe, wu_ref[0].astype(BF16))).astype(BF16)
    ye = _dot(hid, wd_ref[0].astype(BF16)).astype(BF16)
    cap = CAP_LAT + CAP_CTX
    for i in range(FFN_SAMPLES):
        yl_ref[i, 0] = ye[i * cap:i * cap + CAP_LAT]
        yc_ref[i, 0] = ye[i * cap + CAP_LAT:(i + 1) * cap]


def _moe_ffn(xe_l, xe_c, layer, wg, wu, wd):
    wspec = pl.BlockSpec((None, 1, D_MODEL, D_MODEL), lambda e, b: (layer, e, 0, 0))
    lat = pl.BlockSpec((FFN_SAMPLES, 1, CAP_LAT, D_MODEL), lambda e, b: (b, e, 0, 0))
    ctx = pl.BlockSpec((FFN_SAMPLES, 1, CAP_CTX, D_MODEL), lambda e, b: (b, e, 0, 0))
    return pl.pallas_call(
        _moe_ffn_kernel,
        grid=(N_EXPERTS, BATCH // FFN_SAMPLES),
        in_specs=[lat, ctx, wspec, wspec, wspec],
        out_specs=[lat, ctx],
        out_shape=[
            jax.ShapeDtypeStruct((BATCH, N_EXPERTS, CAP_LAT, D_MODEL), BF16),
            jax.ShapeDtypeStruct((BATCH, N_EXPERTS, CAP_CTX, D_MODEL), BF16),
        ],
        compiler_params=pltpu.CompilerParams(vmem_limit_bytes=VMEM_LIMIT),
        name="moe_ffn",
    )(xe_l, xe_c, wg, wu, wd)


def _moe_combine_kernel(starts_ref, pos_ref, aff_ref, yl_ref, yc_ref, x_ref, mod_ref, modc_ref, o_ref,
                        g_ref, yw_ref, *, tiles, with_context):
    b = pl.program_id(0)
    step = pl.program_id(1)
    wins = [_all_windows(starts_ref, b, step * tiles + k) for k in range(tiles)]
    lat_fit = functools.reduce(jnp.logical_and, [fits for _, fits in wins[:-1]])
    all_fit = jnp.logical_and(lat_fit, wins[-1][1])

    def gates(g_ref, row0, cap, k, e, first_slot):
        slot = lax.broadcasted_iota(I32, (cap, ROW_TILE), 0)
        g_ref[row0 + e * cap:row0 + (e + 1) * cap, :] = jnp.where(
            slot == pos_ref[0, e:e + 1, _tile_rows(k)] - first_slot,
            aff_ref[0, e:e + 1, _tile_rows(k)], 0.0).astype(BF16)

    def finish(k, moe, m_ref):
        o_ref[0, _tile_rows(k), :] = x_ref[0, _tile_rows(k), :] + m_ref[5:6, :] * moe

    def windowed(k):
        starts = wins[k][0]
        base = k * N_EXPERTS * WIN_SLOTS
        for e in range(N_EXPERTS):
            gates(g_ref, base, WIN_SLOTS, k, e, starts[e])
            row0 = pl.multiple_of(e * CAP_LAT + starts[e], SLOT_ALIGN)
            yw_ref[base + e * WIN_SLOTS:base + (e + 1) * WIN_SLOTS, :] = yl_ref[0, pl.ds(row0, WIN_SLOTS), :]
        finish(k, _dot_tn(g_ref[base:base + N_EXPERTS * WIN_SLOTS, :],
                          yw_ref[base:base + N_EXPERTS * WIN_SLOTS, :]), mod_ref)

    def full(k):
        @pl.when(starts_ref[0] >= 0)
        def _one_tile():
            for e in range(N_EXPERTS):
                gates(g_ref, 0, CAP_LAT, k, e, 0)
            finish(k, _dot_tn(g_ref[...], yl_ref[0]), mod_ref)

    def context(k):
        base = N_EXPERTS * (CAP_LAT - CAP_CTX)
        for e in range(N_EXPERTS):
            gates(g_ref, base, CAP_CTX, k, e, 0)
        finish(k, _dot_tn(g_ref[base:base + N_EXPERTS * CAP_CTX, :], yc_ref[0]), modc_ref)

    last = step == pl.num_programs(1) - 1 if with_context else False
    not_last = jnp.logical_not(last) if with_context else True

    @pl.when(jnp.logical_and(not_last, all_fit))
    def _windowed_step():
        for k in range(tiles):
            windowed(k)

    @pl.when(jnp.logical_and(not_last, jnp.logical_not(all_fit)))
    def _full_step():
        for k in range(tiles):
            full(k)

    if with_context:
        @pl.when(jnp.logical_and(last, lat_fit))
        def _windowed_last_step():
            for k in range(tiles - 1):
                windowed(k)
            context(tiles - 1)

        @pl.when(jnp.logical_and(last, jnp.logical_not(lat_fit)))
        def _full_last_step():
            for k in range(tiles - 1):
                full(k)
            context(tiles - 1)


N_COMBINE_INPUTS = 8
N_COMBINE_SCRATCH = 2


def _combine_project_kernel(*refs, projection_body, n_proj_inputs):
    n_in = N_COMBINE_INPUTS + n_proj_inputs
    combine_in, proj_in = refs[:N_COMBINE_INPUTS], refs[N_COMBINE_INPUTS:n_in]
    x_out, proj_out = refs[n_in], refs[n_in + 1:len(refs) - N_COMBINE_SCRATCH]
    scratch = refs[len(refs) - N_COMBINE_SCRATCH:]
    _moe_combine_kernel(*combine_in, x_out, *scratch, tiles=TILES_PER_STEP, with_context=True)
    projection_body(x_out, *proj_in, *proj_out)


def _moe_combine(starts, pos, aff_t, ye_l, ye_c, x1, mod, layer, with_context, projection=None):
    tiles = TILES_PER_STEP if with_context else 4
    rows = TOK if with_context else SEQ
    step = pl.BlockSpec((1, tiles * ROW_TILE, D_MODEL), lambda b, t, s: (b, t, 0))
    sel = pl.BlockSpec((1, N_EXPERTS, tiles * ROW_TILE), lambda b, t, s: (b, 0, t))
    x_shape = jax.ShapeDtypeStruct((BATCH, rows, D_MODEL), F32)
    if projection is None:
        body = functools.partial(_moe_combine_kernel, tiles=tiles, with_context=with_context)
        extra_specs, extra_args, out_specs, out_shape = [], [], step, x_shape
    else:
        assert with_context and tiles * ROW_TILE == STEP_ROWS
        body = functools.partial(_combine_project_kernel, projection_body=projection.body,
                                 n_proj_inputs=len(projection.args))
        extra_specs, extra_args = projection.in_specs, projection.args
        out_specs, out_shape = [step] + projection.out_specs, [x_shape] + projection.out_shape
    return pl.pallas_call(
        body,
        grid_spec=pltpu.PrefetchScalarGridSpec(
            num_scalar_prefetch=1,
            grid=(BATCH, rows // (tiles * ROW_TILE)),
            in_specs=[
                sel, sel,
                pl.BlockSpec((1, N_EXPERTS * CAP_LAT, D_MODEL), lambda b, t, s: (b, 0, 0)),
                pl.BlockSpec((1, N_EXPERTS * CAP_CTX, D_MODEL), lambda b, t, s: (b, 0, 0)),
                step,
                *_mod_specs(layer),
            ] + extra_specs,
            out_specs=out_specs,
            scratch_shapes=[
                pltpu.VMEM((N_EXPERTS * CAP_LAT, ROW_TILE), BF16),
                pltpu.VMEM((tiles * N_EXPERTS * WIN_SLOTS, D_MODEL), BF16),
            ],
        ),
        out_shape=out_shape,
        compiler_params=pltpu.CompilerParams(vmem_limit_bytes=VMEM_LIMIT_FUSED),
        name="moe_combine" if projection is None else "moe_combine_" + projection.name,
    )(starts, pos, aff_t, ye_l.reshape(BATCH, N_EXPERTS * CAP_LAT, D_MODEL),
      ye_c.reshape(BATCH, N_EXPERTS * CAP_CTX, D_MODEL), x1, mod, mod, *extra_args)


def _rope_tables():
    rows = SEQ // GRID_W
    row = jnp.repeat(jnp.arange(rows, dtype=F32), GRID_W)
    col = jnp.tile(jnp.arange(GRID_W, dtype=F32), rows)
    axis_dim = HEAD_DIM // 2
    inv_freq = ROPE_THETA ** (-jnp.arange(0, axis_dim, 2, dtype=F32) / axis_dim)
    ang_r = row[:, None] * inv_freq[None, :]
    ang_c = col[:, None] * inv_freq[None, :]
    cos_h = jnp.concatenate([jnp.cos(ang_r), jnp.cos(ang_r), jnp.cos(ang_c), jnp.cos(ang_c)], axis=-1)
    sin_h = jnp.concatenate([-jnp.sin(ang_r), jnp.sin(ang_r), -jnp.sin(ang_c), jnp.sin(ang_c)], axis=-1)
    cos_t = jnp.concatenate([jnp.tile(cos_h, (1, 2)), jnp.ones((CTX_LEN, LANES), F32)], axis=0)
    sin_t = jnp.concatenate([jnp.tile(sin_h, (1, 2)), jnp.zeros((CTX_LEN, LANES), F32)], axis=0)
    return cos_t, sin_t


def kernel(x, c, ctx, c_ctx, ada_w, ada_b, norm1_g, norm2_g, attn_w_in, attn_w_out, qnorm_a, knorm_a,
           qnorm_b, knorm_b, sink_b, conv_w_in, conv_k, conv_b, conv_w_out, router_w, moe_w_gate,
           moe_w_up, moe_w_down):
    assert x.shape == (BATCH, SEQ, D_MODEL) and ctx.shape == (BATCH, CTX_LEN, D_MODEL)
    cos_t, sin_t = _rope_tables()
    mod_rows = 32
    c_all = jnp.concatenate([c, c_ctx[None, :], jnp.zeros((mod_rows - BATCH - 1, D_MODEL), F32)], axis=0)
    mod = _adaln(c_all, ada_w, ada_b).reshape(DEPTH, mod_rows, N_ADA, D_MODEL)
    def projection(i):
        j = i // 2
        g1 = norm1_g[i].reshape(1, D_MODEL)
        if i % 2 == 0:
            tile2 = lambda g: jnp.tile(g.reshape(1, HEAD_DIM), (1, 2))
            qk_g = jnp.concatenate([tile2(qnorm_a[j]), tile2(knorm_a[j]), tile2(qnorm_b[j]),
                                    tile2(knorm_b[j])], axis=0)
            return _attn_in(mod, i, g1, attn_w_in[j].astype(BF16), qk_g, cos_t, sin_t)
        return _conv_in(mod, i, g1, conv_w_in[j].astype(BF16))

    xs, first_ctx = x, ctx
    proj = projection(0)(x, ctx)
    for i in range(DEPTH):
        j = i // 2
        last = i == DEPTH - 1
        g2 = norm2_g[i].reshape(1, D_MODEL)
        rw_t = router_w[i].T
        rw_hi = rw_t.astype(BF16)
        rw_split = jnp.concatenate([rw_hi, (rw_t - rw_hi.astype(F32)).astype(BF16)], axis=0)
        if i % 2 == 0:
            q, kv = proj
            o = _attention(q, kv, _attention_params(qnorm_a[j], knorm_a[j], qnorm_b[j], knorm_b[j],
                                                    sink_b[j]), N_Q_TILES)
            x1, h2, aff_t = _mixer_out(xs, mod, i, attn_w_out[j].astype(BF16), g2, rw_split, attn_o=o,
                                       ctx=first_ctx if i == 0 else None)
        else:
            u, bg = proj
            x1, h2, aff_t = _mixer_out(xs, mod, i, conv_w_out[j].astype(BF16), g2, rw_split,
                                       conv=(u, bg, conv_k[j], conv_b[j].reshape(1, D_MODEL)))
        pos, starts = _topk(aff_t)
        pos = pos.reshape(BATCH, N_EXPERTS, TOK)
        starts = starts[:, :STARTS_STRIDE].reshape(-1)
        xe_l, xe_c = _moe_gather(starts, pos, h2)
        ye_l, ye_c = _moe_ffn(xe_l, xe_c, i, moe_w_gate, moe_w_up, moe_w_down)
        if last:
            return _moe_combine(starts, pos, aff_t, ye_l, ye_c, x1, mod, i, with_context=False)
        xs, *proj = _moe_combine(starts, pos, aff_t, ye_l, ye_c, x1, mod, i, with_context=True,
                                 projection=projection(i + 1))
```
